```python
import math
import jax
import jax.numpy as jnp
from jax import lax
import numpy as np

D_MODEL = 1024
BATCH = 2
SEQ = 8192
DEPTH = 2

N_MIXERS = 4
GROUP_WIDTH = D_MODEL // N_MIXERS
HEAD_DIM = 64
GROUP_HEADS = GROUP_WIDTH // HEAD_DIM
D_FF = 2816
SHORT_CONV = 4
CONF_KERNEL = 31
CONF_GROUPS = 4
GDN_CHUNK = 64
Q_BLOCK = 128
N_MEM = 256
MEM_HEADS = 4
MEM_HEAD_DIM = D_MODEL // MEM_HEADS
DN_ALPHA = float((2 * DEPTH) ** 0.25)
DN_INIT = float((8 * DEPTH) ** -0.25)
LN_EPS = 1e-5
RMS_EPS = 1e-6
L2_EPS = 1e-6
NEG_BIG = -1e30
IN_SPLITS = (3 * GROUP_WIDTH,
             GROUP_WIDTH,
             GROUP_HEADS,
             GROUP_HEADS,
             3 * GROUP_WIDTH,
             GROUP_HEADS,
             2 * GROUP_WIDTH,
             3 * GROUP_WIDTH)
IN_WIDTH = sum(IN_SPLITS)
IN_OFFSETS = tuple(int(o) for o in np.cumsum(IN_SPLITS)[:-1])

kernel_name = 'hybrid_headgroup_gdn_fox_conv_stickbreak'


def layer_norm(x, g, b):
    xf = x.astype(jnp.float32)
    mu = jnp.mean(xf, axis=-1, keepdims=True)
    var = jnp.mean(jnp.square(xf - mu), axis=-1, keepdims=True)
    y = (xf - mu) * lax.rsqrt(var + LN_EPS) * g.astype(jnp.float32) + b.astype(jnp.float32)
    return y.astype(x.dtype)


def swiglu(x, w_gate, w_up, w_down):
    return (jax.nn.silu(x @ w_gate) * (x @ w_up)) @ w_down


def causal_depthwise_conv(x, w):
    width, ch = w.shape
    xp = jnp.pad(x, ((0, 0), (width - 1, 0), (0, 0)))
    return lax.conv_general_dilated(xp, w[:, None, :].astype(x.dtype), window_strides=(1,), padding='VALID',
                                    dimension_numbers=('NWC', 'WIO', 'NWC'), feature_group_count=ch)


def to_heads(t):
    b, s, _ = t.shape
    return t.reshape(b, s, GROUP_HEADS, HEAD_DIM).transpose(0, 2, 1, 3)


def from_heads(t):
    b, h, s, d = t.shape
    return t.transpose(0, 2, 1, 3).reshape(b, s, h * d)


def l2_normalize(t):
    tf = t.astype(jnp.float32)
    return tf * lax.rsqrt(jnp.sum(tf * tf, axis=-1, keepdims=True) + L2_EPS)


def query_blocks(t):
    b, h, s = t.shape[:3]
    return jnp.moveaxis(t.reshape(b, h, s // Q_BLOCK, Q_BLOCK, *t.shape[3:]), 2, 0)


def merge_blocks(t):
    nb, b, h, qb, d = t.shape
    return jnp.moveaxis(t, 0, 2).reshape(b, h, nb * qb, d)


def gated_delta_rule_chunked(q, k, v, g, beta):
    f32 = jnp.float32
    b, h, s, dk = q.shape
    dv = v.shape[-1]
    c = GDN_CHUNK
    n = s // c
    q = q.astype(f32).reshape(b, h, n, c, dk) * (dk ** -0.5)
    k = k.astype(f32).reshape(b, h, n, c, dk)
    v = v.astype(f32).reshape(b, h, n, c, dv)
    beta = beta.astype(f32).reshape(b, h, n, c)
    g = jnp.cumsum(g.astype(f32).reshape(b, h, n, c), axis=-1)
    k_beta = k * beta[..., None]
    v_beta = v * beta[..., None]
    lower_incl = jnp.tril(jnp.ones((c, c), dtype=bool))
    strict_lower = jnp.tril(jnp.ones((c, c), dtype=bool), -1)
    decay = jnp.exp(jnp.where(lower_incl, g[..., :, None] - g[..., None, :], -jnp.inf))
    lkk = jnp.where(strict_lower, jnp.einsum('bhncd,bhnsd->bhncs', k_beta, k) * decay, 0.0)
    eye = jnp.eye(c, dtype=f32)
    t_inv = lax.linalg.triangular_solve(eye + lkk, jnp.broadcast_to(eye, lkk.shape), left_side=True, lower=True)
    u = jnp.einsum('bhncs,bhnsv->bhncv', t_inv, v_beta)
    w = jnp.einsum('bhncs,bhnsd->bhncd', t_inv, k_beta * jnp.exp(g)[..., None])
    a_qk = jnp.einsum('bhncd,bhnsd->bhncs', q, k) * decay
    g_last = g[..., -1]
    q_dec = q * jnp.exp(g)[..., None]
    k_dec = k * jnp.exp(g_last[..., None] - g)[..., None]

    def chunk_step(state, inp):
        q_c, k_c, u_c, w_c, a_c, gl_c = inp
        v_new = u_c - jnp.einsum('bhcd,bhdv->bhcv', w_c, state)
        o_c = jnp.einsum('bhcd,bhdv->bhcv', q_c, state) + jnp.einsum('bhcs,bhsv->bhcv', a_c, v_new)
        state = state * jnp.exp(gl_c)[..., None, None] + jnp.einsum('bhcd,bhcv->bhdv', k_c, v_new)
        return state, o_c

    xs = tuple(jnp.moveaxis(t, 2, 0) for t in (q_dec, k_dec, u, w, a_qk, g_last))
    state0 = jnp.zeros((b, h, dk, dv), f32)
    _, o = lax.scan(chunk_step, state0, xs)
    return jnp.moveaxis(o, 0, 2).reshape(b, h, s, dv)


def forgetting_attention(q, k, v, log_f):
    s_len, d = q.shape[2], q.shape[3]
    scale = d ** -0.5
    cum = jnp.cumsum(log_f.astype(jnp.float32), axis=-1)
    key_pos = jnp.arange(s_len)
    starts = jnp.arange(s_len // Q_BLOCK) * Q_BLOCK

    def block(args):
        q_blk, cum_blk, start = args
        logits = jnp.einsum('bhqd,bhkd->bhqk', q_blk, k).astype(jnp.float32) * scale
        logits = logits + cum_blk[..., :, None] - cum[..., None, :]
        causal = key_pos[None, :] <= (start + jnp.arange(Q_BLOCK))[:, None]
        probs = jax.nn.softmax(jnp.where(causal, logits, NEG_BIG), axis=-1)
        return jnp.einsum('bhqk,bhkd->bhqd', probs.astype(v.dtype), v)

    return merge_blocks(lax.map(block, (query_blocks(q), query_blocks(cum), starts)))


def stick_breaking_attention(q, k, v):
    s_len, d = q.shape[2], q.shape[3]
    scale = d ** -0.5
    key_pos = jnp.arange(s_len)
    starts = jnp.arange(s_len // Q_BLOCK) * Q_BLOCK

    def block(args):
        q_blk, start = args
        z = jnp.einsum('bhqd,bhkd->bhqk', q_blk, k).astype(jnp.float32) * scale
        strict = key_pos[None, :] < (start + jnp.arange(Q_BLOCK))[:, None]
        log_keep = jnp.where(strict, jax.nn.log_sigmoid(-z), 0.0)
        log_rest = lax.cumsum(log_keep, axis=3, reverse=True) - log_keep
        weights = jnp.where(strict, jnp.exp(jax.nn.log_sigmoid(z) + log_rest), 0.0)
        return jnp.einsum('bhqk,bhkd->bhqd', weights.astype(v.dtype), v)

    return merge_blocks(lax.map(block, (query_blocks(q), starts)))


def channel_group_norm(h, g, b):
    bsz, s, ch = h.shape
    hf = h.astype(jnp.float32).reshape(bsz, s, CONF_GROUPS, ch // CONF_GROUPS)
    mu = jnp.mean(hf, axis=-1, keepdims=True)
    var = jnp.mean(jnp.square(hf - mu), axis=-1, keepdims=True)
    hn = ((hf - mu) * lax.rsqrt(var + LN_EPS)).reshape(bsz, s, ch)
    return (hn * g.astype(jnp.float32) + b.astype(jnp.float32)).astype(h.dtype)


def parallel_head_group_mixers(x, w_in, gdn_conv_w, gdn_a_log, gdn_dt_bias, gdn_norm_g, fox_b_f,
                               conf_dw_w, conf_dw_b, conf_norm_g, conf_norm_b, w_out):
    f32 = jnp.float32
    proj = x @ w_in
    gdn_qkv, gdn_z, gdn_a, gdn_b, fox_qkv, fox_f, conf_glu, sb_qkv = jnp.split(proj, IN_OFFSETS, axis=-1)

    qkv = jax.nn.silu(causal_depthwise_conv(gdn_qkv, gdn_conv_w))
    q_a, k_a, v_a = (to_heads(t) for t in jnp.split(qkv, 3, axis=-1))
    beta = jax.nn.sigmoid(gdn_b.astype(f32)).transpose(0, 2, 1)
    log_decay = (-jnp.exp(gdn_a_log.astype(f32)) *
                 jax.nn.softplus(gdn_a.astype(f32) + gdn_dt_bias.astype(f32))).transpose(0, 2, 1)
    o_a = gated_delta_rule_chunked(l2_normalize(q_a), l2_normalize(k_a), v_a, log_decay, beta)
    o_a = o_a * lax.rsqrt(jnp.mean(o_a * o_a, axis=-1, keepdims=True) + RMS_EPS) * gdn_norm_g.astype(f32)
    y_a = (from_heads(o_a) * jax.nn.silu(gdn_z.astype(f32))).astype(x.dtype)

    q_b, k_b, v_b = (to_heads(t) for t in jnp.split(fox_qkv, 3, axis=-1))
    log_f = jax.nn.log_sigmoid(fox_f.astype(f32) + fox_b_f.astype(f32)).transpose(0, 2, 1)
    y_b = from_heads(forgetting_attention(q_b, k_b, v_b, log_f)).astype(x.dtype)

    val, gate = jnp.split(conf_glu, 2, axis=-1)
    c = causal_depthwise_conv(val * jax.nn.sigmoid(gate), conf_dw_w) + conf_dw_b
    y_c = jax.nn.silu(channel_group_norm(c, conf_norm_g, conf_norm_b)).astype(x.dtype)

    q_d, k_d, v_d = (to_heads(t) for t in jnp.split(sb_qkv, 3, axis=-1))
    y_d = from_heads(stick_breaking_attention(q_d, k_d, v_d)).astype(x.dtype)

    return jnp.concatenate([y_a, y_b, y_c, y_d], axis=-1) @ w_out


def memory_cross_attention(x, mem, w_q, w_kv, w_o):
    b, s, _ = x.shape
    m = mem.shape[1]
    q = (x @ w_q).reshape(b, s, MEM_HEADS, MEM_HEAD_DIM)
    k, v = jnp.split(mem @ w_kv, 2, axis=-1)
    k = k.reshape(b, m, MEM_HEADS, MEM_HEAD_DIM)
    v = v.reshape(b, m, MEM_HEADS, MEM_HEAD_DIM)
    scores = jnp.einsum('bthd,bmhd->bhtm', q, k).astype(jnp.float32) * (MEM_HEAD_DIM ** -0.5)
    probs = jax.nn.softmax(scores, axis=-1).astype(v.dtype)
    out = jnp.einsum('bhtm,bmhd->bthd', probs, v).reshape(b, s, D_MODEL)
    return out @ w_o


def setup_inputs(seed: int = 0) -> dict:
    key = jax.random.key(seed)
    keys = iter(jax.random.split(key, 40))
    f32 = jnp.float32
    L = DEPTH

    def normal(shape, scale):
        return jax.random.normal(next(keys), shape, f32) * scale

    def gain(shape):
        return 1.0 + normal(shape, 0.02)

    dt = jnp.exp(jax.random.uniform(next(keys), (L, GROUP_HEADS), f32, math.log(1e-3), math.log(1e-1)))
    gdn_dt_bias = dt + jnp.log(-jnp.expm1(-dt))
    gdn_a_log = jnp.log(jax.random.uniform(next(keys), (L, GROUP_HEADS), f32, 1.0, 16.0))
    d_in = D_MODEL ** -0.5
    return {
        'x': normal((BATCH, SEQ, D_MODEL), 1.0),
        'mem': normal((BATCH, N_MEM, D_MODEL), 1.0),
        'ffn1_w_gate': normal((L, D_MODEL, D_FF), d_in),
        'ffn1_w_up': normal((L, D_MODEL, D_FF), d_in),
        'ffn1_w_down': normal((L, D_FF, D_MODEL), DN_INIT * D_FF ** -0.5),
        'ln_ffn1_g': gain((L, D_MODEL)),
        'ln_ffn1_b': normal((L, D_MODEL), 0.02),
        'w_in': normal((L, D_MODEL, IN_WIDTH), d_in),
        'gdn_conv_w': normal((L, SHORT_CONV, 3 * GROUP_WIDTH), SHORT_CONV ** -0.5),
        'gdn_a_log': gdn_a_log,
        'gdn_dt_bias': gdn_dt_bias,
        'gdn_norm_g': gain((L, HEAD_DIM)),
        'fox_b_f': 3.0 + normal((L, GROUP_HEADS), 0.1),
        'conf_dw_w': normal((L, CONF_KERNEL, GROUP_WIDTH), CONF_KERNEL ** -0.5),
        'conf_dw_b': normal((L, GROUP_WIDTH), 0.02),
        'conf_norm_g': gain((L, GROUP_WIDTH)),
        'conf_norm_b': normal((L, GROUP_WIDTH), 0.02),
        'w_out': normal((L, D_MODEL, D_MODEL), DN_INIT * d_in),
        'ln_mix_g': gain((L, D_MODEL)),
        'ln_mix_b': normal((L, D_MODEL), 0.02),
        'mem_w_q': normal((L, D_MODEL, D_MODEL), d_in),
        'mem_w_kv': normal((L, D_MODEL, 2 * D_MODEL), d_in),
        'mem_w_o': normal((L, D_MODEL, D_MODEL), DN_INIT * d_in),
        'ln_mem_g': gain((L, D_MODEL)),
        'ln_mem_b': normal((L, D_MODEL), 0.02),
        'ffn2_w_gate': normal((L, D_MODEL, D_FF), d_in),
        'ffn2_w_up': normal((L, D_MODEL, D_FF), d_in),
        'ffn2_w_down': normal((L, D_FF, D_MODEL), DN_INIT * D_FF ** -0.5),
        'ln_ffn2_g': gain((L, D_MODEL)),
        'ln_ffn2_b': normal((L, D_MODEL), 0.02),
    }


def reference(x, mem, ffn1_w_gate, ffn1_w_up, ffn1_w_down, ln_ffn1_g, ln_ffn1_b,
              w_in, gdn_conv_w, gdn_a_log, gdn_dt_bias, gdn_norm_g, fox_b_f,
              conf_dw_w, conf_dw_b, conf_norm_g, conf_norm_b, w_out, ln_mix_g, ln_mix_b,
              mem_w_q, mem_w_kv, mem_w_o, ln_mem_g, ln_mem_b,
              ffn2_w_gate, ffn2_w_up, ffn2_w_down, ln_ffn2_g, ln_ffn2_b):
    for i in range(DEPTH):
        x = layer_norm(DN_ALPHA * x + 0.5 * swiglu(x, ffn1_w_gate[i], ffn1_w_up[i], ffn1_w_down[i]),
                       ln_ffn1_g[i], ln_ffn1_b[i])
        mix = parallel_head_group_mixers(x, w_in[i], gdn_conv_w[i], gdn_a_log[i], gdn_dt_bias[i], gdn_norm_g[i],
                                         fox_b_f[i], conf_dw_w[i], conf_dw_b[i], conf_norm_g[i], conf_norm_b[i],
                                         w_out[i])
        x = layer_norm(DN_ALPHA * x + mix, ln_mix_g[i], ln_mix_b[i])
        x = layer_norm(DN_ALPHA * x + memory_cross_attention(x, mem, mem_w_q[i], mem_w_kv[i], mem_w_o[i]),
                       ln_mem_g[i], ln_mem_b[i])
        x = layer_norm(DN_ALPHA * x + 0.5 * swiglu(x, ffn2_w_gate[i], ffn2_w_up[i], ffn2_w_down[i]),
                       ln_ffn2_g[i], ln_ffn2_b[i])
    return x
```

```python
import functools
import math

import jax
import jax.numpy as jnp
from jax import lax
from jax.experimental import pallas as pl
from jax.experimental.pallas import tpu as pltpu

F32 = jnp.float32
BF16 = jnp.bfloat16

D_MODEL = 1024
DEPTH = 2
GROUP_WIDTH = 256
HEAD_DIM = 64
GROUP_HEADS = 4
D_FF = 2816
SHORT_CONV = 4
CONF_KERNEL = 31
CONF_GROUPS = 4
GDN_CHUNK = 64
N_MEM = 256
MEM_HEADS = 4
MEM_HEAD_DIM = D_MODEL // MEM_HEADS
DN_ALPHA = float((2 * DEPTH) ** 0.25)
LN_EPS = 1e-5
RMS_EPS = 1e-6
L2_EPS = 1e-6
NEG_BIG = -1e30

GATE_LANES = 128
GATE_A0, GATE_B0, GATE_F0 = 0, 4, 8

TOKEN_TILE = 512
ATTN_BLOCK = 256
GDN_BLOCK = 256
CONV_BLOCK = 512
VMEM_LIMIT = 56 * 1024 * 1024


def _params(sem, vmem=VMEM_LIMIT):
    return pltpu.CompilerParams(dimension_semantics=sem, vmem_limit_bytes=vmem)


def _resident(shape):
    nd = len(shape)
    return pl.BlockSpec(shape, lambda *_: (0,) * nd, pipeline_mode=pl.Buffered(1))


def _layer_norm(y, g, b):
    mu = jnp.mean(y, axis=-1, keepdims=True)
    d = y - mu
    var = jnp.mean(d * d, axis=-1, keepdims=True)
    return d * lax.rsqrt(var + LN_EPS) * g + b


def _sigmoid(x):
    return 1.0 / (1.0 + jnp.exp(-x))


def _silu(x):
    return x * _sigmoid(x)


def _softplus(x):
    return jnp.maximum(x, 0.0) + jnp.log1p(jnp.exp(-jnp.abs(x)))


def _dot(a, b):
    return jnp.dot(a, b, preferred_element_type=F32)


def _dot_nt(a, b):
    return lax.dot_general(a, b, (((1,), (1,)), ((), ())), preferred_element_type=F32)


def _dot_tn(a, b):
    return lax.dot_general(a, b, (((0,), (0,)), ((), ())), preferred_element_type=F32)


def _dot_f32(a, b):
    return jnp.dot(a, b, preferred_element_type=F32, precision=lax.Precision.HIGHEST)


def _ffn_kernel(x_ref, wg_ref, wu_ref, wd_ref, g_ref, b_ref, o_ref, *, n_split):
    x = x_ref[0]
    xb = x.astype(BF16)
    fc = D_FF // n_split
    acc = None
    for c in range(n_split):
        h = _dot(xb, wg_ref[:, c * fc:(c + 1) * fc])
        u = _dot(xb, wu_ref[:, c * fc:(c + 1) * fc])
        a = (_silu(h) * u).astype(BF16)
        part = _dot(a, wd_ref[c * fc:(c + 1) * fc, :])
        acc = part if acc is None else acc + part
    o_ref[0] = _layer_norm(DN_ALPHA * x + 0.5 * acc, g_ref[...], b_ref[...])


def _ffn_ln(x, wg, wu, wd, g, b):
    bsz, t, d = x.shape
    tm = min(TOKEN_TILE, t)
    row = pl.BlockSpec((1, tm, d), lambda i, j: (i, j, 0))
    return pl.pallas_call(
        functools.partial(_ffn_kernel, n_split=2),
        grid=(bsz, t // tm),
        in_specs=[row, _resident(wg.shape), _resident(wu.shape), _resident(wd.shape),
                  _resident((1, d)), _resident((1, d))],
        out_specs=row,
        out_shape=jax.ShapeDtypeStruct(x.shape, F32),
        compiler_params=_params(("parallel", "parallel")),
        name="ffn_ln",
    )(x, wg, wu, wd, g.reshape(1, d), b.reshape(1, d))


N_HEAD_GROUPS_F32 = 4
N_HEAD_GROUPS_BF16 = 6


def _split_heads(r, o_ref, scale=None):
    for h in range(GROUP_HEADS):
        piece = r[:, h * HEAD_DIM:(h + 1) * HEAD_DIM]
        if scale is not None:
            piece = piece * scale
        o_ref[0, h] = piece.astype(o_ref.dtype)


def _inproj_kernel(x_ref, w_ref, gq_ref, gk_ref, gv_ref, gz_ref, fq_ref, fk_ref, fv_ref,
                   glu_ref, sq_ref, sk_ref, sv_ref, gate_ref):
    xb = x_ref[0].astype(BF16)
    gw = GROUP_WIDTH

    def group(i, width=gw):
        return _dot(xb, w_ref[:, i * gw:i * gw + width])

    qk_scale = HEAD_DIM ** -0.5
    _split_heads(group(0), gq_ref)
    _split_heads(group(1), gk_ref)
    _split_heads(group(2), gv_ref)
    _split_heads(group(3), gz_ref)
    _split_heads(group(4), fq_ref, qk_scale)
    _split_heads(group(5), fk_ref)
    _split_heads(group(6), fv_ref)
    glu_ref[0] = group(7) * _sigmoid(group(8))
    _split_heads(group(9), sq_ref, qk_scale)
    _split_heads(group(10), sk_ref)
    _split_heads(group(11), sv_ref)
    gate_ref[0] = group(12, GATE_LANES)


def _in_projection(x, w_cat):
    bsz, t, d = x.shape
    tm = min(TOKEN_TILE, t)
    row = lambda c: pl.BlockSpec((1, tm, c), lambda i, j: (i, j, 0))
    heads = pl.BlockSpec((1, GROUP_HEADS, tm, HEAD_DIM), lambda i, j: (i, 0, j, 0))
    hshape = lambda dt: jax.ShapeDtypeStruct((bsz, GROUP_HEADS, t, HEAD_DIM), dt)
    out_shape = ([hshape(F32)] * 4 + [hshape(BF16)] * 3 + [jax.ShapeDtypeStruct((bsz, t, GROUP_WIDTH), F32)]
                 + [hshape(BF16)] * 3 + [jax.ShapeDtypeStruct((bsz, t, GATE_LANES), F32)])
    out_specs = [heads] * 4 + [heads] * 3 + [row(GROUP_WIDTH)] + [heads] * 3 + [row(GATE_LANES)]
    return pl.pallas_call(
        _inproj_kernel,
        grid=(bsz, t // tm),
        in_specs=[row(d), _resident(w_cat.shape)],
        out_specs=out_specs,
        out_shape=out_shape,
        compiler_params=_params(("parallel", "parallel")),
        name="in_projection",
    )(x, w_cat)


def _fox_cum_kernel(gate_ref, bias_ref, o_ref, carry_ref):
    @pl.when(pl.program_id(1) == 0)
    def _():
        carry_ref[...] = jnp.zeros_like(carry_ref)

    tb = gate_ref.shape[1]
    logit = gate_ref[0] + bias_ref[...]
    log_f = -_softplus(-logit)
    rows = lax.broadcasted_iota(jnp.int32, (tb, tb), 0)
    cols = lax.broadcasted_iota(jnp.int32, (tb, tb), 1)
    tri = (cols <= rows).astype(F32)
    cum = _dot_f32(tri, log_f) + carry_ref[...]
    carry_ref[...] = cum[tb - 1:tb, :]
    o_ref[0, 0] = cum.T


def _fox_cumsum(gates, bias_row):
    bsz, t, _ = gates.shape
    tb = min(ATTN_BLOCK, t)
    return pl.pallas_call(
        _fox_cum_kernel,
        grid=(bsz, t // tb),
        in_specs=[pl.BlockSpec((1, tb, GATE_LANES), lambda i, j: (i, j, 0)), _resident((1, GATE_LANES))],
        out_specs=pl.BlockSpec((1, 1, GATE_LANES, tb), lambda i, j: (i, j, 0, 0)),
        out_shape=jax.ShapeDtypeStruct((bsz, t // tb, GATE_LANES, tb), F32),
        scratch_shapes=[pltpu.VMEM((1, GATE_LANES), F32)],
        compiler_params=_params(("parallel", "arbitrary")),
        name="fox_cumsum",
    )(gates, bias_row)


def _causal_masks(blk):
    rows = lax.broadcasted_iota(jnp.int32, (blk, blk), 0)
    cols = lax.broadcasted_iota(jnp.int32, (blk, blk), 1)
    return cols <= rows, cols < rows


def _fox_kernel(q_ref, k_ref, v_ref, ck_ref, o_ref):
    blk = q_ref.shape[2]
    head = pl.program_id(1)
    qi = pl.program_id(2)
    q = q_ref[0, 0]
    causal, _ = _causal_masks(blk)

    def step(j, carry, masked):
        m, l, acc = carry
        start = pl.multiple_of(j * blk, blk)
        kb = k_ref[0, 0, pl.ds(start, blk), :]
        vb = v_ref[0, 0, pl.ds(start, blk), :]
        s = _dot_nt(q, kb) - ck_ref[0, j, pl.ds(head, 1), :]
        if masked:
            s = jnp.where(causal, s, NEG_BIG)
        m_new = jnp.maximum(m, jnp.max(s, axis=-1, keepdims=True))
        p = jnp.exp(s - m_new)
        alpha = jnp.exp(m - m_new)
        l = alpha * l + jnp.sum(p, axis=-1, keepdims=True)
        acc = alpha * acc + _dot(p.astype(BF16), vb)
        return m_new, l, acc

    init = (jnp.full((blk, 1), NEG_BIG, F32), jnp.zeros((blk, 1), F32), jnp.zeros((blk, HEAD_DIM), F32))
    carry = lax.fori_loop(0, qi, lambda j, c: step(j, c, False), init)
    _, l, acc = step(qi, carry, True)
    o_ref[0, 0] = acc / l


def _fox_attention(q, k, v, cum_t):
    bsz, nh, t, hd = q.shape
    blk = min(ATTN_BLOCK, t)
    nk = t // blk
    whole = pl.BlockSpec((1, 1, t, hd), lambda b, h, i: (b, h, 0, 0))
    qblk = pl.BlockSpec((1, 1, blk, hd), lambda b, h, i: (b, h, i, 0))
    return pl.pallas_call(
        _fox_kernel,
        grid=(bsz, nh, nk),
        in_specs=[qblk, whole, whole,
                  pl.BlockSpec((1, nk, 8, blk), lambda b, h, i: (b, 0, GATE_F0 // 8, 0))],
        out_specs=qblk,
        out_shape=jax.ShapeDtypeStruct((bsz, nh, t, hd), F32),
        compiler_params=_params(("parallel", "parallel", "arbitrary")),
        name="fox_attention",
    )(q, k, v, cum_t)


def _sb_kernel(q_ref, k_ref, v_ref, o_ref):
    blk = q_ref.shape[2]
    qi = pl.program_id(2)
    q = q_ref[0, 0]
    causal, strict = _causal_masks(blk)
    suffix = causal.astype(BF16)

    def step(j, carry, masked):
        rest, acc = carry
        start = pl.multiple_of(j * blk, blk)
        kb = k_ref[0, 0, pl.ds(start, blk), :]
        vb = v_ref[0, 0, pl.ds(start, blk), :]
        z = _dot_nt(q, kb)
        log_keep = -_softplus(z)
        if masked:
            log_keep = jnp.where(strict, log_keep, 0.0)
        hi = log_keep.astype(BF16)
        lo = (log_keep - hi.astype(F32)).astype(BF16)
        tail = _dot(hi, suffix) + _dot(lo, suffix)
        w = jnp.exp(z + tail + rest)
        if masked:
            w = jnp.where(strict, w, 0.0)
        acc = acc + _dot(w.astype(BF16), vb)
        rest = rest + tail[:, 0:1]
        return rest, acc

    carry = step(qi, (jnp.zeros((blk, 1), F32), jnp.zeros((blk, HEAD_DIM), F32)), True)
    _, acc = lax.fori_loop(0, qi, lambda i, c: step(qi - 1 - i, c, False), carry)
    o_ref[0, 0] = acc


def _sb_attention(q, k, v):
    bsz, nh, t, hd = q.shape
    blk = min(ATTN_BLOCK, t)
    whole = pl.BlockSpec((1, 1, t, hd), lambda b, h, i: (b, h, 0, 0))
    qblk = pl.BlockSpec((1, 1, blk, hd), lambda b, h, i: (b, h, i, 0))
    return pl.pallas_call(
        _sb_kernel,
        grid=(bsz, nh, t // blk),
        in_specs=[qblk, whole, whole],
        out_specs=qblk,
        out_shape=jax.ShapeDtypeStruct((bsz, nh, t, hd), F32),
        compiler_params=_params(("parallel", "parallel", "arbitrary")),
        name="sb_attention",
    )(q, k, v)


CONV_HALO = 32


def _conv_kernel(prev_ref, cur_ref, w_ref, b_ref, ng_ref, nb_ref, o_ref, xs_ref):
    tb = cur_ref.shape[1]
    first = pl.program_id(1) == 0
    xs_ref[0:CONV_HALO, :] = jnp.where(first, 0.0, prev_ref[0])
    xs_ref[CONV_HALO:CONV_HALO + tb, :] = cur_ref[0]
    base = CONV_HALO - (CONF_KERNEL - 1)
    acc = jnp.zeros((tb, GROUP_WIDTH), F32) + b_ref[...]
    for tap in range(CONF_KERNEL):
        acc = acc + w_ref[tap:tap + 1, :] * xs_ref[base + tap:base + tap + tb, :]
    gsz = GROUP_WIDTH // CONF_GROUPS
    parts = []
    for g in range(CONF_GROUPS):
        cg = acc[:, g * gsz:(g + 1) * gsz]
        mu = jnp.mean(cg, axis=-1, keepdims=True)
        d = cg - mu
        var = jnp.mean(d * d, axis=-1, keepdims=True)
        parts.append(d * lax.rsqrt(var + LN_EPS))
    hn = jnp.concatenate(parts, axis=-1) * ng_ref[...] + nb_ref[...]
    o_ref[0] = _silu(hn)


def _conv_module(glu, w, b, ng, nb):
    bsz, t, c = glu.shape
    tb = min(CONV_BLOCK, t)
    per = tb // CONV_HALO
    vec = lambda a: a.reshape(1, c)
    return pl.pallas_call(
        _conv_kernel,
        grid=(bsz, t // tb),
        in_specs=[pl.BlockSpec((1, CONV_HALO, c), lambda i, j: (i, jnp.maximum(j * per - 1, 0), 0)),
                  pl.BlockSpec((1, tb, c), lambda i, j: (i, j, 0)),
                  _resident(w.shape), _resident((1, c)), _resident((1, c)), _resident((1, c))],
        out_specs=pl.BlockSpec((1, tb, c), lambda i, j: (i, j, 0)),
        out_shape=jax.ShapeDtypeStruct(glu.shape, F32),
        scratch_shapes=[pltpu.VMEM((CONV_HALO + tb, c), F32)],
        compiler_params=_params(("parallel", "parallel")),
        name="conv_module",
    )(glu, glu, w, vec(b), vec(ng), vec(nb))


GDN_HALO = 8


def _unit_lower_inverse(low):
    c = low.shape[0]
    rows = lax.broadcasted_iota(jnp.int32, (c, c), 0)
    cols = lax.broadcasted_iota(jnp.int32, (c, c), 1)
    inv = jnp.where(rows == cols, 1.0, 0.0).astype(F32)
    size = 1
    while size < c:
        rb = rows // size
        off = jnp.where((rb % 2 == 1) & (cols // size == rb - 1), low, 0.0)
        if size == 1:
            inv = inv - off
        else:
            inv = inv - _dot_f32(inv, _dot_f32(off, inv))
        size *= 2
    return inv


def _gdn_kernel(qp_ref, kp_ref, vp_ref, q_ref, k_ref, v_ref, z_ref, gate_ref, cw_ref, alog_ref, dtb_ref,
                ng_ref, o_ref, xs_ref, qn_ref, kn_ref, vn_ref, state_ref):
    tb = q_ref.shape[2]
    c = GDN_CHUNK
    first = pl.program_id(1) == 0

    @pl.when(first)
    def _():
        state_ref[...] = jnp.zeros_like(state_ref)

    base = GDN_HALO - (SHORT_CONV - 1)
    for idx, (p_ref, c_ref, dst_ref) in enumerate(((qp_ref, q_ref, qn_ref), (kp_ref, k_ref, kn_ref),
                                                   (vp_ref, v_ref, vn_ref))):
        for h in range(GROUP_HEADS):
            xs_ref[0:GDN_HALO, :] = jnp.where(first, 0.0, p_ref[0, h])
            xs_ref[GDN_HALO:GDN_HALO + tb, :] = c_ref[0, h]
            acc = jnp.zeros((tb, HEAD_DIM), F32)
            for tap in range(SHORT_CONV):
                acc = acc + cw_ref[idx, h, tap:tap + 1, :] * xs_ref[base + tap:base + tap + tb, :]
            y = _silu(acc)
            if idx < 2:
                y = y * lax.rsqrt(jnp.sum(y * y, axis=-1, keepdims=True) + L2_EPS)
            dst_ref[h] = y

    gates = gate_ref[0]
    log_decay = -jnp.exp(alog_ref[...]) * _softplus(gates + dtb_ref[...])
    beta_all = _sigmoid(gates)
    rows = lax.broadcasted_iota(jnp.int32, (tb, tb), 0)
    cols = lax.broadcasted_iota(jnp.int32, (tb, tb), 1)
    chunk_tri = ((cols <= rows) & (cols // c == rows // c)).astype(F32)
    gcum = _dot_f32(chunk_tri, log_decay)
    gcum_t = gcum.T

    crow = lax.broadcasted_iota(jnp.int32, (c, c), 0)
    ccol = lax.broadcasted_iota(jnp.int32, (c, c), 1)
    lower_incl = ccol <= crow
    strict_lower = ccol < crow

    for ci in range(tb // c):
        r0 = ci * c
        for h in range(GROUP_HEADS):
            qc = qn_ref[h, r0:r0 + c, :] * (HEAD_DIM ** -0.5)
            kc = kn_ref[h, r0:r0 + c, :]
            vc = vn_ref[h, r0:r0 + c, :]
            beta = beta_all[r0:r0 + c, GATE_B0 + h:GATE_B0 + h + 1]
            g_col = gcum[r0:r0 + c, GATE_A0 + h:GATE_A0 + h + 1]
            g_row = gcum_t[GATE_A0 + h:GATE_A0 + h + 1, r0:r0 + c]
            g_last = g_col[c - 1:c, :]
            decay = jnp.exp(jnp.where(lower_incl, g_col - g_row, -jnp.inf))
            k_beta = kc * beta
            v_beta = vc * beta
            kcb = kc.astype(BF16)
            lkk = jnp.where(strict_lower, _dot_nt(k_beta.astype(BF16), kcb) * decay, 0.0)
            t_inv = _unit_lower_inverse(lkk).astype(BF16)
            eg = jnp.exp(g_col)
            u = _dot(t_inv, v_beta.astype(BF16))
            w = _dot(t_inv, (k_beta * eg).astype(BF16))
            a_qk = _dot_nt(qc.astype(BF16), kcb) * decay
            q_dec = qc * eg
            k_dec = kc * jnp.exp(g_last - g_col)
            state = state_ref[h]
            sb = state.astype(BF16)
            v_new = u - _dot(w.astype(BF16), sb)
            vnb = v_new.astype(BF16)
            o = _dot(q_dec.astype(BF16), sb) + _dot(a_qk.astype(BF16), vnb)
            state_ref[h] = state * jnp.exp(g_last) + _dot_tn(k_dec.astype(BF16), vnb)
            o = o * lax.rsqrt(jnp.mean(o * o, axis=-1, keepdims=True) + RMS_EPS) * ng_ref[...]
            o_ref[0, h, r0:r0 + c, :] = o * _silu(z_ref[0, h, r0:r0 + c, :])


def _gdn_mixer(q, k, v, z, gates, conv_w, a_log, dt_bias, norm_g):
    bsz, nh, t, hd = q.shape
    tb = min(GDN_BLOCK, t)
    per = tb // GDN_HALO
    cur = pl.BlockSpec((1, nh, tb, hd), lambda i, j: (i, 0, j, 0))
    prev = pl.BlockSpec((1, nh, GDN_HALO, hd), lambda i, j: (i, 0, jnp.maximum(j * per - 1, 0), 0))
    cw = conv_w.reshape(SHORT_CONV, 3, nh, hd).transpose(1, 2, 0, 3)
    lane_row = lambda vals, off: jnp.zeros((1, GATE_LANES), F32).at[0, off:off + nh].set(vals)
    return pl.pallas_call(
        _gdn_kernel,
        grid=(bsz, t // tb),
        in_specs=[prev, prev, prev, cur, cur, cur, cur,
                  pl.BlockSpec((1, tb, GATE_LANES), lambda i, j: (i, j, 0)),
                  _resident(cw.shape), _resident((1, GATE_LANES)), _resident((1, GATE_LANES)),
                  _resident((1, hd))],
        out_specs=cur,
        out_shape=jax.ShapeDtypeStruct(q.shape, F32),
        scratch_shapes=[pltpu.VMEM((GDN_HALO + tb, hd), F32),
                        pltpu.VMEM((nh, tb, hd), F32), pltpu.VMEM((nh, tb, hd), F32),
                        pltpu.VMEM((nh, tb, hd), F32), pltpu.VMEM((nh, hd, hd), F32)],
        compiler_params=_params(("parallel", "arbitrary")),
        name="gdn_mixer",
    )(q, k, v, q, k, v, z, gates, cw, lane_row(a_log, GATE_A0), lane_row(dt_bias, GATE_A0),
      norm_g.reshape(1, hd))


def _outproj_kernel(x_ref, ya_ref, yb_ref, yc_ref, yd_ref, w_ref, g_ref, b_ref, o_ref):
    gw = GROUP_WIDTH

    def heads(y_ref, group):
        tot = None
        for h in range(GROUP_HEADS):
            r0 = group * gw + h * HEAD_DIM
            part = _dot(y_ref[0, h].astype(BF16), w_ref[r0:r0 + HEAD_DIM, :])
            tot = part if tot is None else tot + part
        return tot

    mix = heads(ya_ref, 0) + heads(yb_ref, 1) + _dot(yc_ref[0].astype(BF16), w_ref[2 * gw:3 * gw, :])
    mix = mix + heads(yd_ref, 3)
    o_ref[0] = _layer_norm(DN_ALPHA * x_ref[0] + mix, g_ref[...], b_ref[...])


def _out_projection_ln(x, ya, yb, yc, yd, w_out, g, b):
    bsz, t, d = x.shape
    tm = min(TOKEN_TILE, t)
    row = lambda c: pl.BlockSpec((1, tm, c), lambda i, j: (i, j, 0))
    heads = pl.BlockSpec((1, GROUP_HEADS, tm, HEAD_DIM), lambda i, j: (i, 0, j, 0))
    return pl.pallas_call(
        _outproj_kernel,
        grid=(bsz, t // tm),
        in_specs=[row(d), heads, heads, row(GROUP_WIDTH), heads, _resident(w_out.shape),
                  _resident((1, d)), _resident((1, d))],
        out_specs=row(d),
        out_shape=jax.ShapeDtypeStruct(x.shape, F32),
        compiler_params=_params(("parallel", "parallel")),
        name="out_projection_ln",
    )(x, ya, yb, yc, yd, w_out, g.reshape(1, d), b.reshape(1, d))


def _memkv_kernel(m_ref, w_ref, o_ref):
    o_ref[0] = _dot(m_ref[0].astype(BF16), w_ref[...]).astype(o_ref.dtype)


def _memory_kv(mem, w_kv):
    bsz, m, d = mem.shape
    return pl.pallas_call(
        _memkv_kernel,
        grid=(bsz,),
        in_specs=[pl.BlockSpec((1, m, d), lambda i: (i, 0, 0)), _resident(w_kv.shape)],
        out_specs=pl.BlockSpec((1, m, 2 * d), lambda i: (i, 0, 0)),
        out_shape=jax.ShapeDtypeStruct((bsz, m, 2 * d), BF16),
        compiler_params=_params(("parallel",)),
        name="memory_kv",
    )(mem, w_kv)


def _memattn_kernel(x_ref, kv_ref, wq_ref, wo_ref, g_ref, b_ref, o_ref):
    x = x_ref[0]
    q = _dot(x.astype(BF16), wq_ref[...])
    hd = MEM_HEAD_DIM
    outs = []
    for h in range(MEM_HEADS):
        qh = (q[:, h * hd:(h + 1) * hd] * (hd ** -0.5)).astype(BF16)
        kh = kv_ref[0, :, h * hd:(h + 1) * hd]
        vh = kv_ref[0, :, D_MODEL + h * hd:D_MODEL + (h + 1) * hd]
        s = _dot_nt(qh, kh)
        p = jnp.exp(s - jnp.max(s, axis=-1, keepdims=True))
        p = p / jnp.sum(p, axis=-1, keepdims=True)
        outs.append(_dot(p.astype(BF16), vh).astype(BF16))
    y = _dot(jnp.concatenate(outs, axis=-1), wo_ref[...])
    o_ref[0] = _layer_norm(DN_ALPHA * x + y, g_ref[...], b_ref[...])


def _memory_attention_ln(x, kv, wq, wo, g, b):
    bsz, t, d = x.shape
    tm = min(TOKEN_TILE, t)
    row = pl.BlockSpec((1, tm, d), lambda i, j: (i, j, 0))
    return pl.pallas_call(
        _memattn_kernel,
        grid=(bsz, t // tm),
        in_specs=[row, pl.BlockSpec((1, kv.shape[1], 2 * d), lambda i, j: (i, 0, 0)),
                  _resident(wq.shape), _resident(wo.shape), _resident((1, d)), _resident((1, d))],
        out_specs=row,
        out_shape=jax.ShapeDtypeStruct(x.shape, F32),
        compiler_params=_params(("parallel", "parallel")),
        name="memory_attention_ln",
    )(x, kv, wq, wo, g.reshape(1, d), b.reshape(1, d))


def _combined_in_weight(w_in):
    gw, nh = GROUP_WIDTH, GROUP_HEADS
    o = 0
    gdn_qkv = w_in[:, o:o + 3 * gw]; o += 3 * gw
    gdn_z = w_in[:, o:o + gw]; o += gw
    gdn_a = w_in[:, o:o + nh]; o += nh
    gdn_b = w_in[:, o:o + nh]; o += nh
    fox_qkv = w_in[:, o:o + 3 * gw]; o += 3 * gw
    fox_f = w_in[:, o:o + nh]; o += nh
    conf = w_in[:, o:o + 2 * gw]; o += 2 * gw
    sb_qkv = w_in[:, o:o + 3 * gw]
    pad = jnp.zeros((w_in.shape[0], GATE_LANES - 3 * nh), w_in.dtype)
    return jnp.concatenate([gdn_qkv, gdn_z, fox_qkv, conf, sb_qkv, gdn_a, gdn_b, fox_f, pad], axis=1).astype(BF16)


def kernel(x, mem, ffn1_w_gate, ffn1_w_up, ffn1_w_down, ln_ffn1_g, ln_ffn1_b, w_in, gdn_conv_w, gdn_a_log, gdn_dt_bias, gdn_norm_g, fox_b_f, conf_dw_w, conf_dw_b, conf_norm_g, conf_norm_b, w_out, ln_mix_g, ln_mix_b, mem_w_q, mem_w_kv, mem_w_o, ln_mem_g, ln_mem_b, ffn2_w_gate, ffn2_w_up, ffn2_w_down, ln_ffn2_g, ln_ffn2_b):
    bf = lambda a: a.astype(BF16)
    for i in range(DEPTH):
        x = _ffn_ln(x, bf(ffn1_w_gate[i]), bf(ffn1_w_up[i]), bf(ffn1_w_down[i]), ln_ffn1_g[i], ln_ffn1_b[i])

        (gq, gk, gv, gz, fq, fk, fv, glu, sq, sk, sv, gates) = _in_projection(x, _combined_in_weight(w_in[i]))
        y_a = _gdn_mixer(gq, gk, gv, gz, gates, gdn_conv_w[i], gdn_a_log[i], gdn_dt_bias[i], gdn_norm_g[i])
        f_bias = jnp.zeros((1, GATE_LANES), F32).at[0, GATE_F0:GATE_F0 + GROUP_HEADS].set(fox_b_f[i])
        y_b = _fox_attention(fq, fk, fv, _fox_cumsum(gates, f_bias))
        y_c = _conv_module(glu, conf_dw_w[i], conf_dw_b[i], conf_norm_g[i], conf_norm_b[i])
        y_d = _sb_attention(sq, sk, sv)
        x = _out_projection_ln(x, y_a, y_b, y_c, y_d, bf(w_out[i]), ln_mix_g[i], ln_mix_b[i])

        kv = _memory_kv(mem, bf(mem_w_kv[i]))
        x = _memory_attention_ln(x, kv, bf(mem_w_q[i]), bf(mem_w_o[i]), ln_mem_g[i], ln_mem_b[i])

        x = _ffn_ln(x, bf(ffn2_w_gate[i]), bf(ffn2_w_up[i]), bf(ffn2_w_down[i]), ln_ffn2_g[i], ln_ffn2_b[i])
    return x
```

```python
import functools
import math

import jax
import jax.numpy as jnp
from jax import lax
from jax.experimental import pallas as pl
from jax.experimental.pallas import tpu as pltpu

F32 = jnp.float32
BF16 = jnp.bfloat16

D_MODEL = 1024
DEPTH = 2
GROUP_WIDTH = 256
HEAD_DIM = 64
GROUP_HEADS = 4
D_FF = 2816
SHORT_CONV = 4
CONF_KERNEL = 31
CONF_GROUPS = 4
GDN_CHUNK = 64
N_MEM = 256
MEM_HEADS = 4
MEM_HEAD_DIM = D_MODEL // MEM_HEADS
DN_ALPHA = float((2 * DEPTH) ** 0.25)
LN_EPS = 1e-5
RMS_EPS = 1e-6
L2_EPS = 1e-6
NEG_BIG = -1e30

GATE_LANES = 128
GATE_A0, GATE_B0, GATE_F0 = 0, 4, 8

TOKEN_TILE = 512
PAIR_WIDTH = 2 * HEAD_DIM
FOX_BLOCK = 512
SB_Q_BLOCK = 512
SB_K_BLOCK = 256
GDN_BLOCK = 256
CONV_BLOCK = 512
VMEM_LIMIT = 56 * 1024 * 1024


def _params(sem, vmem=VMEM_LIMIT):
    return pltpu.CompilerParams(dimension_semantics=sem, vmem_limit_bytes=vmem)


def _resident(shape):
    nd = len(shape)
    return pl.BlockSpec(shape, lambda *_: (0,) * nd, pipeline_mode=pl.Buffered(1))


def _layer_norm(y, g, b):
    mu = jnp.mean(y, axis=-1, keepdims=True)
    d = y - mu
    var = jnp.mean(d * d, axis=-1, keepdims=True)
    return d * lax.rsqrt(var + LN_EPS) * g + b


def _sigmoid(x):
    return 1.0 / (1.0 + jnp.exp(-x))


def _silu(x):
    return x * _sigmoid(x)


def _softplus(x):
    return jnp.maximum(x, 0.0) + jnp.log1p(jnp.exp(-jnp.abs(x)))


def _dot(a, b):
    return jnp.dot(a, b, preferred_element_type=F32)


def _dot_nt(a, b):
    return lax.dot_general(a, b, (((1,), (1,)), ((), ())), preferred_element_type=F32)


def _dot_tn(a, b):
    return lax.dot_general(a, b, (((0,), (0,)), ((), ())), preferred_element_type=F32)


def _dot_f32(a, b):
    return jnp.dot(a, b, preferred_element_type=F32, precision=lax.Precision.HIGHEST)


def _ffn_kernel(x_ref, wg_ref, wu_ref, wd_ref, g_ref, b_ref, o_ref, *, n_split):
    x = x_ref[0]
    xb = x.astype(BF16)
    fc = D_FF // n_split
    acc = None
    for c in range(n_split):
        h = _dot(xb, wg_ref[:, c * fc:(c + 1) * fc])
        u = _dot(xb, wu_ref[:, c * fc:(c + 1) * fc])
        a = (_silu(h) * u).astype(BF16)
        part = _dot(a, wd_ref[c * fc:(c + 1) * fc, :])
        acc = part if acc is None else acc + part
    o_ref[0] = _layer_norm(DN_ALPHA * x + 0.5 * acc, g_ref[...], b_ref[...])


def _ffn_ln(x, wg, wu, wd, g, b):
    bsz, t, d = x.shape
    tm = min(TOKEN_TILE, t)
    row = pl.BlockSpec((1, tm, d), lambda i, j: (i, j, 0))
    return pl.pallas_call(
        functools.partial(_ffn_kernel, n_split=2),
        grid=(bsz, t // tm),
        in_specs=[row, _resident(wg.shape), _resident(wu.shape), _resident(wd.shape),
                  _resident((1, d)), _resident((1, d))],
        out_specs=row,
        out_shape=jax.ShapeDtypeStruct(x.shape, F32),
        compiler_params=_params(("parallel", "parallel")),
        name="ffn_ln",
    )(x, wg, wu, wd, g.reshape(1, d), b.reshape(1, d))


def _split_heads(r, o_ref, scale=None):
    for h in range(GROUP_HEADS):
        piece = r[:, h * HEAD_DIM:(h + 1) * HEAD_DIM]
        if scale is not None:
            piece = piece * scale
        o_ref[0, h] = piece.astype(o_ref.dtype)


def _inproj_kernel(x_ref, w_ref, gq_ref, gk_ref, gv_ref, gz_ref, fq_ref, fk_ref, fv_ref,
                   glu_ref, sq_ref, sk_ref, sv_ref, gate_ref):
    xb = x_ref[0].astype(BF16)
    gw = GROUP_WIDTH

    def group(i, width=gw):
        return _dot(xb, w_ref[:, i * gw:i * gw + width])

    qk_scale = HEAD_DIM ** -0.5
    _split_heads(group(0), gq_ref)
    _split_heads(group(1), gk_ref)
    _split_heads(group(2), gv_ref)
    _split_heads(group(3), gz_ref)
    fq_ref[0] = (group(4) * qk_scale).astype(BF16)
    fk_ref[0] = group(5).astype(BF16)
    fv_ref[0] = group(6).astype(BF16)
    glu_ref[0] = group(7) * _sigmoid(group(8))
    sq_ref[0] = (group(9) * qk_scale).astype(BF16)
    sk_ref[0] = group(10).astype(BF16)
    sv_ref[0] = group(11).astype(BF16)
    gate_ref[0] = group(12, GATE_LANES)


def _in_projection(x, w_cat):
    bsz, t, d = x.shape
    tm = min(TOKEN_TILE, t)
    row = lambda c: pl.BlockSpec((1, tm, c), lambda i, j: (i, j, 0))
    heads = pl.BlockSpec((1, GROUP_HEADS, tm, HEAD_DIM), lambda i, j: (i, 0, j, 0))
    hshape = lambda dt: jax.ShapeDtypeStruct((bsz, GROUP_HEADS, t, HEAD_DIM), dt)
    wide = lambda dt: jax.ShapeDtypeStruct((bsz, t, GROUP_WIDTH), dt)
    out_shape = ([hshape(F32)] * 4 + [wide(BF16)] * 3 + [wide(F32)] + [wide(BF16)] * 3
                 + [jax.ShapeDtypeStruct((bsz, t, GATE_LANES), F32)])
    out_specs = [heads] * 4 + [row(GROUP_WIDTH)] * 7 + [row(GATE_LANES)]
    return pl.pallas_call(
        _inproj_kernel,
        grid=(bsz, t // tm),
        in_specs=[row(d), _resident(w_cat.shape)],
        out_specs=out_specs,
        out_shape=out_shape,
        compiler_params=_params(("parallel", "parallel")),
        name="in_projection",
    )(x, w_cat)


def _fox_cum_kernel(gate_ref, bias_ref, o_ref, carry_ref):
    @pl.when(pl.program_id(1) == 0)
    def _():
        carry_ref[...] = jnp.zeros_like(carry_ref)

    tb = gate_ref.shape[1]
    logit = gate_ref[0] + bias_ref[...]
    log_f = -_softplus(-logit)
    rows = lax.broadcasted_iota(jnp.int32, (tb, tb), 0)
    cols = lax.broadcasted_iota(jnp.int32, (tb, tb), 1)
    tri = (cols <= rows).astype(F32)
    cum = _dot_f32(tri, log_f) + carry_ref[...]
    carry_ref[...] = cum[tb - 1:tb, :]
    o_ref[0, 0] = cum.T


def _fox_cumsum(gates, bias_row):
    bsz, t, _ = gates.shape
    tb = min(FOX_BLOCK, t)
    return pl.pallas_call(
        _fox_cum_kernel,
        grid=(bsz, t // tb),
        in_specs=[pl.BlockSpec((1, tb, GATE_LANES), lambda i, j: (i, j, 0)), _resident((1, GATE_LANES))],
        out_specs=pl.BlockSpec((1, 1, GATE_LANES, tb), lambda i, j: (i, j, 0, 0)),
        out_shape=jax.ShapeDtypeStruct((bsz, t // tb, GATE_LANES, tb), F32),
        scratch_shapes=[pltpu.VMEM((1, GATE_LANES), F32)],
        compiler_params=_params(("parallel", "arbitrary")),
        name="fox_cumsum",
    )(gates, bias_row)


def _causal_masks(blk):
    rows = lax.broadcasted_iota(jnp.int32, (blk, blk), 0)
    cols = lax.broadcasted_iota(jnp.int32, (blk, blk), 1)
    return cols <= rows, cols < rows


def _stack_heads(x2):
    lane = lax.broadcasted_iota(jnp.int32, x2.shape, 1)
    zero = jnp.zeros_like(x2)
    return jnp.concatenate([jnp.where(lane < HEAD_DIM, x2, zero), jnp.where(lane >= HEAD_DIM, x2, zero)], axis=0)


def _unstack_heads(y, rows):
    lane = lax.broadcasted_iota(jnp.int32, (rows, PAIR_WIDTH), 1)
    return jnp.where(lane < HEAD_DIM, y[:rows], y[rows:])


def _fox_kernel(q_ref, k_ref, v_ref, ck_ref, o_ref):
    blk = q_ref.shape[1]
    pair = pl.program_id(1)
    qi = pl.program_id(2)
    qs = _stack_heads(q_ref[0])
    causal, _ = _causal_masks(blk)
    ones = jnp.ones((blk, PAIR_WIDTH), BF16)

    def step(j, carry, masked):
        m, acc = carry
        start = pl.multiple_of(j * blk, blk)
        kb = k_ref[0, pl.ds(start, blk), :]
        vb = jnp.concatenate([v_ref[0, pl.ds(start, blk), :], ones], axis=1)
        s = _dot_nt(qs, kb)
        halves = []
        for h in range(2):
            sh = s[h * blk:(h + 1) * blk] - ck_ref[0, j, pl.ds(2 * pair + h, 1), :]
            if masked:
                sh = jnp.where(causal, sh, NEG_BIG)
            halves.append(sh)
        s = jnp.concatenate(halves, axis=0)
        m_new = jnp.maximum(m, jnp.max(s, axis=-1, keepdims=True))
        p = jnp.exp(s - m_new)
        acc = jnp.exp(m - m_new) * acc + _dot(p.astype(BF16), vb)
        return m_new, acc

    init = (jnp.full((2 * blk, 1), NEG_BIG, F32), jnp.zeros((2 * blk, 2 * PAIR_WIDTH), F32))
    carry = lax.fori_loop(0, qi, lambda j, c: step(j, c, False), init)
    _, acc = step(qi, carry, True)
    o_ref[0] = _unstack_heads(acc[:, :PAIR_WIDTH] / acc[:, PAIR_WIDTH:PAIR_WIDTH + 1], blk)


def _fox_attention(q, k, v, cum_t):
    bsz, t, width = q.shape
    blk = min(FOX_BLOCK, t)
    nk = t // blk
    whole = pl.BlockSpec((1, t, PAIR_WIDTH), lambda b, p, i: (b, 0, p))
    qblk = pl.BlockSpec((1, blk, PAIR_WIDTH), lambda b, p, i: (b, i, p))
    return pl.pallas_call(
        _fox_kernel,
        grid=(bsz, width // PAIR_WIDTH, nk),
        in_specs=[qblk, whole, whole,
                  pl.BlockSpec((1, nk, 8, blk), lambda b, p, i: (b, 0, GATE_F0 // 8, 0))],
        out_specs=qblk,
        out_shape=jax.ShapeDtypeStruct((bsz, t, width), F32),
        compiler_params=_params(("parallel", "parallel", "arbitrary")),
        name="fox_attention",
    )(q, k, v, cum_t)


def _sb_kernel(q_ref, k_ref, v_ref, o_ref, *, tk):
    tq = q_ref.shape[1]
    per = tq // tk
    qi = pl.program_id(2)
    qs = _stack_heads(q_ref[0])
    rows = lax.broadcasted_iota(jnp.int32, (2 * tq, tk), 0) % tq
    cols = lax.broadcasted_iota(jnp.int32, (2 * tq, tk), 1)
    suffix = _causal_masks(tk)[0].astype(BF16)

    def step(j, carry, strict):
        rest, acc = carry
        start = pl.multiple_of(j * tk, tk)
        kb = k_ref[0, pl.ds(start, tk), :]
        vb = v_ref[0, pl.ds(start, tk), :]
        z = _dot_nt(qs, kb)
        log_keep = -_softplus(z)
        if strict is not None:
            log_keep = jnp.where(strict, log_keep, 0.0)
        hi = log_keep.astype(BF16)
        lo = (log_keep - hi.astype(F32)).astype(BF16)
        tails = _dot(jnp.concatenate([hi, lo], axis=0), suffix)
        tail = tails[:2 * tq] + tails[2 * tq:]
        w = jnp.exp(z + tail + rest)
        if strict is not None:
            w = jnp.where(strict, w, 0.0)
        acc = acc + _dot(w.astype(BF16), vb)
        rest = rest + tail[:, 0:1]
        return rest, acc

    carry = (jnp.zeros((2 * tq, 1), F32), jnp.zeros((2 * tq, PAIR_WIDTH), F32))
    for d in reversed(range(per)):
        carry = step(qi * per + d, carry, cols + d * tk < rows)
    _, acc = lax.fori_loop(0, qi * per, lambda i, c: step(qi * per - 1 - i, c, None), carry)
    o_ref[0] = _unstack_heads(acc, tq)


def _sb_attention(q, k, v):
    bsz, t, width = q.shape
    tq = min(SB_Q_BLOCK, t)
    tk = min(SB_K_BLOCK, tq)
    whole = pl.BlockSpec((1, t, PAIR_WIDTH), lambda b, p, i: (b, 0, p))
    qblk = pl.BlockSpec((1, tq, PAIR_WIDTH), lambda b, p, i: (b, i, p))
    return pl.pallas_call(
        functools.partial(_sb_kernel, tk=tk),
        grid=(bsz, width // PAIR_WIDTH, t // tq),
        in_specs=[qblk, whole, whole],
        out_specs=qblk,
        out_shape=jax.ShapeDtypeStruct((bsz, t, width), F32),
        compiler_params=_params(("parallel", "parallel", "arbitrary")),
        name="sb_attention",
    )(q, k, v)


CONV_HALO = 32


def _conv_kernel(prev_ref, cur_ref, w_ref, b_ref, ng_ref, nb_ref, o_ref, xs_ref):
    tb = cur_ref.shape[1]
    first = pl.program_id(1) == 0
    xs_ref[0:CONV_HALO, :] = jnp.where(first, 0.0, prev_ref[0])
    xs_ref[CONV_HALO:CONV_HALO + tb, :] = cur_ref[0]
    base = CONV_HALO - (CONF_KERNEL - 1)
    acc = jnp.zeros((tb, GROUP_WIDTH), F32) + b_ref[...]
    for tap in range(CONF_KERNEL):
        acc = acc + w_ref[tap:tap + 1, :] * xs_ref[base + tap:base + tap + tb, :]
    gsz = GROUP_WIDTH // CONF_GROUPS
    parts = []
    for g in range(CONF_GROUPS):
        cg = acc[:, g * gsz:(g + 1) * gsz]
        mu = jnp.mean(cg, axis=-1, keepdims=True)
        d = cg - mu
        var = jnp.mean(d * d, axis=-1, keepdims=True)
        parts.append(d * lax.rsqrt(var + LN_EPS))
    hn = jnp.concatenate(parts, axis=-1) * ng_ref[...] + nb_ref[...]
    o_ref[0] = _silu(hn)


def _conv_module(glu, w, b, ng, nb):
    bsz, t, c = glu.shape
    tb = min(CONV_BLOCK, t)
    per = tb // CONV_HALO
    vec = lambda a: a.reshape(1, c)
    return pl.pallas_call(
        _conv_kernel,
        grid=(bsz, t // tb),
        in_specs=[pl.BlockSpec((1, CONV_HALO, c), lambda i, j: (i, jnp.maximum(j * per - 1, 0), 0)),
                  pl.BlockSpec((1, tb, c), lambda i, j: (i, j, 0)),
                  _resident(w.shape), _resident((1, c)), _resident((1, c)), _resident((1, c))],
        out_specs=pl.BlockSpec((1, tb, c), lambda i, j: (i, j, 0)),
        out_shape=jax.ShapeDtypeStruct(glu.shape, F32),
        scratch_shapes=[pltpu.VMEM((CONV_HALO + tb, c), F32)],
        compiler_params=_params(("parallel", "parallel")),
        name="conv_module",
    )(glu, glu, w, vec(b), vec(ng), vec(nb))


GDN_HALO = 8


def _unit_lower_inverse(low):
    c = low.shape[0]
    rows = lax.broadcasted_iota(jnp.int32, (c, c), 0)
    cols = lax.broadcasted_iota(jnp.int32, (c, c), 1)
    inv = jnp.where(rows == cols, 1.0, 0.0).astype(F32)
    size = 1
    while size < c:
        rb = rows // size
        off = jnp.where((rb % 2 == 1) & (cols // size == rb - 1), low, 0.0)
        if size == 1:
            inv = inv - off
        else:
            inv = inv - _dot_f32(inv, _dot_f32(off, inv))
        size *= 2
    return inv


def _gdn_kernel(qp_ref, kp_ref, vp_ref, q_ref, k_ref, v_ref, z_ref, gate_ref, cw_ref, alog_ref, dtb_ref,
                ng_ref, o_ref, xs_ref, qn_ref, kn_ref, vn_ref, state_ref):
    tb = q_ref.shape[2]
    c = GDN_CHUNK
    first = pl.program_id(1) == 0

    @pl.when(first)
    def _():
        state_ref[...] = jnp.zeros_like(state_ref)

    base = GDN_HALO - (SHORT_CONV - 1)
    for idx, (p_ref, c_ref, dst_ref) in enumerate(((qp_ref, q_ref, qn_ref), (kp_ref, k_ref, kn_ref),
                                                   (vp_ref, v_ref, vn_ref))):
        for h in range(GROUP_HEADS):
            xs_ref[0:GDN_HALO, :] = jnp.where(first, 0.0, p_ref[0, h])
            xs_ref[GDN_HALO:GDN_HALO + tb, :] = c_ref[0, h]
            acc = jnp.zeros((tb, HEAD_DIM), F32)
            for tap in range(SHORT_CONV):
                acc = acc + cw_ref[idx, h, tap:tap + 1, :] * xs_ref[base + tap:base + tap + tb, :]
            y = _silu(acc)
            if idx < 2:
                y = y * lax.rsqrt(jnp.sum(y * y, axis=-1, keepdims=True) + L2_EPS)
            dst_ref[h] = y

    gates = gate_ref[0]
    log_decay = -jnp.exp(alog_ref[...]) * _softplus(gates + dtb_ref[...])
    beta_all = _sigmoid(gates)
    rows = lax.broadcasted_iota(jnp.int32, (tb, tb), 0)
    cols = lax.broadcasted_iota(jnp.int32, (tb, tb), 1)
    chunk_tri = ((cols <= rows) & (cols // c == rows // c)).astype(F32)
    gcum = _dot_f32(chunk_tri, log_decay)
    gcum_t = gcum.T

    crow = lax.broadcasted_iota(jnp.int32, (c, c), 0)
    ccol = lax.broadcasted_iota(jnp.int32, (c, c), 1)
    lower_incl = ccol <= crow
    strict_lower = ccol < crow

    for ci in range(tb // c):
        r0 = ci * c
        for h in range(GROUP_HEADS):
            qc = qn_ref[h, r0:r0 + c, :] * (HEAD_DIM ** -0.5)
            kc = kn_ref[h, r0:r0 + c, :]
            vc = vn_ref[h, r0:r0 + c, :]
            beta = beta_all[r0:r0 + c, GATE_B0 + h:GATE_B0 + h + 1]
            g_col = gcum[r0:r0 + c, GATE_A0 + h:GATE_A0 + h + 1]
            g_row = gcum_t[GATE_A0 + h:GATE_A0 + h + 1, r0:r0 + c]
            g_last = g_col[c - 1:c, :]
            decay = jnp.exp(jnp.where(lower_incl, g_col - g_row, -jnp.inf))
            k_beta = kc * beta
            v_beta = vc * beta
            kcb = kc.astype(BF16)
            lkk = jnp.where(strict_lower, _dot_nt(k_beta.astype(BF16), kcb) * decay, 0.0)
            t_inv = _unit_lower_inverse(lkk).astype(BF16)
            eg = jnp.exp(g_col)
            u = _dot(t_inv, v_beta.astype(BF16))
            w = _dot(t_inv, (k_beta * eg).astype(BF16))
            a_qk = _dot_nt(qc.astype(BF16), kcb) * decay
            q_dec = qc * eg
            k_dec = kc * jnp.exp(g_last - g_col)
            state = state_ref[h]
            sb = state.astype(BF16)
            v_new = u - _dot(w.astype(BF16), sb)
            vnb = v_new.astype(BF16)
            o = _dot(q_dec.astype(BF16), sb) + _dot(a_qk.astype(BF16), vnb)
            state_ref[h] = state * jnp.exp(g_last) + _dot_tn(k_dec.astype(BF16), vnb)
            o = o * lax.rsqrt(jnp.mean(o * o, axis=-1, keepdims=True) + RMS_EPS) * ng_ref[...]
            o_ref[0, h, r0:r0 + c, :] = o * _silu(z_ref[0, h, r0:r0 + c, :])


def _gdn_mixer(q, k, v, z, gates, conv_w, a_log, dt_bias, norm_g):
    bsz, nh, t, hd = q.shape
    tb = min(GDN_BLOCK, t)
    per = tb // GDN_HALO
    cur = pl.BlockSpec((1, nh, tb, hd), lambda i, j: (i, 0, j, 0))
    prev = pl.BlockSpec((1, nh, GDN_HALO, hd), lambda i, j: (i, 0, jnp.maximum(j * per - 1, 0), 0))
    cw = conv_w.reshape(SHORT_CONV, 3, nh, hd).transpose(1, 2, 0, 3)
    lane_row = lambda vals, off: jnp.zeros((1, GATE_LANES), F32).at[0, off:off + nh].set(vals)
    return pl.pallas_call(
        _gdn_kernel,
        grid=(bsz, t // tb),
        in_specs=[prev, prev, prev, cur, cur, cur, cur,
                  pl.BlockSpec((1, tb, GATE_LANES), lambda i, j: (i, j, 0)),
                  _resident(cw.shape), _resident((1, GATE_LANES)), _resident((1, GATE_LANES)),
                  _resident((1, hd))],
        out_specs=cur,
        out_shape=jax.ShapeDtypeStruct(q.shape, F32),
        scratch_shapes=[pltpu.VMEM((GDN_HALO + tb, hd), F32),
                        pltpu.VMEM((nh, tb, hd), F32), pltpu.VMEM((nh, tb, hd), F32),
                        pltpu.VMEM((nh, tb, hd), F32), pltpu.VMEM((nh, hd, hd), F32)],
        compiler_params=_params(("parallel", "arbitrary")),
        name="gdn_mixer",
    )(q, k, v, q, k, v, z, gates, cw, lane_row(a_log, GATE_A0), lane_row(dt_bias, GATE_A0),
      norm_g.reshape(1, hd))


def _outproj_kernel(x_ref, ya_ref, yb_ref, yc_ref, yd_ref, w_ref, g_ref, b_ref, o_ref):
    gw = GROUP_WIDTH

    def heads(y_ref, group):
        tot = None
        for h in range(GROUP_HEADS):
            r0 = group * gw + h * HEAD_DIM
            part = _dot(y_ref[0, h].astype(BF16), w_ref[r0:r0 + HEAD_DIM, :])
            tot = part if tot is None else tot + part
        return tot

    wide = lambda y_ref, group: _dot(y_ref[0].astype(BF16), w_ref[group * gw:(group + 1) * gw, :])
    mix = heads(ya_ref, 0) + wide(yb_ref, 1) + wide(yc_ref, 2) + wide(yd_ref, 3)
    o_ref[0] = _layer_norm(DN_ALPHA * x_ref[0] + mix, g_ref[...], b_ref[...])


def _out_projection_ln(x, ya, yb, yc, yd, w_out, g, b):
    bsz, t, d = x.shape
    tm = min(TOKEN_TILE, t)
    row = lambda c: pl.BlockSpec((1, tm, c), lambda i, j: (i, j, 0))
    heads = pl.BlockSpec((1, GROUP_HEADS, tm, HEAD_DIM), lambda i, j: (i, 0, j, 0))
    return pl.pallas_call(
        _outproj_kernel,
        grid=(bsz, t // tm),
        in_specs=[row(d), heads, row(GROUP_WIDTH), row(GROUP_WIDTH), row(GROUP_WIDTH), _resident(w_out.shape),
                  _resident((1, d)), _resident((1, d))],
        out_specs=row(d),
        out_shape=jax.ShapeDtypeStruct(x.shape, F32),
        compiler_params=_params(("parallel", "parallel")),
        name="out_projection_ln",
    )(x, ya, yb, yc, yd, w_out, g.reshape(1, d), b.reshape(1, d))


def _memkv_kernel(m_ref, w_ref, o_ref):
    o_ref[0] = _dot(m_ref[0].astype(BF16), w_ref[...]).astype(o_ref.dtype)


def _memory_kv(mem, w_kv):
    bsz, m, d = mem.shape
    return pl.pallas_call(
        _memkv_kernel,
        grid=(bsz,),
        in_specs=[pl.BlockSpec((1, m, d), lambda i: (i, 0, 0)), _resident(w_kv.shape)],
        out_specs=pl.BlockSpec((1, m, 2 * d), lambda i: (i, 0, 0)),
        out_shape=jax.ShapeDtypeStruct((bsz, m, 2 * d), BF16),
        compiler_params=_params(("parallel",)),
        name="memory_kv",
    )(mem, w_kv)


def _memattn_kernel(x_ref, kv_ref, wq_ref, wo_ref, g_ref, b_ref, o_ref):
    x = x_ref[0]
    q = _dot(x.astype(BF16), wq_ref[...])
    hd = MEM_HEAD_DIM
    outs = []
    for h in range(MEM_HEADS):
        qh = (q[:, h * hd:(h + 1) * hd] * (hd ** -0.5)).astype(BF16)
        kh = kv_ref[0, :, h * hd:(h + 1) * hd]
        vh = kv_ref[0, :, D_MODEL + h * hd:D_MODEL + (h + 1) * hd]
        s = _dot_nt(qh, kh)
        p = jnp.exp(s - jnp.max(s, axis=-1, keepdims=True))
        p = p / jnp.sum(p, axis=-1, keepdims=True)
        outs.append(_dot(p.astype(BF16), vh).astype(BF16))
    y = _dot(jnp.concatenate(outs, axis=-1), wo_ref[...])
    o_ref[0] = _layer_norm(DN_ALPHA * x + y, g_ref[...], b_ref[...])


def _memory_attention_ln(x, kv, wq, wo, g, b):
    bsz, t, d = x.shape
    tm = min(TOKEN_TILE, t)
    row = pl.BlockSpec((1, tm, d), lambda i, j: (i, j, 0))
    return pl.pallas_call(
        _memattn_kernel,
        grid=(bsz, t // tm),
        in_specs=[row, pl.BlockSpec((1, kv.shape[1], 2 * d), lambda i, j: (i, 0, 0)),
                  _resident(wq.shape), _resident(wo.shape), _resident((1, d)), _resident((1, d))],
        out_specs=row,
        out_shape=jax.ShapeDtypeStruct(x.shape, F32),
        compiler_params=_params(("parallel", "parallel")),
        name="memory_attention_ln",
    )(x, kv, wq, wo, g.reshape(1, d), b.reshape(1, d))


def _combined_in_weight(w_in):
    gw, nh = GROUP_WIDTH, GROUP_HEADS
    o = 0
    gdn_qkv = w_in[:, o:o + 3 * gw]; o += 3 * gw
    gdn_z = w_in[:, o:o + gw]; o += gw
    gdn_a = w_in[:, o:o + nh]; o += nh
    gdn_b = w_in[:, o:o + nh]; o += nh
    fox_qkv = w_in[:, o:o + 3 * gw]; o += 3 * gw
    fox_f = w_in[:, o:o + nh]; o += nh
    conf = w_in[:, o:o + 2 * gw]; o += 2 * gw
    sb_qkv = w_in[:, o:o + 3 * gw]
    pad = jnp.zeros((w_in.shape[0], GATE_LANES - 3 * nh), w_in.dtype)
    return jnp.concatenate([gdn_qkv, gdn_z, fox_qkv, conf, sb_qkv, gdn_a, gdn_b, fox_f, pad], axis=1).astype(BF16)


def kernel(x, mem, ffn1_w_gate, ffn1_w_up, ffn1_w_down, ln_ffn1_g, ln_ffn1_b, w_in, gdn_conv_w, gdn_a_log, gdn_dt_bias, gdn_norm_g, fox_b_f, conf_dw_w, conf_dw_b, conf_norm_g, conf_norm_b, w_out, ln_mix_g, ln_mix_b, mem_w_q, mem_w_kv, mem_w_o, ln_mem_g, ln_mem_b, ffn2_w_gate, ffn2_w_up, ffn2_w_down, ln_ffn2_g, ln_ffn2_b):
    bf = lambda a: a.astype(BF16)
    for i in range(DEPTH):
        x = _ffn_ln(x, bf(ffn1_w_gate[i]), bf(ffn1_w_up[i]), bf(ffn1_w_down[i]), ln_ffn1_g[i], ln_ffn1_b[i])

        (gq, gk, gv, gz, fq, fk, fv, glu, sq, sk, sv, gates) = _in_projection(x, _combined_in_weight(w_in[i]))
        y_a = _gdn_mixer(gq, gk, gv, gz, gates, gdn_conv_w[i], gdn_a_log[i], gdn_dt_bias[i], gdn_norm_g[i])
        f_bias = jnp.zeros((1, GATE_LANES), F32).at[0, GATE_F0:GATE_F0 + GROUP_HEADS].set(fox_b_f[i])
        y_b = _fox_attention(fq, fk, fv, _fox_cumsum(gates, f_bias))
        y_c = _conv_module(glu, conf_dw_w[i], conf_dw_b[i], conf_norm_g[i], conf_norm_b[i])
        y_d = _sb_attention(sq, sk, sv)
        x = _out_projection_ln(x, y_a, y_b, y_c, y_d, bf(w_out[i]), ln_mix_g[i], ln_mix_b[i])

        kv = _memory_kv(mem, bf(mem_w_kv[i]))
        x = _memory_attention_ln(x, kv, bf(mem_w_q[i]), bf(mem_w_o[i]), ln_mem_g[i], ln_mem_b[i])

        x = _ffn_ln(x, bf(ffn2_w_gate[i]), bf(ffn2_w_up[i]), bf(ffn2_w_down[i]), ln_ffn2_g[i], ln_ffn2_b[i])
    return x
```

```python
import functools
import math

import jax
import jax.numpy as jnp
from jax import lax
from jax.experimental import pallas as pl
from jax.experimental.pallas import tpu as pltpu

F32 = jnp.float32
BF16 = jnp.bfloat16

D_MODEL = 1024
DEPTH = 2
GROUP_WIDTH = 256
HEAD_DIM = 64
GROUP_HEADS = 4
D_FF = 2816
SHORT_CONV = 4
CONF_KERNEL = 31
CONF_GROUPS = 4
GDN_CHUNK = 64
N_MEM = 256
MEM_HEADS = 4
MEM_HEAD_DIM = D_MODEL // MEM_HEADS
DN_ALPHA = float((2 * DEPTH) ** 0.25)
LN_EPS = 1e-5
RMS_EPS = 1e-6
L2_EPS = 1e-6
NEG_BIG = -1e30

GATE_LANES = 128
GATE_A0, GATE_B0, GATE_F0 = 0, 4, 8

TOKEN_TILE = 512
PAIR_WIDTH = 2 * HEAD_DIM
FOX_BLOCK = 512
SB_Q_BLOCK = 512
SB_K_BLOCK = 256
ATTN_SUBTILE = 256
SB_LOG_UNDERFLOW = -105.0
GDN_BLOCK = 256
CONV_BLOCK = 512
VMEM_LIMIT = 56 * 1024 * 1024


def _params(sem, vmem=VMEM_LIMIT):
    return pltpu.CompilerParams(dimension_semantics=sem, vmem_limit_bytes=vmem)


def _resident(shape):
    nd = len(shape)
    return pl.BlockSpec(shape, lambda *_: (0,) * nd, pipeline_mode=pl.Buffered(1))


def _layer_norm(y, g, b):
    mu = jnp.mean(y, axis=-1, keepdims=True)
    d = y - mu
    var = jnp.mean(d * d, axis=-1, keepdims=True)
    return d * lax.rsqrt(var + LN_EPS) * g + b


def _sigmoid(x):
    return 1.0 / (1.0 + jnp.exp(-x))


def _silu(x):
    return x * _sigmoid(x)


def _softplus(x):
    return jnp.maximum(x, 0.0) + jnp.log(1.0 + jnp.exp(-jnp.abs(x)))


def _dot(a, b):
    return jnp.dot(a, b, preferred_element_type=F32)


def _dot_nt(a, b):
    return lax.dot_general(a, b, (((1,), (1,)), ((), ())), preferred_element_type=F32)


def _dot_tn(a, b):
    return lax.dot_general(a, b, (((0,), (0,)), ((), ())), preferred_element_type=F32)


def _dot_f32(a, b):
    return jnp.dot(a, b, preferred_element_type=F32, precision=lax.Precision.HIGHEST)


def _ffn_kernel(x_ref, wg_ref, wu_ref, wd_ref, g_ref, b_ref, o_ref, *, n_split):
    x = x_ref[0]
    xb = x.astype(BF16)
    fc = D_FF // n_split
    acc = None
    for c in range(n_split):
        h = _dot(xb, wg_ref[:, c * fc:(c + 1) * fc])
        u = _dot(xb, wu_ref[:, c * fc:(c + 1) * fc])
        a = (_silu(h) * u).astype(BF16)
        part = _dot(a, wd_ref[c * fc:(c + 1) * fc, :])
        acc = part if acc is None else acc + part
    o_ref[0] = _layer_norm(DN_ALPHA * x + 0.5 * acc, g_ref[...], b_ref[...])


def _ffn_ln(x, wg, wu, wd, g, b):
    bsz, t, d = x.shape
    tm = min(TOKEN_TILE, t)
    row = pl.BlockSpec((1, tm, d), lambda i, j: (i, j, 0))
    return pl.pallas_call(
        functools.partial(_ffn_kernel, n_split=2),
        grid=(bsz, t // tm),
        in_specs=[row, _resident(wg.shape), _resident(wu.shape), _resident(wd.shape),
                  _resident((1, d)), _resident((1, d))],
        out_specs=row,
        out_shape=jax.ShapeDtypeStruct(x.shape, F32),
        compiler_params=_params(("parallel", "parallel")),
        name="ffn_ln",
    )(x, wg, wu, wd, g.reshape(1, d), b.reshape(1, d))


def _inproj_kernel(x_ref, w_ref, gq_ref, gk_ref, gv_ref, gz_ref, fq_ref, fk_ref, fv_ref,
                   glu_ref, sq_ref, sk_ref, sv_ref, gate_ref):
    xb = x_ref[0].astype(BF16)
    gw = GROUP_WIDTH

    def group(i, width=gw):
        return _dot(xb, w_ref[:, i * gw:i * gw + width])

    qk_scale = HEAD_DIM ** -0.5
    gq_ref[0] = group(0)
    gk_ref[0] = group(1)
    gv_ref[0] = group(2)
    gz_ref[0] = group(3)
    fq_ref[0] = (group(4) * qk_scale).astype(BF16)
    fk_ref[0] = group(5).astype(BF16)
    fv_ref[0] = group(6).astype(BF16)
    glu_ref[0] = group(7) * _sigmoid(group(8))
    sq_ref[0] = (group(9) * qk_scale).astype(BF16)
    sk_ref[0] = group(10).astype(BF16)
    sv_ref[0] = group(11).astype(BF16)
    gate_ref[0] = group(12, GATE_LANES)


def _in_projection(x, w_cat):
    bsz, t, d = x.shape
    tm = min(TOKEN_TILE, t)
    row = lambda c: pl.BlockSpec((1, tm, c), lambda i, j: (i, j, 0))
    wide = lambda dt: jax.ShapeDtypeStruct((bsz, t, GROUP_WIDTH), dt)
    out_shape = ([wide(F32)] * 4 + [wide(BF16)] * 3 + [wide(F32)] + [wide(BF16)] * 3
                 + [jax.ShapeDtypeStruct((bsz, t, GATE_LANES), F32)])
    out_specs = [row(GROUP_WIDTH)] * 11 + [row(GATE_LANES)]
    return pl.pallas_call(
        _inproj_kernel,
        grid=(bsz, t // tm),
        in_specs=[row(d), _resident(w_cat.shape)],
        out_specs=out_specs,
        out_shape=out_shape,
        compiler_params=_params(("parallel", "parallel")),
        name="in_projection",
    )(x, w_cat)


def _fox_cum_kernel(gate_ref, bias_ref, o_ref, carry_ref):
    @pl.when(pl.program_id(1) == 0)
    def _():
        carry_ref[...] = jnp.zeros_like(carry_ref)

    tb = gate_ref.shape[1]
    logit = gate_ref[0] + bias_ref[...]
    log_f = -_softplus(-logit)
    rows = lax.broadcasted_iota(jnp.int32, (tb, tb), 0)
    cols = lax.broadcasted_iota(jnp.int32, (tb, tb), 1)
    tri = (cols <= rows).astype(F32)
    cum = _dot_f32(tri, log_f) + carry_ref[...]
    carry_ref[...] = cum[tb - 1:tb, :]
    o_ref[0, 0] = cum.T


def _fox_cumsum(gates, bias_row):
    bsz, t, _ = gates.shape
    tb = min(FOX_BLOCK, t)
    return pl.pallas_call(
        _fox_cum_kernel,
        grid=(bsz, t // tb),
        in_specs=[pl.BlockSpec((1, tb, GATE_LANES), lambda i, j: (i, j, 0)), _resident((1, GATE_LANES))],
        out_specs=pl.BlockSpec((1, 1, GATE_LANES, tb), lambda i, j: (i, j, 0, 0)),
        out_shape=jax.ShapeDtypeStruct((bsz, t // tb, GATE_LANES, tb), F32),
        scratch_shapes=[pltpu.VMEM((1, GATE_LANES), F32)],
        compiler_params=_params(("parallel", "arbitrary")),
        name="fox_cumsum",
    )(gates, bias_row)


def _causal_masks(blk):
    rows = lax.broadcasted_iota(jnp.int32, (blk, blk), 0)
    cols = lax.broadcasted_iota(jnp.int32, (blk, blk), 1)
    return cols <= rows, cols < rows


def _stack_heads(x2):
    lane = lax.broadcasted_iota(jnp.int32, x2.shape, 1)
    zero = jnp.zeros_like(x2)
    return jnp.concatenate([jnp.where(lane < HEAD_DIM, x2, zero), jnp.where(lane >= HEAD_DIM, x2, zero)], axis=0)


def _unstack_heads(y, rows):
    lane = lax.broadcasted_iota(jnp.int32, (rows, PAIR_WIDTH), 1)
    return jnp.where(lane < HEAD_DIM, y[:rows], y[rows:])


def _skewed(n_tiles, stages):
    for t in range(n_tiles + len(stages) - 1):
        for k in reversed(range(len(stages))):
            if 0 <= t - k < n_tiles:
                stages[k](t - k)


def _fox_kernel(q_ref, k_ref, v_ref, ck_ref, o_ref, *, sub):
    blk = q_ref.shape[1]
    pair = pl.program_id(1)
    qi = pl.program_id(2)
    qs = _stack_heads(q_ref[0])
    n_sub = 2 * blk // sub
    rows = lax.broadcasted_iota(jnp.int32, (sub, blk), 0)
    cols = lax.broadcasted_iota(jnp.int32, (sub, blk), 1)
    ones = jnp.ones((blk, PAIR_WIDTH), BF16)

    def step(j, carry, masked):
        ms, accs = carry
        start = pl.multiple_of(j * blk, blk)
        kb = k_ref[0, pl.ds(start, blk), :]
        vb = jnp.concatenate([v_ref[0, pl.ds(start, blk), :], ones], axis=1)
        ms, accs = list(ms), list(accs)
        s, p, alpha = [None] * n_sub, [None] * n_sub, [None] * n_sub

        def logits(i):
            head, off = divmod(i * sub, blk)
            si = _dot_nt(qs[i * sub:(i + 1) * sub], kb) - ck_ref[0, j, pl.ds(2 * pair + head, 1), :]
            s[i] = jnp.where(cols <= rows + off, si, NEG_BIG) if masked else si

        def probs(i):
            m_new = jnp.maximum(ms[i], jnp.max(s[i], axis=-1, keepdims=True))
            p[i] = jnp.exp(s[i] - m_new).astype(BF16)
            alpha[i] = jnp.exp(ms[i] - m_new)
            ms[i] = m_new

        def values(i):
            accs[i] = alpha[i] * accs[i] + _dot(p[i], vb)

        _skewed(n_sub, [logits, probs, values])
        return tuple(ms), tuple(accs)

    init = (tuple(jnp.full((sub, 1), NEG_BIG, F32) for _ in range(n_sub)),
            tuple(jnp.zeros((sub, 2 * PAIR_WIDTH), F32) for _ in range(n_sub)))
    carry = lax.fori_loop(0, qi, lambda j, c: step(j, c, False), init)
    _, accs = step(qi, carry, True)
    acc = jnp.concatenate(accs, axis=0)
    o_ref[0] = _unstack_heads(acc[:, :PAIR_WIDTH] / acc[:, PAIR_WIDTH:PAIR_WIDTH + 1], blk)


def _fox_attention(q, k, v, cum_t):
    bsz, t, width = q.shape
    blk = min(FOX_BLOCK, t)
    nk = t // blk
    whole = pl.BlockSpec((1, t, PAIR_WIDTH), lambda b, p, i: (b, 0, p))
    qblk = pl.BlockSpec((1, blk, PAIR_WIDTH), lambda b, p, i: (b, i, p))
    return pl.pallas_call(
        functools.partial(_fox_kernel, sub=min(ATTN_SUBTILE, blk)),
        grid=(bsz, width // PAIR_WIDTH, nk),
        in_specs=[qblk, whole, whole,
                  pl.BlockSpec((1, nk, 8, blk), lambda b, p, i: (b, 0, GATE_F0 // 8, 0))],
        out_specs=qblk,
        out_shape=jax.ShapeDtypeStruct((bsz, t, width), F32),
        compiler_params=_params(("parallel", "parallel", "arbitrary")),
        name="fox_attention",
    )(q, k, v, cum_t)


def _sb_kernel(q_ref, k_ref, v_ref, o_ref, *, tk, sub):
    tq = q_ref.shape[1]
    per = tq // tk
    qi = pl.program_id(2)
    qs = _stack_heads(q_ref[0])
    n_sub = 2 * tq // sub
    rows = lax.broadcasted_iota(jnp.int32, (sub, tk), 0)
    cols = lax.broadcasted_iota(jnp.int32, (sub, tk), 1)
    suffix = _causal_masks(tk)[0].astype(BF16)

    def step(j, carry, diag):
        rests, accs = carry
        start = pl.multiple_of(j * tk, tk)
        kb = k_ref[0, pl.ds(start, tk), :]
        vb = v_ref[0, pl.ds(start, tk), :]
        rests, accs = list(rests), list(accs)
        z, split, w, strict = [None] * n_sub, [None] * n_sub, [None] * n_sub, [None] * n_sub

        def logits(i):
            z[i] = _dot_nt(qs[i * sub:(i + 1) * sub], kb)
            if diag is not None:
                strict[i] = cols + diag < rows + (i * sub) % tq

        def keep(i):
            log_keep = -_softplus(z[i])
            if diag is not None:
                log_keep = jnp.where(strict[i], log_keep, 0.0)
            split[i] = jnp.concatenate(_split_bf16(log_keep), axis=0)

        def weights(i):
            tails = _dot(split[i], suffix)
            tail = tails[:sub] + tails[sub:]
            wi = jnp.exp(z[i] + tail + rests[i])
            if diag is not None:
                wi = jnp.where(strict[i], wi, 0.0)
            w[i] = wi.astype(BF16)
            rests[i] = rests[i] + tail[:, 0:1]

        def values(i):
            accs[i] = accs[i] + _dot(w[i], vb)

        _skewed(n_sub, [logits, keep, weights, values])
        return tuple(rests), tuple(accs)

    carry = (tuple(jnp.zeros((sub, 1), F32) for _ in range(n_sub)),
             tuple(jnp.zeros((sub, PAIR_WIDTH), F32) for _ in range(n_sub)))
    for d in reversed(range(per)):
        carry = step(qi * per + d, carry, d * tk)

    def largest(rests):
        return functools.reduce(jnp.maximum, [jnp.max(r) for r in rests])

    def more(state):
        i, top, _ = state
        return (i < qi * per) & (top > SB_LOG_UNDERFLOW)

    def sweep(state):
        i, _, c = state
        c = step(qi * per - 1 - i, c, None)
        return i + 1, largest(c[0]), c

    _, _, (_, accs) = lax.while_loop(more, sweep, (jnp.int32(0), largest(carry[0]), carry))
    o_ref[0] = _unstack_heads(jnp.concatenate(accs, axis=0), tq)


def _sb_attention(q, k, v):
    bsz, t, width = q.shape
    tq = min(SB_Q_BLOCK, t)
    tk = min(SB_K_BLOCK, tq)
    whole = pl.BlockSpec((1, t, PAIR_WIDTH), lambda b, p, i: (b, 0, p))
    qblk = pl.BlockSpec((1, tq, PAIR_WIDTH), lambda b, p, i: (b, i, p))
    return pl.pallas_call(
        functools.partial(_sb_kernel, tk=tk, sub=min(ATTN_SUBTILE, tq)),
        grid=(bsz, width // PAIR_WIDTH, t // tq),
        in_specs=[qblk, whole, whole],
        out_specs=qblk,
        out_shape=jax.ShapeDtypeStruct((bsz, t, width), F32),
        compiler_params=_params(("parallel", "parallel", "arbitrary")),
        name="sb_attention",
    )(q, k, v)


CONV_HALO = 32


def _conv_kernel(prev_ref, cur_ref, w_ref, b_ref, ng_ref, nb_ref, o_ref, xs_ref):
    tb = cur_ref.shape[1]
    first = pl.program_id(1) == 0
    xs_ref[0:CONV_HALO, :] = jnp.where(first, 0.0, prev_ref[0])
    xs_ref[CONV_HALO:CONV_HALO + tb, :] = cur_ref[0]
    base = CONV_HALO - (CONF_KERNEL - 1)
    acc = jnp.zeros((tb, GROUP_WIDTH), F32) + b_ref[...]
    for tap in range(CONF_KERNEL):
        acc = acc + w_ref[tap:tap + 1, :] * xs_ref[base + tap:base + tap + tb, :]
    gsz = GROUP_WIDTH // CONF_GROUPS
    parts = []
    for g in range(CONF_GROUPS):
        cg = acc[:, g * gsz:(g + 1) * gsz]
        mu = jnp.mean(cg, axis=-1, keepdims=True)
        d = cg - mu
        var = jnp.mean(d * d, axis=-1, keepdims=True)
        parts.append(d * lax.rsqrt(var + LN_EPS))
    hn = jnp.concatenate(parts, axis=-1) * ng_ref[...] + nb_ref[...]
    o_ref[0] = _silu(hn)


def _conv_module(glu, w, b, ng, nb):
    bsz, t, c = glu.shape
    tb = min(CONV_BLOCK, t)
    per = tb // CONV_HALO
    vec = lambda a: a.reshape(1, c)
    return pl.pallas_call(
        _conv_kernel,
        grid=(bsz, t // tb),
        in_specs=[pl.BlockSpec((1, CONV_HALO, c), lambda i, j: (i, jnp.maximum(j * per - 1, 0), 0)),
                  pl.BlockSpec((1, tb, c), lambda i, j: (i, j, 0)),
                  _resident(w.shape), _resident((1, c)), _resident((1, c)), _resident((1, c))],
        out_specs=pl.BlockSpec((1, tb, c), lambda i, j: (i, j, 0)),
        out_shape=jax.ShapeDtypeStruct(glu.shape, F32),
        scratch_shapes=[pltpu.VMEM((CONV_HALO + tb, c), F32)],
        compiler_params=_params(("parallel", "parallel")),
        name="conv_module",
    )(glu, glu, w, vec(b), vec(ng), vec(nb))


GDN_HALO = 8


def _split_bf16(x):
    hi = x.astype(BF16)
    return hi, (x - hi.astype(F32)).astype(BF16)


def _group_sum(x, ones_bd):
    rows = x.shape[0]
    hi, lo = _split_bf16(x)
    r = _dot(jnp.concatenate([hi, lo], axis=0), ones_bd)
    return r[:rows] + r[rows:]


def _unit_lower_inverses(lows, rows, cols):
    n = lows[0].shape[0]
    eye = jnp.where(rows == cols, 1.0, 0.0)
    first = (rows % 2 == 1) & (cols == rows - 1)
    invs = [eye - jnp.where(first, low, 0.0) for low in lows]
    size = 2
    while size < GDN_CHUNK:
        rb = rows // size
        level = (rb % 2 == 1) & (cols // size == rb - 1)
        splits = [_split_bf16(inv) for inv in invs]
        xs = []
        for low, (d_hi, d_lo) in zip(lows, splits):
            x2 = _dot(jnp.where(level, low, 0.0).astype(BF16), jnp.concatenate([d_hi, d_lo], axis=1))
            xs.append(_split_bf16(x2[:, :n] + x2[:, n:]))
        nxt = []
        for inv, (d_hi, d_lo), (x_hi, x_lo) in zip(invs, splits, xs):
            y4 = _dot(jnp.concatenate([d_hi, d_lo], axis=0), jnp.concatenate([x_hi, x_lo], axis=1))
            nxt.append(inv - (y4[:n, :n] + y4[:n, n:] + y4[n:, :n] + y4[n:, n:]))
        invs = nxt
        size *= 2
    return invs


def _gdn_kernel(qp_ref, kp_ref, vp_ref, q_ref, k_ref, v_ref, z_ref, gate_ref, cw_ref, alog_ref, dtb_ref,
                ng_ref, o_ref, xs_ref, y_ref, state_ref):
    tb = q_ref.shape[1]
    c = GDN_CHUNK
    n = 2 * c
    gw = GROUP_WIDTH
    first = pl.program_id(1) == 0

    @pl.when(first)
    def _():
        state_ref[...] = jnp.zeros_like(state_ref)

    r2 = lax.broadcasted_iota(jnp.int32, (gw, gw), 0)
    c2 = lax.broadcasted_iota(jnp.int32, (gw, gw), 1)
    ones_bd = (r2 // HEAD_DIM == c2 // HEAD_DIM).astype(BF16)

    base = GDN_HALO - (SHORT_CONV - 1)

    def conv_silu(idx, p_ref, c_ref):
        xs_ref[0:GDN_HALO, :] = jnp.where(first, 0.0, p_ref[0])
        xs_ref[GDN_HALO:GDN_HALO + tb, :] = c_ref[0]
        acc = None
        for tap in range(SHORT_CONV):
            term = cw_ref[tap:tap + 1, idx * gw:(idx + 1) * gw] * xs_ref[base + tap:base + tap + tb, :]
            acc = term if acc is None else acc + term
        return _silu(acc)

    def l2_normalize(y):
        return y * lax.rsqrt(_group_sum(y * y, ones_bd) + L2_EPS)

    qn = l2_normalize(conv_silu(0, qp_ref, q_ref)) * (HEAD_DIM ** -0.5)
    kn = l2_normalize(conv_silu(1, kp_ref, k_ref))
    vn = conv_silu(2, vp_ref, v_ref)

    gates = gate_ref[0]
    log_decay = -jnp.exp(alog_ref[...]) * _softplus(gates + dtb_ref[...])
    beta_all = _sigmoid(gates)
    rows_t = lax.broadcasted_iota(jnp.int32, (tb, tb), 0)
    cols_t = lax.broadcasted_iota(jnp.int32, (tb, tb), 1)
    chunk_tri = ((cols_t <= rows_t) & (cols_t // c == rows_t // c)).astype(F32)
    gcum = _dot_f32(chunk_tri, log_decay)
    lane_bcast = lambda x, lane: jnp.broadcast_to(x[:, lane:lane + 1], (tb, PAIR_WIDTH))
    g_wide = [lane_bcast(gcum, GATE_A0 + h) for h in range(GROUP_HEADS)]
    b_wide = [lane_bcast(beta_all, GATE_B0 + h) for h in range(GROUP_HEADS)]
    low_half = lax.broadcasted_iota(jnp.int32, (tb, PAIR_WIDTH), 1) < HEAD_DIM

    rows = lax.broadcasted_iota(jnp.int32, (n, n), 0)
    cols = lax.broadcasted_iota(jnp.int32, (n, n), 1)
    same_head = rows // c == cols // c
    lower_incl = same_head & (cols <= rows)
    strict_lower = same_head & (cols < rows)

    pairs = []
    for p in range(GROUP_HEADS // 2):
        ls = slice(p * PAIR_WIDTH, (p + 1) * PAIR_WIDTH)
        g_nat = jnp.where(low_half, g_wide[2 * p], g_wide[2 * p + 1])
        b_nat = jnp.where(low_half, b_wide[2 * p], b_wide[2 * p + 1])
        eg = jnp.exp(g_nat)
        k_beta = kn[:, ls] * b_nat
        pairs.append(dict(ls=ls, g_nat=g_nat, q=qn[:, ls], k=kn[:, ls], q_dec=qn[:, ls] * eg, k_beta=k_beta,
                          v_beta=vn[:, ls] * b_nat, kb_eg=k_beta * eg))

    n_chunks = tb // c
    systems = [(ci, p) for ci in range(n_chunks) for p in range(len(pairs))]
    chunk_rows = lambda ci: slice(ci * c, (ci + 1) * c)

    lkks, a_qks = [], []
    for ci, p in systems:
        d, r = pairs[p], chunk_rows(ci)
        g_col = jnp.concatenate([g_wide[2 * p][r], g_wide[2 * p + 1][r]], axis=0)
        decay = jnp.exp(jnp.where(lower_incl, g_col - g_col.T, -jnp.inf))
        k_st = _stack_heads(d["k"][r]).astype(BF16)
        lhs = jnp.concatenate([_stack_heads(d["k_beta"][r]), _stack_heads(d["q"][r])], axis=0).astype(BF16)
        gram = _dot_nt(lhs, k_st)
        lkks.append(jnp.where(strict_lower, gram[:n] * decay, 0.0))
        a_qks.append((gram[n:] * decay).astype(BF16))
    t_invs = _unit_lower_inverses(lkks, rows, cols)
    uws = []
    for (ci, p), t_inv in zip(systems, t_invs):
        d, r = pairs[p], chunk_rows(ci)
        rhs = jnp.concatenate([_stack_heads(d["v_beta"][r]), _stack_heads(d["kb_eg"][r])], axis=1)
        uws.append(_dot(t_inv.astype(BF16), rhs.astype(BF16)))

    states = [state_ref[p] for p in range(len(pairs))]
    for idx, (ci, p) in enumerate(systems):
        d, r = pairs[p], chunk_rows(ci)
        uw, state = uws[idx], states[p]
        g_last = d["g_nat"][ci * c + c - 1:ci * c + c, :]
        k_dec = _stack_heads(d["k"][r] * jnp.exp(g_last - d["g_nat"][r])).astype(BF16)
        wq = _dot(jnp.concatenate([uw[:, n:].astype(BF16), _stack_heads(d["q_dec"][r]).astype(BF16)], axis=0),
                  state.astype(BF16))
        vnb = (uw[:, :n] - wq[:n]).astype(BF16)
        o_st = wq[n:] + _dot(a_qks[idx], vnb)
        states[p] = state * jnp.exp(g_last) + _dot_tn(k_dec, vnb)
        y_ref[r, d["ls"]] = o_st[:c] + o_st[c:]
    for p in range(len(pairs)):
        state_ref[p] = states[p]

    o = y_ref[...]
    mean_sq = _group_sum(o * o, ones_bd) * (1.0 / HEAD_DIM)
    o_ref[0] = o * lax.rsqrt(mean_sq + RMS_EPS) * ng_ref[...] * _silu(z_ref[0])


def _gdn_mixer(q, k, v, z, gates, conv_w, a_log, dt_bias, norm_g):
    bsz, t, width = q.shape
    nh = GROUP_HEADS
    tb = min(GDN_BLOCK, t)
    per = tb // GDN_HALO
    cur = pl.BlockSpec((1, tb, width), lambda i, j: (i, j, 0))
    prev = pl.BlockSpec((1, GDN_HALO, width), lambda i, j: (i, jnp.maximum(j * per - 1, 0), 0))
    lane_row = lambda vals, off: jnp.zeros((1, GATE_LANES), F32).at[0, off:off + nh].set(vals)
    return pl.pallas_call(
        _gdn_kernel,
        grid=(bsz, t // tb),
        in_specs=[prev, prev, prev, cur, cur, cur, cur,
                  pl.BlockSpec((1, tb, GATE_LANES), lambda i, j: (i, j, 0)),
                  _resident(conv_w.shape), _resident((1, GATE_LANES)), _resident((1, GATE_LANES)),
                  _resident((1, width))],
        out_specs=cur,
        out_shape=jax.ShapeDtypeStruct(q.shape, F32),
        scratch_shapes=[pltpu.VMEM((GDN_HALO + tb, width), F32), pltpu.VMEM((tb, width), F32),
                        pltpu.VMEM((nh // 2, PAIR_WIDTH, PAIR_WIDTH), F32)],
        compiler_params=_params(("parallel", "arbitrary")),
        name="gdn_mixer",
    )(q, k, v, q, k, v, z, gates, conv_w, lane_row(a_log, GATE_A0), lane_row(dt_bias, GATE_A0),
      jnp.tile(norm_g, nh).reshape(1, width))


def _outproj_kernel(x_ref, ya_ref, yb_ref, yc_ref, yd_ref, w_ref, g_ref, b_ref, o_ref):
    gw = GROUP_WIDTH
    mix = None
    for group, y_ref in enumerate((ya_ref, yb_ref, yc_ref, yd_ref)):
        part = _dot(y_ref[0].astype(BF16), w_ref[group * gw:(group + 1) * gw, :])
        mix = part if mix is None else mix + part
    o_ref[0] = _layer_norm(DN_ALPHA * x_ref[0] + mix, g_ref[...], b_ref[...])


def _out_projection_ln(x, ya, yb, yc, yd, w_out, g, b):
    bsz, t, d = x.shape
    tm = min(TOKEN_TILE, t)
    row = lambda c: pl.BlockSpec((1, tm, c), lambda i, j: (i, j, 0))
    return pl.pallas_call(
        _outproj_kernel,
        grid=(bsz, t // tm),
        in_specs=[row(d)] + [row(GROUP_WIDTH)] * 4 + [_resident(w_out.shape),
                  _resident((1, d)), _resident((1, d))],
        out_specs=row(d),
        out_shape=jax.ShapeDtypeStruct(x.shape, F32),
        compiler_params=_params(("parallel", "parallel")),
        name="out_projection_ln",
    )(x, ya, yb, yc, yd, w_out, g.reshape(1, d), b.reshape(1, d))


def _memkv_kernel(m_ref, w_ref, o_ref):
    o_ref[0] = _dot(m_ref[0].astype(BF16), w_ref[...]).astype(o_ref.dtype)


def _memory_kv(mem, w_kv):
    bsz, m, d = mem.shape
    return pl.pallas_call(
        _memkv_kernel,
        grid=(bsz,),
        in_specs=[pl.BlockSpec((1, m, d), lambda i: (i, 0, 0)), _resident(w_kv.shape)],
        out_specs=pl.BlockSpec((1, m, 2 * d), lambda i: (i, 0, 0)),
        out_shape=jax.ShapeDtypeStruct((bsz, m, 2 * d), BF16),
        compiler_params=_params(("parallel",)),
        name="memory_kv",
    )(mem, w_kv)


def _memattn_kernel(x_ref, kv_ref, wq_ref, wo_ref, g_ref, b_ref, o_ref):
    x = x_ref[0]
    q = _dot(x.astype(BF16), wq_ref[...])
    hd = MEM_HEAD_DIM
    outs = []
    for h in range(MEM_HEADS):
        qh = (q[:, h * hd:(h + 1) * hd] * (hd ** -0.5)).astype(BF16)
        kh = kv_ref[0, :, h * hd:(h + 1) * hd]
        vh = kv_ref[0, :, D_MODEL + h * hd:D_MODEL + (h + 1) * hd]
        s = _dot_nt(qh, kh)
        p = jnp.exp(s - jnp.max(s, axis=-1, keepdims=True))
        p = p / jnp.sum(p, axis=-1, keepdims=True)
        outs.append(_dot(p.astype(BF16), vh).astype(BF16))
    y = _dot(jnp.concatenate(outs, axis=-1), wo_ref[...])
    o_ref[0] = _layer_norm(DN_ALPHA * x + y, g_ref[...], b_ref[...])


def _memory_attention_ln(x, kv, wq, wo, g, b):
    bsz, t, d = x.shape
    tm = min(TOKEN_TILE, t)
    row = pl.BlockSpec((1, tm, d), lambda i, j: (i, j, 0))
    return pl.pallas_call(
        _memattn_kernel,
        grid=(bsz, t // tm),
        in_specs=[row, pl.BlockSpec((1, kv.shape[1], 2 * d), lambda i, j: (i, 0, 0)),
                  _resident(wq.shape), _resident(wo.shape), _resident((1, d)), _resident((1, d))],
        out_specs=row,
        out_shape=jax.ShapeDtypeStruct(x.shape, F32),
        compiler_params=_params(("parallel", "parallel")),
        name="memory_attention_ln",
    )(x, kv, wq, wo, g.reshape(1, d), b.reshape(1, d))


def _combined_in_weight(w_in):
    gw, nh = GROUP_WIDTH, GROUP_HEADS
    o = 0
    gdn_qkv = w_in[:, o:o + 3 * gw]; o += 3 * gw
    gdn_z = w_in[:, o:o + gw]; o += gw
    gdn_a = w_in[:, o:o + nh]; o += nh
    gdn_b = w_in[:, o:o + nh]; o += nh
    fox_qkv = w_in[:, o:o + 3 * gw]; o += 3 * gw
    fox_f = w_in[:, o:o + nh]; o += nh
    conf = w_in[:, o:o + 2 * gw]; o += 2 * gw
    sb_qkv = w_in[:, o:o + 3 * gw]
    pad = jnp.zeros((w_in.shape[0], GATE_LANES - 3 * nh), w_in.dtype)
    return jnp.concatenate([gdn_qkv, gdn_z, fox_qkv, conf, sb_qkv, gdn_a, gdn_b, fox_f, pad], axis=1).astype(BF16)


def kernel(x, mem, ffn1_w_gate, ffn1_w_up, ffn1_w_down, ln_ffn1_g, ln_ffn1_b, w_in, gdn_conv_w, gdn_a_log, gdn_dt_bias, gdn_norm_g, fox_b_f, conf_dw_w, conf_dw_b, conf_norm_g, conf_norm_b, w_out, ln_mix_g, ln_mix_b, mem_w_q, mem_w_kv, mem_w_o, ln_mem_g, ln_mem_b, ffn2_w_gate, ffn2_w_up, ffn2_w_down, ln_ffn2_g, ln_ffn2_b):
    bf = lambda a: a.astype(BF16)
    for i in range(DEPTH):
        x = _ffn_ln(x, bf(ffn1_w_gate[i]), bf(ffn1_w_up[i]), bf(ffn1_w_down[i]), ln_ffn1_g[i], ln_ffn1_b[i])

        (gq, gk, gv, gz, fq, fk, fv, glu, sq, sk, sv, gates) = _in_projection(x, _combined_in_weight(w_in[i]))
        y_a = _gdn_mixer(gq, gk, gv, gz, gates, gdn_conv_w[i], gdn_a_log[i], gdn_dt_bias[i], gdn_norm_g[i])
        f_bias = jnp.zeros((1, GATE_LANES), F32).at[0, GATE_F0:GATE_F0 + GROUP_HEADS].set(fox_b_f[i])
        y_b = _fox_attention(fq, fk, fv, _fox_cumsum(gates, f_bias))
        y_c = _conv_module(glu, conf_dw_w[i], conf_dw_b[i], conf_norm_g[i], conf_norm_b[i])
        y_d = _sb_attention(sq, sk, sv)
        x = _out_projection_ln(x, y_a, y_b, y_c, y_d, bf(w_out[i]), ln_mix_g[i], ln_mix_b[i])

        kv = _memory_kv(mem, bf(mem_w_kv[i]))
        x = _memory_attention_ln(x, kv, bf(mem_w_q[i]), bf(mem_w_o[i]), ln_mem_g[i], ln_mem_b[i])

        x = _ffn_ln(x, bf(ffn2_w_gate[i]), bf(ffn2_w_up[i]), bf(ffn2_w_down[i]), ln_ffn2_g[i], ln_ffn2_b[i])
    return x
```

```python
import functools
import math

import jax
import jax.numpy as jnp
from jax import lax
from jax.experimental import pallas as pl
from jax.experimental.pallas import tpu as pltpu

F32 = jnp.float32
BF16 = jnp.bfloat16

D_MODEL = 1024
DEPTH = 2
GROUP_WIDTH = 256
HEAD_DIM = 64
GROUP_HEADS = 4
D_FF = 2816
SHORT_CONV = 4
CONF_KERNEL = 31
CONF_GROUPS = 4
GDN_CHUNK = 64
N_MEM = 256
MEM_HEADS = 4
MEM_HEAD_DIM = D_MODEL // MEM_HEADS
DN_ALPHA = float((2 * DEPTH) ** 0.25)
LN_EPS = 1e-5
RMS_EPS = 1e-6
L2_EPS = 1e-6
NEG_BIG = -1e30

GATE_LANES = 128
GATE_A0, GATE_B0, GATE_F0 = 0, 4, 8

TOKEN_TILE = 512
MXU_WIDTH = 256
FFN_SPLITS = (0, 6 * MXU_WIDTH, D_FF)
PAIR_WIDTH = 2 * HEAD_DIM
FOX_BLOCK = 512
SB_Q_BLOCK = 512
SB_K_BLOCK = 256
ATTN_SUBTILE = 256
SB_LOG_UNDERFLOW = -105.0
FOX_LOG_UNDERFLOW = -106.0
GDN_BLOCK = 256
CONV_BLOCK = 512
VMEM_LIMIT = 56 * 1024 * 1024


def _params(sem, vmem=VMEM_LIMIT):
    return pltpu.CompilerParams(dimension_semantics=sem, vmem_limit_bytes=vmem)


def _resident(shape):
    nd = len(shape)
    return pl.BlockSpec(shape, lambda *_: (0,) * nd, pipeline_mode=pl.Buffered(1))


def _layer_norm(y, g, b):
    mu = jnp.mean(y, axis=-1, keepdims=True)
    d = y - mu
    var = jnp.mean(d * d, axis=-1, keepdims=True)
    return d * lax.rsqrt(var + LN_EPS) * g + b


def _sigmoid(x):
    return 1.0 / (1.0 + jnp.exp(-x))


def _silu(x):
    return x * _sigmoid(x)


def _softplus(x):
    return jnp.maximum(x, 0.0) + jnp.log(1.0 + jnp.exp(-jnp.abs(x)))


def _dot(a, b):
    return jnp.dot(a, b, preferred_element_type=F32)


def _dot_nt(a, b):
    return lax.dot_general(a, b, (((1,), (1,)), ((), ())), preferred_element_type=F32)


def _dot_tn(a, b):
    return lax.dot_general(a, b, (((0,), (0,)), ((), ())), preferred_element_type=F32)


def _dot_f32(a, b):
    return jnp.dot(a, b, preferred_element_type=F32, precision=lax.Precision.HIGHEST)


def _ffn_kernel(x_ref, wg_ref, wu_ref, wd_ref, g_ref, b_ref, o_ref):
    x = x_ref[0]
    xb = x.astype(BF16)
    acc = None
    for lo, hi in zip(FFN_SPLITS[:-1], FFN_SPLITS[1:]):
        h = _dot(xb, wg_ref[:, lo:hi])
        u = _dot(xb, wu_ref[:, lo:hi])
        a = (_silu(h) * u).astype(BF16)
        part = _dot(a, wd_ref[lo:hi, :])
        acc = part if acc is None else acc + part
    o_ref[0] = _layer_norm(DN_ALPHA * x + 0.5 * acc, g_ref[...], b_ref[...])


def _ffn_ln(x, wg, wu, wd, g, b):
    bsz, t, d = x.shape
    tm = min(TOKEN_TILE, t)
    row = pl.BlockSpec((1, tm, d), lambda i, j: (i, j, 0))
    return pl.pallas_call(
        _ffn_kernel,
        grid=(bsz, t // tm),
        in_specs=[row, _resident(wg.shape), _resident(wu.shape), _resident(wd.shape),
                  _resident((1, d)), _resident((1, d))],
        out_specs=row,
        out_shape=jax.ShapeDtypeStruct(x.shape, F32),
        compiler_params=_params(("parallel", "parallel")),
        name="ffn_ln",
    )(x, wg, wu, wd, g.reshape(1, d), b.reshape(1, d))


def _inproj_kernel(x_ref, w_ref, gq_ref, gk_ref, gv_ref, gz_ref, fq_ref, fk_ref, fv_ref,
                   glu_ref, sq_ref, sk_ref, sv_ref, gate_ref):
    xb = x_ref[0].astype(BF16)
    gw = GROUP_WIDTH

    def group(i, width=gw):
        return _dot(xb, w_ref[:, i * gw:i * gw + width])

    qk_scale = HEAD_DIM ** -0.5
    gq_ref[0] = group(0)
    gk_ref[0] = group(1)
    gv_ref[0] = group(2)
    gz_ref[0] = group(3)
    fq_ref[0] = (group(4) * qk_scale).astype(BF16)
    fk_ref[0] = group(5).astype(BF16)
    fv_ref[0] = group(6).astype(BF16)
    glu_ref[0] = group(7) * _sigmoid(group(8))
    sq_ref[0] = (group(9) * qk_scale).astype(BF16)
    sk_ref[0] = group(10).astype(BF16)
    sv_ref[0] = group(11).astype(BF16)
    gate_ref[0] = group(12, GATE_LANES)


def _in_projection(x, w_cat):
    bsz, t, d = x.shape
    tm = min(TOKEN_TILE, t)
    row = lambda c: pl.BlockSpec((1, tm, c), lambda i, j: (i, j, 0))
    wide = lambda dt: jax.ShapeDtypeStruct((bsz, t, GROUP_WIDTH), dt)
    out_shape = ([wide(F32)] * 4 + [wide(BF16)] * 3 + [wide(F32)] + [wide(BF16)] * 3
                 + [jax.ShapeDtypeStruct((bsz, t, GATE_LANES), F32)])
    out_specs = [row(GROUP_WIDTH)] * 11 + [row(GATE_LANES)]
    return pl.pallas_call(
        _inproj_kernel,
        grid=(bsz, t // tm),
        in_specs=[row(d), _resident(w_cat.shape)],
        out_specs=out_specs,
        out_shape=out_shape,
        compiler_params=_params(("parallel", "parallel")),
        name="in_projection",
    )(x, w_cat)


def _fox_cum_kernel(gate_ref, bias_ref, o_ref, carry_ref):
    @pl.when(pl.program_id(1) == 0)
    def _():
        carry_ref[...] = jnp.zeros_like(carry_ref)

    tb = gate_ref.shape[1]
    logit = gate_ref[0] + bias_ref[...]
    log_f = -_softplus(-logit)
    rows = lax.broadcasted_iota(jnp.int32, (tb, tb), 0)
    cols = lax.broadcasted_iota(jnp.int32, (tb, tb), 1)
    tri = (cols <= rows).astype(F32)
    cum = _dot_f32(tri, log_f) + carry_ref[...]
    carry_ref[...] = cum[tb - 1:tb, :]
    o_ref[0, 0] = cum.T


def _fox_cumsum(gates, bias_row):
    bsz, t, _ = gates.shape
    tb = min(FOX_BLOCK, t)
    return pl.pallas_call(
        _fox_cum_kernel,
        grid=(bsz, t // tb),
        in_specs=[pl.BlockSpec((1, tb, GATE_LANES), lambda i, j: (i, j, 0)), _resident((1, GATE_LANES))],
        out_specs=pl.BlockSpec((1, 1, GATE_LANES, tb), lambda i, j: (i, j, 0, 0)),
        out_shape=jax.ShapeDtypeStruct((bsz, t // tb, GATE_LANES, tb), F32),
        scratch_shapes=[pltpu.VMEM((1, GATE_LANES), F32)],
        compiler_params=_params(("parallel", "arbitrary")),
        name="fox_cumsum",
    )(gates, bias_row)


def _causal_masks(blk):
    rows = lax.broadcasted_iota(jnp.int32, (blk, blk), 0)
    cols = lax.broadcasted_iota(jnp.int32, (blk, blk), 1)
    return cols <= rows, cols < rows


def _stack_heads(x2):
    lane = lax.broadcasted_iota(jnp.int32, x2.shape, 1)
    zero = jnp.zeros_like(x2)
    return jnp.concatenate([jnp.where(lane < HEAD_DIM, x2, zero), jnp.where(lane >= HEAD_DIM, x2, zero)], axis=0)


def _unstack_heads(y, rows):
    lane = lax.broadcasted_iota(jnp.int32, (rows, PAIR_WIDTH), 1)
    return jnp.where(lane < HEAD_DIM, y[:rows], y[rows:])


def _skewed(n_tiles, stages):
    for t in range(n_tiles + len(stages) - 1):
        for k in reversed(range(len(stages))):
            if 0 <= t - k < n_tiles:
                stages[k](t - k)


def _fox_kernel(q_ref, k_ref, v_ref, ck_ref, o_ref, kmax_ref, *, sub):
    blk = q_ref.shape[1]
    pair = pl.program_id(1)
    qi = pl.program_id(2)

    @pl.when(qi == 0)
    def _():
        def widest(c, best):
            kk = k_ref[0, pl.ds(pl.multiple_of(c * blk, blk), blk), :].astype(F32)
            return jnp.maximum(best, jnp.sum(kk * kk, axis=-1, keepdims=True))
        best = lax.fori_loop(0, k_ref.shape[1] // blk, widest, jnp.zeros((blk, 1), F32))
        kmax_ref[...] = jnp.sqrt(jnp.max(best, axis=0, keepdims=True))

    qs = _stack_heads(q_ref[0])
    n_sub = 2 * blk // sub
    tile = lambda i: slice(i * sub, (i + 1) * sub)
    qf = qs.astype(F32)
    qk_bound = jnp.sqrt(jnp.sum(qf * qf, axis=-1, keepdims=True)) * kmax_ref[...]
    rows = lax.broadcasted_iota(jnp.int32, (sub, blk), 0)
    cols = lax.broadcasted_iota(jnp.int32, (sub, blk), 1)
    ones = jnp.ones((blk, PAIR_WIDTH), BF16)

    def step(j, carry, masked):
        ms, accs = carry
        start = pl.multiple_of(j * blk, blk)
        kb = k_ref[0, pl.ds(start, blk), :]
        vb = jnp.concatenate([v_ref[0, pl.ds(start, blk), :], ones], axis=1)
        ms, accs = list(ms), list(accs)
        s, p, alpha = [None] * n_sub, [None] * n_sub, [None] * n_sub

        def logits(i):
            head, off = divmod(i * sub, blk)
            si = _dot_nt(qs[tile(i)], kb) - ck_ref[0, j, pl.ds(2 * pair + head, 1), :]
            s[i] = jnp.where(cols <= rows + off, si, NEG_BIG) if masked else si

        def probs(i):
            m_new = jnp.maximum(ms[i], jnp.max(s[i], axis=-1, keepdims=True))
            p[i] = jnp.exp(s[i] - m_new).astype(BF16)
            alpha[i] = jnp.exp(ms[i] - m_new)
            ms[i] = m_new

        def values(i):
            accs[i] = alpha[i] * accs[i] + _dot(p[i], vb)

        _skewed(n_sub, [logits, probs, values])
        return tuple(ms), tuple(accs)

    def headroom(j, ms):
        room = None
        for i in range(n_sub):
            last = -ck_ref[0, j, pl.ds(2 * pair + (i * sub) // blk, 1), :][:, blk - 1:blk]
            r = jnp.max(qk_bound[tile(i)] - ms[i] + last)
            room = r if room is None else jnp.maximum(room, r)
        return room

    init = (tuple(jnp.full((sub, 1), NEG_BIG, F32) for _ in range(n_sub)),
            tuple(jnp.zeros((sub, 2 * PAIR_WIDTH), F32) for _ in range(n_sub)))
    carry = step(qi, init, True)

    def more(state):
        i, room, _ = state
        return (i < qi) & (room > FOX_LOG_UNDERFLOW)

    def sweep(state):
        i, _, c = state
        c = step(qi - 1 - i, c, False)
        return i + 1, headroom(jnp.maximum(qi - 2 - i, 0), c[0]), c

    _, _, (_, accs) = lax.while_loop(more, sweep, (jnp.int32(0), headroom(jnp.maximum(qi - 1, 0), carry[0]), carry))
    acc = jnp.concatenate(accs, axis=0)
    o_ref[0] = _unstack_heads(acc[:, :PAIR_WIDTH] / acc[:, PAIR_WIDTH:PAIR_WIDTH + 1], blk)


def _fox_attention(q, k, v, cum_t):
    bsz, t, width = q.shape
    blk = min(FOX_BLOCK, t)
    nk = t // blk
    whole = pl.BlockSpec((1, t, PAIR_WIDTH), lambda b, p, i: (b, 0, p))
    qblk = pl.BlockSpec((1, blk, PAIR_WIDTH), lambda b, p, i: (b, i, p))
    return pl.pallas_call(
        functools.partial(_fox_kernel, sub=min(ATTN_SUBTILE, blk)),
        grid=(bsz, width // PAIR_WIDTH, nk),
        in_specs=[qblk, whole, whole,
                  pl.BlockSpec((1, nk, 8, blk), lambda b, p, i: (b, 0, GATE_F0 // 8, 0))],
        out_specs=qblk,
        out_shape=jax.ShapeDtypeStruct((bsz, t, width), F32),
        scratch_shapes=[pltpu.VMEM((1, 1), F32)],
        compiler_params=_params(("arbitrary", "arbitrary", "arbitrary")),
        name="fox_attention",
    )(q, k, v, cum_t)


def _sb_kernel(q_ref, k_ref, v_ref, o_ref, *, tk, sub):
    tq = q_ref.shape[1]
    per = tq // tk
    qi = pl.program_id(2)
    qs = _stack_heads(q_ref[0])
    n_sub = 2 * tq // sub
    rows = lax.broadcasted_iota(jnp.int32, (sub, tk), 0)
    cols = lax.broadcasted_iota(jnp.int32, (sub, tk), 1)
    suffix = _causal_masks(tk)[0].astype(BF16)

    def step(j, carry, diag):
        rests, accs = carry
        start = pl.multiple_of(j * tk, tk)
        kb = k_ref[0, pl.ds(start, tk), :]
        vb = v_ref[0, pl.ds(start, tk), :]
        rests, accs = list(rests), list(accs)
        z, split, w, strict = [None] * n_sub, [None] * n_sub, [None] * n_sub, [None] * n_sub

        def logits(i):
            z[i] = _dot_nt(qs[i * sub:(i + 1) * sub], kb)
            if diag is not None:
                strict[i] = cols + diag < rows + (i * sub) % tq

        def keep(i):
            log_keep = -_softplus(z[i])
            if diag is not None:
                log_keep = jnp.where(strict[i], log_keep, 0.0)
            split[i] = jnp.concatenate(_split_bf16(log_keep), axis=0)

        def weights(i):
            tails = _dot(split[i], suffix)
            tail = tails[:sub] + tails[sub:]
            wi = jnp.exp(z[i] + tail + rests[i])
            if diag is not None:
                wi = jnp.where(strict[i], wi, 0.0)
            w[i] = wi.astype(BF16)
            rests[i] = rests[i] + tail[:, 0:1]

        def values(i):
            accs[i] = accs[i] + _dot(w[i], vb)

        _skewed(n_sub, [logits, keep, weights, values])
        return tuple(rests), tuple(accs)

    carry = (tuple(jnp.zeros((sub, 1), F32) for _ in range(n_sub)),
             tuple(jnp.zeros((sub, PAIR_WIDTH), F32) for _ in range(n_sub)))
    for d in reversed(range(per)):
        carry = step(qi * per + d, carry, d * tk)

    def largest(rests):
        return functools.reduce(jnp.maximum, [jnp.max(r) for r in rests])

    def more(state):
        i, top, _ = state
        return (i < qi * per) & (top > SB_LOG_UNDERFLOW)

    def sweep(state):
        i, _, c = state
        c = step(qi * per - 1 - i, c, None)
        return i + 1, largest(c[0]), c

    _, _, (_, accs) = lax.while_loop(more, sweep, (jnp.int32(0), largest(carry[0]), carry))
    o_ref[0] = _unstack_heads(jnp.concatenate(accs, axis=0), tq)


def _sb_attention(q, k, v):
    bsz, t, width = q.shape
    tq = min(SB_Q_BLOCK, t)
    tk = min(SB_K_BLOCK, tq)
    whole = pl.BlockSpec((1, t, PAIR_WIDTH), lambda b, p, i: (b, 0, p))
    qblk = pl.BlockSpec((1, tq, PAIR_WIDTH), lambda b, p, i: (b, i, p))
    return pl.pallas_call(
        functools.partial(_sb_kernel, tk=tk, sub=min(ATTN_SUBTILE, tq)),
        grid=(bsz, width // PAIR_WIDTH, t // tq),
        in_specs=[qblk, whole, whole],
        out_specs=qblk,
        out_shape=jax.ShapeDtypeStruct((bsz, t, width), F32),
        compiler_params=_params(("parallel", "parallel", "arbitrary")),
        name="sb_attention",
    )(q, k, v)


CONV_HALO = 32


def _conv_kernel(prev_ref, cur_ref, w_ref, b_ref, ng_ref, nb_ref, o_ref, xs_ref):
    tb = cur_ref.shape[1]
    first = pl.program_id(1) == 0
    xs_ref[0:CONV_HALO, :] = jnp.where(first, 0.0, prev_ref[0])
    xs_ref[CONV_HALO:CONV_HALO + tb, :] = cur_ref[0]
    base = CONV_HALO - (CONF_KERNEL - 1)
    acc = jnp.zeros((tb, GROUP_WIDTH), F32) + b_ref[...]
    for tap in range(CONF_KERNEL):
        acc = acc + w_ref[tap:tap + 1, :] * xs_ref[base + tap:base + tap + tb, :]
    gsz = GROUP_WIDTH // CONF_GROUPS
    parts = []
    for g in range(CONF_GROUPS):
        cg = acc[:, g * gsz:(g + 1) * gsz]
        mu = jnp.mean(cg, axis=-1, keepdims=True)
        d = cg - mu
        var = jnp.mean(d * d, axis=-1, keepdims=True)
        parts.append(d * lax.rsqrt(var + LN_EPS))
    hn = jnp.concatenate(parts, axis=-1) * ng_ref[...] + nb_ref[...]
    o_ref[0] = _silu(hn)


def _conv_module(glu, w, b, ng, nb):
    bsz, t, c = glu.shape
    tb = min(CONV_BLOCK, t)
    per = tb // CONV_HALO
    vec = lambda a: a.reshape(1, c)
    return pl.pallas_call(
        _conv_kernel,
        grid=(bsz, t // tb),
        in_specs=[pl.BlockSpec((1, CONV_HALO, c), lambda i, j: (i, jnp.maximum(j * per - 1, 0), 0)),
                  pl.BlockSpec((1, tb, c), lambda i, j: (i, j, 0)),
                  _resident(w.shape), _resident((1, c)), _resident((1, c)), _resident((1, c))],
        out_specs=pl.BlockSpec((1, tb, c), lambda i, j: (i, j, 0)),
        out_shape=jax.ShapeDtypeStruct(glu.shape, F32),
        scratch_shapes=[pltpu.VMEM((CONV_HALO + tb, c), F32)],
        compiler_params=_params(("parallel", "parallel")),
        name="conv_module",
    )(glu, glu, w, vec(b), vec(ng), vec(nb))


GDN_HALO = 8


def _split_bf16(x):
    hi = x.astype(BF16)
    return hi, (x - hi.astype(F32)).astype(BF16)


def _group_sum(x, ones_bd):
    rows = x.shape[0]
    hi, lo = _split_bf16(x)
    r = _dot(jnp.concatenate([hi, lo], axis=0), ones_bd)
    return r[:rows] + r[rows:]


def _unit_lower_inverses(lows, rows, cols):
    n = lows[0].shape[0]
    eye = jnp.where(rows == cols, 1.0, 0.0)
    first = (rows % 2 == 1) & (cols == rows - 1)
    invs = [eye - jnp.where(first, low, 0.0) for low in lows]
    size = 2
    while size < GDN_CHUNK:
        rb = rows // size
        level = (rb % 2 == 1) & (cols // size == rb - 1)
        splits = [_split_bf16(inv) for inv in invs]
        xs = []
        for low, (d_hi, d_lo) in zip(lows, splits):
            x2 = _dot(jnp.where(level, low, 0.0).astype(BF16), jnp.concatenate([d_hi, d_lo], axis=1))
            xs.append(_split_bf16(x2[:, :n] + x2[:, n:]))
        nxt = []
        for inv, (d_hi, d_lo), (x_hi, x_lo) in zip(invs, splits, xs):
            y4 = _dot(jnp.concatenate([d_hi, d_lo], axis=0), jnp.concatenate([x_hi, x_lo], axis=1))
            nxt.append(inv - (y4[:n, :n] + y4[:n, n:] + y4[n:, :n] + y4[n:, n:]))
        invs = nxt
        size *= 2
    return invs


def _gdn_kernel(qp_ref, kp_ref, vp_ref, q_ref, k_ref, v_ref, z_ref, gate_ref, cw_ref, alog_ref, dtb_ref,
                ng_ref, o_ref, xs_ref, y_ref, state_ref):
    tb = q_ref.shape[1]
    c = GDN_CHUNK
    n = 2 * c
    gw = GROUP_WIDTH
    first = pl.program_id(1) == 0

    @pl.when(first)
    def _():
        state_ref[...] = jnp.zeros_like(state_ref)

    r2 = lax.broadcasted_iota(jnp.int32, (gw, gw), 0)
    c2 = lax.broadcasted_iota(jnp.int32, (gw, gw), 1)
    ones_bd = (r2 // HEAD_DIM == c2 // HEAD_DIM).astype(BF16)

    base = GDN_HALO - (SHORT_CONV - 1)

    def conv_silu(idx, p_ref, c_ref):
        xs_ref[0:GDN_HALO, :] = jnp.where(first, 0.0, p_ref[0])
        xs_ref[GDN_HALO:GDN_HALO + tb, :] = c_ref[0]
        acc = None
        for tap in range(SHORT_CONV):
            term = cw_ref[tap:tap + 1, idx * gw:(idx + 1) * gw] * xs_ref[base + tap:base + tap + tb, :]
            acc = term if acc is None else acc + term
        return _silu(acc)

    def l2_normalize(y):
        return y * lax.rsqrt(_group_sum(y * y, ones_bd) + L2_EPS)

    qn = l2_normalize(conv_silu(0, qp_ref, q_ref)) * (HEAD_DIM ** -0.5)
    kn = l2_normalize(conv_silu(1, kp_ref, k_ref))
    vn = conv_silu(2, vp_ref, v_ref)

    gates = gate_ref[0]
    log_decay = -jnp.exp(alog_ref[...]) * _softplus(gates + dtb_ref[...])
    beta_all = _sigmoid(gates)
    rows_t = lax.broadcasted_iota(jnp.int32, (tb, tb), 0)
    cols_t = lax.broadcasted_iota(jnp.int32, (tb, tb), 1)
    chunk_tri = ((cols_t <= rows_t) & (cols_t // c == rows_t // c)).astype(F32)
    gcum = _dot_f32(chunk_tri, log_decay)
    lane_bcast = lambda x, lane: jnp.broadcast_to(x[:, lane:lane + 1], (tb, PAIR_WIDTH))
    g_wide = [lane_bcast(gcum, GATE_A0 + h) for h in range(GROUP_HEADS)]
    b_wide = [lane_bcast(beta_all, GATE_B0 + h) for h in range(GROUP_HEADS)]
    low_half = lax.broadcasted_iota(jnp.int32, (tb, PAIR_WIDTH), 1) < HEAD_DIM

    rows = lax.broadcasted_iota(jnp.int32, (n, n), 0)
    cols = lax.broadcasted_iota(jnp.int32, (n, n), 1)
    same_head = rows // c == cols // c
    lower_incl = same_head & (cols <= rows)
    strict_lower = same_head & (cols < rows)

    pairs = []
    for p in range(GROUP_HEADS // 2):
        ls = slice(p * PAIR_WIDTH, (p + 1) * PAIR_WIDTH)
        g_nat = jnp.where(low_half, g_wide[2 * p], g_wide[2 * p + 1])
        b_nat = jnp.where(low_half, b_wide[2 * p], b_wide[2 * p + 1])
        eg = jnp.exp(g_nat)
        k_beta = kn[:, ls] * b_nat
        pairs.append(dict(ls=ls, g_nat=g_nat, q=qn[:, ls], k=kn[:, ls], q_dec=qn[:, ls] * eg, k_beta=k_beta,
                          v_beta=vn[:, ls] * b_nat, kb_eg=k_beta * eg))

    n_chunks = tb // c
    systems = [(ci, p) for ci in range(n_chunks) for p in range(len(pairs))]
    chunk_rows = lambda ci: slice(ci * c, (ci + 1) * c)

    lkks, a_qks = [], []
    for ci, p in systems:
        d, r = pairs[p], chunk_rows(ci)
        g_col = jnp.concatenate([g_wide[2 * p][r], g_wide[2 * p + 1][r]], axis=0)
        decay = jnp.exp(jnp.where(lower_incl, g_col - g_col.T, -jnp.inf))
        k_st = _stack_heads(d["k"][r]).astype(BF16)
        lhs = jnp.concatenate([_stack_heads(d["k_beta"][r]), _stack_heads(d["q"][r])], axis=0).astype(BF16)
        gram = _dot_nt(lhs, k_st)
        lkks.append(jnp.where(strict_lower, gram[:n] * decay, 0.0))
        a_qks.append((gram[n:] * decay).astype(BF16))
    t_invs = _unit_lower_inverses(lkks, rows, cols)
    uws = []
    for (ci, p), t_inv in zip(systems, t_invs):
        d, r = pairs[p], chunk_rows(ci)
        rhs = jnp.concatenate([_stack_heads(d["v_beta"][r]), _stack_heads(d["kb_eg"][r])], axis=1)
        uws.append(_dot(t_inv.astype(BF16), rhs.astype(BF16)))

    states = [state_ref[p] for p in range(len(pairs))]
    for idx, (ci, p) in enumerate(systems):
        d, r = pairs[p], chunk_rows(ci)
        uw, state = uws[idx], states[p]
        g_last = d["g_nat"][ci * c + c - 1:ci * c + c, :]
        k_dec = _stack_heads(d["k"][r] * jnp.exp(g_last - d["g_nat"][r])).astype(BF16)
        wq = _dot(jnp.concatenate([uw[:, n:].astype(BF16), _stack_heads(d["q_dec"][r]).astype(BF16)], axis=0),
                  state.astype(BF16))
        vnb = (uw[:, :n] - wq[:n]).astype(BF16)
        o_st = wq[n:] + _dot(a_qks[idx], vnb)
        states[p] = state * jnp.exp(g_last) + _dot_tn(k_dec, vnb)
        y_ref[r, d["ls"]] = o_st[:c] + o_st[c:]
    for p in range(len(pairs)):
        state_ref[p] = states[p]

    o = y_ref[...]
    mean_sq = _group_sum(o * o, ones_bd) * (1.0 / HEAD_DIM)
    o_ref[0] = o * lax.rsqrt(mean_sq + RMS_EPS) * ng_ref[...] * _silu(z_ref[0])


def _gdn_mixer(q, k, v, z, gates, conv_w, a_log, dt_bias, norm_g):
    bsz, t, width = q.shape
    nh = GROUP_HEADS
    tb = min(GDN_BLOCK, t)
    per = tb // GDN_HALO
    cur = pl.BlockSpec((1, tb, width), lambda i, j: (i, j, 0))
    prev = pl.BlockSpec((1, GDN_HALO, width), lambda i, j: (i, jnp.maximum(j * per - 1, 0), 0))
    lane_row = lambda vals, off: jnp.zeros((1, GATE_LANES), F32).at[0, off:off + nh].set(vals)
    return pl.pallas_call(
        _gdn_kernel,
        grid=(bsz, t // tb),
        in_specs=[prev, prev, prev, cur, cur, cur, cur,
                  pl.BlockSpec((1, tb, GATE_LANES), lambda i, j: (i, j, 0)),
                  _resident(conv_w.shape), _resident((1, GATE_LANES)), _resident((1, GATE_LANES)),
                  _resident((1, width))],
        out_specs=cur,
        out_shape=jax.ShapeDtypeStruct(q.shape, F32),
        scratch_shapes=[pltpu.VMEM((GDN_HALO + tb, width), F32), pltpu.VMEM((tb, width), F32),
                        pltpu.VMEM((nh // 2, PAIR_WIDTH, PAIR_WIDTH), F32)],
        compiler_params=_params(("parallel", "arbitrary")),
        name="gdn_mixer",
    )(q, k, v, q, k, v, z, gates, conv_w, lane_row(a_log, GATE_A0), lane_row(dt_bias, GATE_A0),
      jnp.tile(norm_g, nh).reshape(1, width))


def _outproj_kernel(x_ref, ya_ref, yb_ref, yc_ref, yd_ref, w_ref, g_ref, b_ref, o_ref):
    gw = GROUP_WIDTH
    mix = None
    for group, y_ref in enumerate((ya_ref, yb_ref, yc_ref, yd_ref)):
        part = _dot(y_ref[0].astype(BF16), w_ref[group * gw:(group + 1) * gw, :])
        mix = part if mix is None else mix + part
    o_ref[0] = _layer_norm(DN_ALPHA * x_ref[0] + mix, g_ref[...], b_ref[...])


def _out_projection_ln(x, ya, yb, yc, yd, w_out, g, b):
    bsz, t, d = x.shape
    tm = min(TOKEN_TILE, t)
    row = lambda c: pl.BlockSpec((1, tm, c), lambda i, j: (i, j, 0))
    return pl.pallas_call(
        _outproj_kernel,
        grid=(bsz, t // tm),
        in_specs=[row(d)] + [row(GROUP_WIDTH)] * 4 + [_resident(w_out.shape),
                  _resident((1, d)), _resident((1, d))],
        out_specs=row(d),
        out_shape=jax.ShapeDtypeStruct(x.shape, F32),
        compiler_params=_params(("parallel", "parallel")),
        name="out_projection_ln",
    )(x, ya, yb, yc, yd, w_out, g.reshape(1, d), b.reshape(1, d))


def _memkv_kernel(m_ref, w_ref, o_ref):
    o_ref[0] = _dot(m_ref[0].astype(BF16), w_ref[...]).astype(o_ref.dtype)


def _memory_kv(mem, w_kv):
    bsz, m, d = mem.shape
    return pl.pallas_call(
        _memkv_kernel,
        grid=(bsz,),
        in_specs=[pl.BlockSpec((1, m, d), lambda i: (i, 0, 0)), _resident(w_kv.shape)],
        out_specs=pl.BlockSpec((1, m, 2 * d), lambda i: (i, 0, 0)),
        out_shape=jax.ShapeDtypeStruct((bsz, m, 2 * d), BF16),
        compiler_params=_params(("parallel",)),
        name="memory_kv",
    )(mem, w_kv)


def _memattn_kernel(x_ref, kv_ref, wq_ref, wo_ref, g_ref, b_ref, o_ref):
    x = x_ref[0]
    q = _dot(x.astype(BF16), wq_ref[...])
    hd = MEM_HEAD_DIM
    outs = []
    for h in range(MEM_HEADS):
        qh = (q[:, h * hd:(h + 1) * hd] * (hd ** -0.5)).astype(BF16)
        kh = kv_ref[0, :, h * hd:(h + 1) * hd]
        vh = kv_ref[0, :, D_MODEL + h * hd:D_MODEL + (h + 1) * hd]
        s = _dot_nt(qh, kh)
        p = jnp.exp(s - jnp.max(s, axis=-1, keepdims=True))
        p = p / jnp.sum(p, axis=-1, keepdims=True)
        outs.append(_dot(p.astype(BF16), vh).astype(BF16))
    y = _dot(jnp.concatenate(outs, axis=-1), wo_ref[...])
    o_ref[0] = _layer_norm(DN_ALPHA * x + y, g_ref[...], b_ref[...])


def _memory_attention_ln(x, kv, wq, wo, g, b):
    bsz, t, d = x.shape
    tm = min(TOKEN_TILE, t)
    row = pl.BlockSpec((1, tm, d), lambda i, j: (i, j, 0))
    return pl.pallas_call(
        _memattn_kernel,
        grid=(bsz, t // tm),
        in_specs=[row, pl.BlockSpec((1, kv.shape[1], 2 * d), lambda i, j: (i, 0, 0)),
                  _resident(wq.shape), _resident(wo.shape), _resident((1, d)), _resident((1, d))],
        out_specs=row,
        out_shape=jax.ShapeDtypeStruct(x.shape, F32),
        compiler_params=_params(("parallel", "parallel")),
        name="memory_attention_ln",
    )(x, kv, wq, wo, g.reshape(1, d), b.reshape(1, d))


def _combined_in_weight(w_in):
    gw, nh = GROUP_WIDTH, GROUP_HEADS
    o = 0
    gdn_qkv = w_in[:, o:o + 3 * gw]; o += 3 * gw
    gdn_z = w_in[:, o:o + gw]; o += gw
    gdn_a = w_in[:, o:o + nh]; o += nh
    gdn_b = w_in[:, o:o + nh]; o += nh
    fox_qkv = w_in[:, o:o + 3 * gw]; o += 3 * gw
    fox_f = w_in[:, o:o + nh]; o += nh
    conf = w_in[:, o:o + 2 * gw]; o += 2 * gw
    sb_qkv = w_in[:, o:o + 3 * gw]
    pad = jnp.zeros((w_in.shape[0], GATE_LANES - 3 * nh), w_in.dtype)
    return jnp.concatenate([gdn_qkv, gdn_z, fox_qkv, conf, sb_qkv, gdn_a, gdn_b, fox_f, pad], axis=1).astype(BF16)


def kernel(x, mem, ffn1_w_gate, ffn1_w_up, ffn1_w_down, ln_ffn1_g, ln_ffn1_b, w_in, gdn_conv_w, gdn_a_log, gdn_dt_bias, gdn_norm_g, fox_b_f, conf_dw_w, conf_dw_b, conf_norm_g, conf_norm_b, w_out, ln_mix_g, ln_mix_b, mem_w_q, mem_w_kv, mem_w_o, ln_mem_g, ln_mem_b, ffn2_w_gate, ffn2_w_up, ffn2_w_down, ln_ffn2_g, ln_ffn2_b):
    bf = lambda a: a.astype(BF16)
    for i in range(DEPTH):
        x = _ffn_ln(x, bf(ffn1_w_gate[i]), bf(ffn1_w_up[i]), bf(ffn1_w_down[i]), ln_ffn1_g[i], ln_ffn1_b[i])

        (gq, gk, gv, gz, fq, fk, fv, glu, sq, sk, sv, gates) = _in_projection(x, _combined_in_weight(w_in[i]))
        y_a = _gdn_mixer(gq, gk, gv, gz, gates, gdn_conv_w[i], gdn_a_log[i], gdn_dt_bias[i], gdn_norm_g[i])
        f_bias = jnp.zeros((1, GATE_LANES), F32).at[0, GATE_F0:GATE_F0 + GROUP_HEADS].set(fox_b_f[i])
        y_b = _fox_attention(fq, fk, fv, _fox_cumsum(gates, f_bias))
        y_c = _conv_module(glu, conf_dw_w[i], conf_dw_b[i], conf_norm_g[i], conf_norm_b[i])
        y_d = _sb_attention(sq, sk, sv)
        x = _out_projection_ln(x, y_a, y_b, y_c, y_d, bf(w_out[i]), ln_mix_g[i], ln_mix_b[i])

        kv = _memory_kv(mem, bf(mem_w_kv[i]))
        x = _memory_attention_ln(x, kv, bf(mem_w_q[i]), bf(mem_w_o[i]), ln_mem_g[i], ln_mem_b[i])

        x = _ffn_ln(x, bf(ffn2_w_gate[i]), bf(ffn2_w_up[i]), bf(ffn2_w_down[i]), ln_ffn2_g[i], ln_ffn2_b[i])
    return x
```

```python
import functools
import math

import jax
import jax.numpy as jnp
from jax import lax
from jax.experimental import pallas as pl
from jax.experimental.pallas import tpu as pltpu

F32 = jnp.float32
BF16 = jnp.bfloat16

D_MODEL = 1024
DEPTH = 2
GROUP_WIDTH = 256
HEAD_DIM = 64
GROUP_HEADS = 4
D_FF = 2816
SHORT_CONV = 4
CONF_KERNEL = 31
CONF_GROUPS = 4
GDN_CHUNK = 64
N_MEM = 256
MEM_HEADS = 4
MEM_HEAD_DIM = D_MODEL // MEM_HEADS
DN_ALPHA = float((2 * DEPTH) ** 0.25)
LN_EPS = 1e-5
RMS_EPS = 1e-6
L2_EPS = 1e-6
NEG_BIG = -1e30

GATE_LANES = 128
GATE_A0, GATE_B0, GATE_F0 = 0, 4, 8

TOKEN_TILE = 512
MXU_WIDTH = 256
SUBLANES = 8
FFN_SPLITS = (0, 6 * MXU_WIDTH, D_FF)
PAIR_WIDTH = 2 * HEAD_DIM
FOX_BLOCK = 512
SB_Q_BLOCK = 512
SB_K_BLOCK = 256
ATTN_SUBTILE = 256
SB_LOG_UNDERFLOW = -105.0
FOX_LOG_UNDERFLOW = -106.0
GDN_BLOCK = 256
CONV_BLOCK = 512
VMEM_LIMIT = 56 * 1024 * 1024


def _params(sem, vmem=VMEM_LIMIT):
    return pltpu.CompilerParams(dimension_semantics=sem, vmem_limit_bytes=vmem)


def _resident(shape):
    nd = len(shape)
    return pl.BlockSpec(shape, lambda *_: (0,) * nd, pipeline_mode=pl.Buffered(1))


def _layer_slab(stacked, layer):
    tail = stacked.shape[1:]
    return pl.BlockSpec((None,) + tail, lambda *_: (layer,) + (0,) * len(tail), pipeline_mode=pl.Buffered(1))


def _layer_norm(y, g, b):
    mu = jnp.mean(y, axis=-1, keepdims=True)
    d = y - mu
    var = jnp.mean(d * d, axis=-1, keepdims=True)
    return d * lax.rsqrt(var + LN_EPS) * g + b


def _sigmoid(x):
    return 1.0 / (1.0 + jnp.exp(-x))


def _silu(x):
    return x * _sigmoid(x)


def _softplus(x):
    return jnp.maximum(x, 0.0) + jnp.log(1.0 + jnp.exp(-jnp.abs(x)))


def _dot(a, b):
    return jnp.dot(a, b, preferred_element_type=F32)


def _dot_nt(a, b):
    return lax.dot_general(a, b, (((1,), (1,)), ((), ())), preferred_element_type=F32)


def _dot_tn(a, b):
    return lax.dot_general(a, b, (((0,), (0,)), ((), ())), preferred_element_type=F32)


def _dot_f32(a, b):
    return jnp.dot(a, b, preferred_element_type=F32, precision=lax.Precision.HIGHEST)


def _ffn_kernel(x_ref, wg_ref, wu_ref, wd_ref, g_ref, b_ref, o_ref):
    x = x_ref[0]
    xb = x.astype(BF16)
    acc = None
    for lo, hi in zip(FFN_SPLITS[:-1], FFN_SPLITS[1:]):
        h = _dot(xb, wg_ref[:, lo:hi])
        u = _dot(xb, wu_ref[:, lo:hi])
        a = (_silu(h) * u).astype(BF16)
        part = _dot(a, wd_ref[lo:hi, :])
        acc = part if acc is None else acc + part
    o_ref[0] = _layer_norm(DN_ALPHA * x + 0.5 * acc, g_ref[...], b_ref[...])


def _ffn_ln(x, layer, wg, wu, wd, g, b):
    bsz, t, d = x.shape
    tm = min(TOKEN_TILE, t)
    row = pl.BlockSpec((1, tm, d), lambda i, j: (i, j, 0))
    return pl.pallas_call(
        _ffn_kernel,
        grid=(bsz, t // tm),
        in_specs=[row] + [_layer_slab(a, layer) for a in (wg, wu, wd, g, b)],
        out_specs=row,
        out_shape=jax.ShapeDtypeStruct(x.shape, F32),
        compiler_params=_params(("parallel", "parallel")),
        name="ffn_ln",
    )(x, wg, wu, wd, g, b)


def _inproj_kernel(x_ref, w_ref, gq_ref, gk_ref, gv_ref, gz_ref, fq_ref, fk_ref, fv_ref,
                   glu_ref, sq_ref, sk_ref, sv_ref, gate_ref):
    xb = x_ref[0].astype(BF16)
    gw = GROUP_WIDTH

    def group(i, width=gw):
        return _dot(xb, w_ref[:, i * gw:i * gw + width])

    qk_scale = HEAD_DIM ** -0.5
    gq_ref[0] = group(0)
    gk_ref[0] = group(1)
    gv_ref[0] = group(2)
    gz_ref[0] = group(3)
    fq_ref[0] = (group(4) * qk_scale).astype(BF16)
    fk_ref[0] = group(5).astype(BF16)
    fv_ref[0] = group(6).astype(BF16)
    glu_ref[0] = group(7) * _sigmoid(group(8))
    sq_ref[0] = (group(9) * qk_scale).astype(BF16)
    sk_ref[0] = group(10).astype(BF16)
    sv_ref[0] = group(11).astype(BF16)
    gate_ref[0] = group(12, GATE_LANES)


def _in_projection(x, layer, w_cat):
    bsz, t, d = x.shape
    tm = min(TOKEN_TILE, t)
    row = lambda c: pl.BlockSpec((1, tm, c), lambda i, j: (i, j, 0))
    wide = lambda dt: jax.ShapeDtypeStruct((bsz, t, GROUP_WIDTH), dt)
    out_shape = ([wide(F32)] * 4 + [wide(BF16)] * 3 + [wide(F32)] + [wide(BF16)] * 3
                 + [jax.ShapeDtypeStruct((bsz, t, GATE_LANES), F32)])
    out_specs = [row(GROUP_WIDTH)] * 11 + [row(GATE_LANES)]
    return pl.pallas_call(
        _inproj_kernel,
        grid=(bsz, t // tm),
        in_specs=[row(d), _layer_slab(w_cat, layer)],
        out_specs=out_specs,
        out_shape=out_shape,
        compiler_params=_params(("parallel", "parallel")),
        name="in_projection",
    )(x, w_cat)


def _fox_cum_kernel(gate_ref, bias_ref, o_ref, carry_ref):
    @pl.when(pl.program_id(1) == 0)
    def _():
        carry_ref[...] = jnp.zeros_like(carry_ref)

    tb = gate_ref.shape[1]
    logit = gate_ref[0] + bias_ref[...]
    log_f = -_softplus(-logit)
    rows = lax.broadcasted_iota(jnp.int32, (tb, tb), 0)
    cols = lax.broadcasted_iota(jnp.int32, (tb, tb), 1)
    tri = (cols <= rows).astype(F32)
    cum = _dot_f32(tri, log_f) + carry_ref[...]
    carry_ref[...] = cum[tb - 1:tb, :]
    o_ref[0, 0] = cum.T


def _fox_cumsum(gates, layer, bias_rows):
    bsz, t, _ = gates.shape
    tb = min(FOX_BLOCK, t)
    return pl.pallas_call(
        _fox_cum_kernel,
        grid=(bsz, t // tb),
        in_specs=[pl.BlockSpec((1, tb, GATE_LANES), lambda i, j: (i, j, 0)), _layer_slab(bias_rows, layer)],
        out_specs=pl.BlockSpec((1, 1, GATE_LANES, tb), lambda i, j: (i, j, 0, 0)),
        out_shape=jax.ShapeDtypeStruct((bsz, t // tb, GATE_LANES, tb), F32),
        scratch_shapes=[pltpu.VMEM((1, GATE_LANES), F32)],
        compiler_params=_params(("parallel", "arbitrary")),
        name="fox_cumsum",
    )(gates, bias_rows)


def _causal_masks(blk):
    rows = lax.broadcasted_iota(jnp.int32, (blk, blk), 0)
    cols = lax.broadcasted_iota(jnp.int32, (blk, blk), 1)
    return cols <= rows, cols < rows


def _stack_heads(x2):
    lane = lax.broadcasted_iota(jnp.int32, x2.shape, 1)
    zero = jnp.zeros_like(x2)
    return jnp.concatenate([jnp.where(lane < HEAD_DIM, x2, zero), jnp.where(lane >= HEAD_DIM, x2, zero)], axis=0)


def _unstack_heads(y, rows):
    lane = lax.broadcasted_iota(jnp.int32, (rows, PAIR_WIDTH), 1)
    return jnp.where(lane < HEAD_DIM, y[:rows], y[rows:])


def _skewed(tiles, stages):
    tiles = list(tiles)
    for t in range(len(tiles) + len(stages) - 1):
        for k in reversed(range(len(stages))):
            if 0 <= t - k < len(tiles):
                stages[k](tiles[t - k])


def _fox_kernel(q_ref, k_ref, v_ref, ck_ref, o_ref, kmax_ref, *, sub):
    blk = q_ref.shape[1]
    pair = pl.program_id(1)
    qi = pl.program_id(2)

    @pl.when(qi == 0)
    def _():
        def widest(c, best):
            kk = k_ref[0, pl.ds(pl.multiple_of(c * blk, blk), blk), :].astype(F32)
            return jnp.maximum(best, jnp.sum(kk * kk, axis=-1, keepdims=True))
        best = lax.fori_loop(0, k_ref.shape[1] // blk, widest, jnp.zeros((blk, 1), F32))
        kmax_ref[...] = jnp.sqrt(jnp.max(best, axis=0, keepdims=True))

    qs = _stack_heads(q_ref[0])
    n_sub = 2 * blk // sub
    tile = lambda i: slice(i * sub, (i + 1) * sub)
    qf = qs.astype(F32)
    qk_bound = jnp.sqrt(jnp.sum(qf * qf, axis=-1, keepdims=True)) * kmax_ref[...]
    rows = lax.broadcasted_iota(jnp.int32, (sub, blk), 0)
    cols = lax.broadcasted_iota(jnp.int32, (sub, blk), 1)
    ones = jnp.ones((blk, PAIR_WIDTH), BF16)

    def step(j, carry, masked):
        ms, accs = carry
        start = pl.multiple_of(j * blk, blk)
        kb = k_ref[0, pl.ds(start, blk), :]
        vb = jnp.concatenate([v_ref[0, pl.ds(start, blk), :], ones], axis=1)
        ms, accs = list(ms), list(accs)
        s, p, alpha = [None] * n_sub, [None] * n_sub, [None] * n_sub

        def logits(i):
            head, off = divmod(i * sub, blk)
            si = _dot_nt(qs[tile(i)], kb) - ck_ref[0, j, pl.ds(2 * pair + head, 1), :]
            s[i] = jnp.where(cols <= rows + off, si, NEG_BIG) if masked else si

        def probs(i):
            m_new = jnp.maximum(ms[i], jnp.max(s[i], axis=-1, keepdims=True))
            p[i] = jnp.exp(s[i] - m_new).astype(BF16)
            alpha[i] = jnp.exp(ms[i] - m_new)
            ms[i] = m_new

        def values(i):
            accs[i] = alpha[i] * accs[i] + _dot(p[i], vb)

        _skewed(range(n_sub), [logits, probs, values])
        return tuple(ms), tuple(accs)

    def headroom(j, ms):
        room = None
        for i in range(n_sub):
            last = -ck_ref[0, j, pl.ds(2 * pair + (i * sub) // blk, 1), :][:, blk - 1:blk]
            r = jnp.max(qk_bound[tile(i)] - ms[i] + last)
            room = r if room is None else jnp.maximum(room, r)
        return room

    init = (tuple(jnp.full((sub, 1), NEG_BIG, F32) for _ in range(n_sub)),
            tuple(jnp.zeros((sub, 2 * PAIR_WIDTH), F32) for _ in range(n_sub)))
    carry = step(qi, init, True)

    def more(state):
        i, room, _ = state
        return (i < qi) & (room > FOX_LOG_UNDERFLOW)

    def sweep(state):
        i, _, c = state
        c = step(qi - 1 - i, c, False)
        return i + 1, headroom(jnp.maximum(qi - 2 - i, 0), c[0]), c

    _, _, (_, accs) = lax.while_loop(more, sweep, (jnp.int32(0), headroom(jnp.maximum(qi - 1, 0), carry[0]), carry))
    acc = jnp.concatenate(accs, axis=0)
    o_ref[0] = _unstack_heads(acc[:, :PAIR_WIDTH] / acc[:, PAIR_WIDTH:PAIR_WIDTH + 1], blk)


def _fox_attention(q, k, v, cum_t):
    bsz, t, width = q.shape
    blk = min(FOX_BLOCK, t)
    nk = t // blk
    whole = pl.BlockSpec((1, t, PAIR_WIDTH), lambda b, p, i: (b, 0, p))
    qblk = pl.BlockSpec((1, blk, PAIR_WIDTH), lambda b, p, i: (b, i, p))
    return pl.pallas_call(
        functools.partial(_fox_kernel, sub=min(ATTN_SUBTILE, blk)),
        grid=(bsz, width // PAIR_WIDTH, nk),
        in_specs=[qblk, whole, whole,
                  pl.BlockSpec((1, nk, 8, blk), lambda b, p, i: (b, 0, GATE_F0 // 8, 0))],
        out_specs=qblk,
        out_shape=jax.ShapeDtypeStruct((bsz, t, width), F32),
        scratch_shapes=[pltpu.VMEM((1, 1), F32)],
        compiler_params=_params(("arbitrary", "arbitrary", "arbitrary")),
        name="fox_attention",
    )(q, k, v, cum_t)


def _sb_kernel(q_ref, k_ref, v_ref, o_ref, *, tk, sub):
    tq = q_ref.shape[1]
    per = tq // tk
    qi = pl.program_id(2)
    qs = _stack_heads(q_ref[0])
    n_sub = 2 * tq // sub
    rows = lax.broadcasted_iota(jnp.int32, (sub, tk), 0)
    cols = lax.broadcasted_iota(jnp.int32, (sub, tk), 1)
    suffix = _causal_masks(tk)[0].astype(BF16)

    def step(j, carry, diag):
        rests, accs = carry
        start = pl.multiple_of(j * tk, tk)
        kb = k_ref[0, pl.ds(start, tk), :]
        vb = v_ref[0, pl.ds(start, tk), :]
        rests, accs = list(rests), list(accs)
        z, split, w, strict = [None] * n_sub, [None] * n_sub, [None] * n_sub, [None] * n_sub

        def logits(i):
            z[i] = _dot_nt(qs[i * sub:(i + 1) * sub], kb)
            if diag is not None:
                strict[i] = cols + diag < rows + (i * sub) % tq

        def keep(i):
            log_keep = -_softplus(z[i])
            if diag is not None:
                log_keep = jnp.where(strict[i], log_keep, 0.0)
            split[i] = jnp.concatenate(_split_bf16(log_keep), axis=0)

        def weights(i):
            tails = _dot(split[i], suffix)
            tail = tails[:sub] + tails[sub:]
            wi = jnp.exp(z[i] + tail + rests[i])
            if diag is not None:
                wi = jnp.where(strict[i], wi, 0.0)
            w[i] = wi.astype(BF16)
            rests[i] = rests[i] + tail[:, 0:1]

        def values(i):
            accs[i] = accs[i] + _dot(w[i], vb)

        live = [i for i in range(n_sub) if diag is None or diag < (i * sub) % tq + sub - 1]
        _skewed(live, [logits, keep, weights, values])
        return tuple(rests), tuple(accs)

    carry = (tuple(jnp.zeros((sub, 1), F32) for _ in range(n_sub)),
             tuple(jnp.zeros((sub, PAIR_WIDTH), F32) for _ in range(n_sub)))
    for d in reversed(range(per)):
        carry = step(qi * per + d, carry, d * tk)

    def largest(rests):
        return functools.reduce(jnp.maximum, [jnp.max(r) for r in rests])

    def more(state):
        i, top, _ = state
        return (i < qi * per) & (top > SB_LOG_UNDERFLOW)

    def sweep(state):
        i, _, c = state
        c = step(qi * per - 1 - i, c, None)
        return i + 1, largest(c[0]), c

    _, _, (_, accs) = lax.while_loop(more, sweep, (jnp.int32(0), largest(carry[0]), carry))
    o_ref[0] = _unstack_heads(jnp.concatenate(accs, axis=0), tq)


def _sb_attention(q, k, v):
    bsz, t, width = q.shape
    tq = min(SB_Q_BLOCK, t)
    tk = min(SB_K_BLOCK, tq)
    whole = pl.BlockSpec((1, t, PAIR_WIDTH), lambda b, p, i: (b, 0, p))
    qblk = pl.BlockSpec((1, tq, PAIR_WIDTH), lambda b, p, i: (b, i, p))
    return pl.pallas_call(
        functools.partial(_sb_kernel, tk=tk, sub=min(ATTN_SUBTILE, tq)),
        grid=(bsz, width // PAIR_WIDTH, t // tq),
        in_specs=[qblk, whole, whole],
        out_specs=qblk,
        out_shape=jax.ShapeDtypeStruct((bsz, t, width), F32),
        compiler_params=_params(("parallel", "parallel", "arbitrary")),
        name="sb_attention",
    )(q, k, v)


CONV_HALO = 32


def _conv_kernel(prev_ref, cur_ref, w_ref, b_ref, ng_ref, nb_ref, o_ref, xs_ref, sh_ref):
    tb = cur_ref.shape[1]
    first = pl.program_id(1) == 0
    xs_ref[0:CONV_HALO, :] = jnp.where(first, 0.0, prev_ref[0])
    xs_ref[CONV_HALO:CONV_HALO + tb, :] = cur_ref[0]
    span = tb + CONV_HALO - SUBLANES
    for r in range(1, SUBLANES):
        sh_ref[r, 0:span, :] = xs_ref[r:r + span, :]
    base = CONV_HALO - (CONF_KERNEL - 1)
    acc = jnp.zeros((tb, GROUP_WIDTH), F32) + b_ref[...]
    for tap in range(CONF_KERNEL):
        whole, r = divmod(base + tap, SUBLANES)
        src = xs_ref if r == 0 else sh_ref.at[r]
        acc = acc + w_ref[tap:tap + 1, :] * src[whole * SUBLANES:whole * SUBLANES + tb, :]
    gsz = GROUP_WIDTH // CONF_GROUPS
    parts = []
    for g in range(CONF_GROUPS):
        cg = acc[:, g * gsz:(g + 1) * gsz]
        mu = jnp.mean(cg, axis=-1, keepdims=True)
        d = cg - mu
        var = jnp.mean(d * d, axis=-1, keepdims=True)
        parts.append(d * lax.rsqrt(var + LN_EPS))
    hn = jnp.concatenate(parts, axis=-1) * ng_ref[...] + nb_ref[...]
    o_ref[0] = _silu(hn)


def _conv_module(glu, layer, w, b, ng, nb):
    bsz, t, c = glu.shape
    tb = min(CONV_BLOCK, t)
    per = tb // CONV_HALO
    return pl.pallas_call(
        _conv_kernel,
        grid=(bsz, t // tb),
        in_specs=[pl.BlockSpec((1, CONV_HALO, c), lambda i, j: (i, jnp.maximum(j * per - 1, 0), 0)),
                  pl.BlockSpec((1, tb, c), lambda i, j: (i, j, 0)),
                  ] + [_layer_slab(a, layer) for a in (w, b, ng, nb)],
        out_specs=pl.BlockSpec((1, tb, c), lambda i, j: (i, j, 0)),
        out_shape=jax.ShapeDtypeStruct(glu.shape, F32),
        scratch_shapes=[pltpu.VMEM((CONV_HALO + tb, c), F32), pltpu.VMEM((SUBLANES, CONV_HALO + tb, c), F32)],
        compiler_params=_params(("parallel", "parallel")),
        name="conv_module",
    )(glu, glu, w, b, ng, nb)


GDN_HALO = 8


def _split_bf16(x):
    hi = x.astype(BF16)
    return hi, (x - hi.astype(F32)).astype(BF16)


def _group_sum(x, ones_bd):
    rows = x.shape[0]
    hi, lo = _split_bf16(x)
    r = _dot(jnp.concatenate([hi, lo], axis=0), ones_bd)
    return r[:rows] + r[rows:]


def _unit_lower_inverses(lows, rows, cols):
    n = lows[0].shape[0]
    eye = jnp.where(rows == cols, 1.0, 0.0)
    first = (rows % 2 == 1) & (cols == rows - 1)
    invs = [eye - jnp.where(first, low, 0.0) for low in lows]
    size = 2
    while size < GDN_CHUNK:
        rb = rows // size
        level = (rb % 2 == 1) & (cols // size == rb - 1)
        splits = [_split_bf16(inv) for inv in invs]
        xs = []
        for low, (d_hi, d_lo) in zip(lows, splits):
            x2 = _dot(jnp.where(level, low, 0.0).astype(BF16), jnp.concatenate([d_hi, d_lo], axis=1))
            xs.append(_split_bf16(x2[:, :n] + x2[:, n:]))
        nxt = []
        for inv, (d_hi, d_lo), (x_hi, x_lo) in zip(invs, splits, xs):
            y4 = _dot(jnp.concatenate([d_hi, d_lo], axis=0), jnp.concatenate([x_hi, x_lo], axis=1))
            nxt.append(inv - (y4[:n, :n] + y4[:n, n:] + y4[n:, :n] + y4[n:, n:]))
        invs = nxt
        size *= 2
    return invs


def _gdn_kernel(qp_ref, kp_ref, vp_ref, q_ref, k_ref, v_ref, z_ref, gate_ref, cw_ref, alog_ref, dtb_ref,
                ng_ref, ones_ref, o_ref, xs_ref, y_ref, state_ref):
    tb = q_ref.shape[1]
    c = GDN_CHUNK
    n = 2 * c
    gw = GROUP_WIDTH
    first = pl.program_id(1) == 0

    @pl.when(first)
    def _():
        state_ref[...] = jnp.zeros_like(state_ref)

    ones_bd = ones_ref[...]

    base = GDN_HALO - (SHORT_CONV - 1)

    def conv_silu(idx, p_ref, c_ref):
        xs_ref[0:GDN_HALO, :] = jnp.where(first, 0.0, p_ref[0])
        xs_ref[GDN_HALO:GDN_HALO + tb, :] = c_ref[0]
        acc = None
        for tap in range(SHORT_CONV):
            term = cw_ref[tap:tap + 1, idx * gw:(idx + 1) * gw] * xs_ref[base + tap:base + tap + tb, :]
            acc = term if acc is None else acc + term
        return _silu(acc)

    def l2_normalize(y):
        return y * lax.rsqrt(_group_sum(y * y, ones_bd) + L2_EPS)

    qn = l2_normalize(conv_silu(0, qp_ref, q_ref)) * (HEAD_DIM ** -0.5)
    kn = l2_normalize(conv_silu(1, kp_ref, k_ref))
    vn = conv_silu(2, vp_ref, v_ref)

    gates = gate_ref[0]
    log_decay = -jnp.exp(alog_ref[...]) * _softplus(gates + dtb_ref[...])
    beta_all = _sigmoid(gates)
    rows_t = lax.broadcasted_iota(jnp.int32, (tb, tb), 0)
    cols_t = lax.broadcasted_iota(jnp.int32, (tb, tb), 1)
    chunk_tri = ((cols_t <= rows_t) & (cols_t // c == rows_t // c)).astype(F32)
    gcum = _dot_f32(chunk_tri, log_decay)
    lane_bcast = lambda x, lane: jnp.broadcast_to(x[:, lane:lane + 1], (tb, PAIR_WIDTH))
    g_wide = [lane_bcast(gcum, GATE_A0 + h) for h in range(GROUP_HEADS)]
    b_wide = [lane_bcast(beta_all, GATE_B0 + h) for h in range(GROUP_HEADS)]
    low_half = lax.broadcasted_iota(jnp.int32, (tb, PAIR_WIDTH), 1) < HEAD_DIM

    rows = lax.broadcasted_iota(jnp.int32, (n, n), 0)
    cols = lax.broadcasted_iota(jnp.int32, (n, n), 1)
    same_head = rows // c == cols // c
    lower_incl = same_head & (cols <= rows)
    strict_lower = same_head & (cols < rows)

    pairs = []
    for p in range(GROUP_HEADS // 2):
        ls = slice(p * PAIR_WIDTH, (p + 1) * PAIR_WIDTH)
        g_nat = jnp.where(low_half, g_wide[2 * p], g_wide[2 * p + 1])
        b_nat = jnp.where(low_half, b_wide[2 * p], b_wide[2 * p + 1])
        eg = jnp.exp(g_nat)
        k_beta = kn[:, ls] * b_nat
        pairs.append(dict(ls=ls, g_nat=g_nat, q=qn[:, ls], k=kn[:, ls], q_dec=qn[:, ls] * eg, k_beta=k_beta,
                          v_beta=vn[:, ls] * b_nat, kb_eg=k_beta * eg))

    n_chunks = tb // c
    systems = [(ci, p) for ci in range(n_chunks) for p in range(len(pairs))]
    chunk_rows = lambda ci: slice(ci * c, (ci + 1) * c)

    lkks, a_qks = [], []
    for ci, p in systems:
        d, r = pairs[p], chunk_rows(ci)
        g_col = jnp.concatenate([g_wide[2 * p][r], g_wide[2 * p + 1][r]], axis=0)
        decay = jnp.exp(jnp.where(lower_incl, g_col - g_col.T, -jnp.inf))
        k_st = _stack_heads(d["k"][r]).astype(BF16)
        lhs = jnp.concatenate([_stack_heads(d["k_beta"][r]), _stack_heads(d["q"][r])], axis=0).astype(BF16)
        gram = _dot_nt(lhs, k_st)
        lkks.append(jnp.where(strict_lower, gram[:n] * decay, 0.0))
        a_qks.append((gram[n:] * decay).astype(BF16))
    t_invs = _unit_lower_inverses(lkks, rows, cols)
    uws = []
    for (ci, p), t_inv in zip(systems, t_invs):
        d, r = pairs[p], chunk_rows(ci)
        rhs = jnp.concatenate([_stack_heads(d["v_beta"][r]), _stack_heads(d["kb_eg"][r])], axis=1)
        uws.append(_dot(t_inv.astype(BF16), rhs.astype(BF16)))

    states = [state_ref[p] for p in range(len(pairs))]
    for idx, (ci, p) in enumerate(systems):
        d, r = pairs[p], chunk_rows(ci)
        uw, state = uws[idx], states[p]
        g_last = d["g_nat"][ci * c + c - 1:ci * c + c, :]
        k_dec = _stack_heads(d["k"][r] * jnp.exp(g_last - d["g_nat"][r])).astype(BF16)
        wq = _dot(jnp.concatenate([uw[:, n:].astype(BF16), _stack_heads(d["q_dec"][r]).astype(BF16)], axis=0),
                  state.astype(BF16))
        vnb = (uw[:, :n] - wq[:n]).astype(BF16)
        o_st = wq[n:] + _dot(a_qks[idx], vnb)
        states[p] = state * jnp.exp(g_last) + _dot_tn(k_dec, vnb)
        y_ref[r, d["ls"]] = o_st[:c] + o_st[c:]
    for p in range(len(pairs)):
        state_ref[p] = states[p]

    o = y_ref[...]
    mean_sq = _group_sum(o * o, ones_bd) * (1.0 / HEAD_DIM)
    o_ref[0] = o * lax.rsqrt(mean_sq + RMS_EPS) * ng_ref[...] * _silu(z_ref[0])


def _gdn_mixer(q, k, v, z, gates, layer, conv_w, a_log_rows, dt_bias_rows, norm_g_rows, head_ones):
    bsz, t, width = q.shape
    nh = GROUP_HEADS
    tb = min(GDN_BLOCK, t)
    per = tb // GDN_HALO
    cur = pl.BlockSpec((1, tb, width), lambda i, j: (i, j, 0))
    prev = pl.BlockSpec((1, GDN_HALO, width), lambda i, j: (i, jnp.maximum(j * per - 1, 0), 0))
    return pl.pallas_call(
        _gdn_kernel,
        grid=(bsz, t // tb),
        in_specs=[prev, prev, prev, cur, cur, cur, cur,
                  pl.BlockSpec((1, tb, GATE_LANES), lambda i, j: (i, j, 0)),
                  ] + [_layer_slab(a, layer) for a in (conv_w, a_log_rows, dt_bias_rows, norm_g_rows)]
                 + [_resident((width, width))],
        out_specs=cur,
        out_shape=jax.ShapeDtypeStruct(q.shape, F32),
        scratch_shapes=[pltpu.VMEM((GDN_HALO + tb, width), F32), pltpu.VMEM((tb, width), F32),
                        pltpu.VMEM((nh // 2, PAIR_WIDTH, PAIR_WIDTH), F32)],
        compiler_params=_params(("parallel", "arbitrary")),
        name="gdn_mixer",
    )(q, k, v, q, k, v, z, gates, conv_w, a_log_rows, dt_bias_rows, norm_g_rows, head_ones)


def _memkv_kernel(m_ref, w_ref, o_ref):
    o_ref[0] = _dot(m_ref[0].astype(BF16), w_ref[...]).astype(o_ref.dtype)


def _memory_kv(mem, layer, w_kv):
    bsz, m, d = mem.shape
    return pl.pallas_call(
        _memkv_kernel,
        grid=(bsz,),
        in_specs=[pl.BlockSpec((1, m, d), lambda i: (i, 0, 0)), _layer_slab(w_kv, layer)],
        out_specs=pl.BlockSpec((1, m, 2 * d), lambda i: (i, 0, 0)),
        out_shape=jax.ShapeDtypeStruct((bsz, m, 2 * d), BF16),
        compiler_params=_params(("parallel",)),
        name="memory_kv",
    )(mem, w_kv)


def _mix_mem_kernel(x_ref, ya_ref, yb_ref, yc_ref, yd_ref, kv_ref, wout_ref, g1_ref, b1_ref, wq_ref, wo_ref,
                    g2_ref, b2_ref, o_ref):
    gw = GROUP_WIDTH
    mix = None
    for group, y_ref in enumerate((ya_ref, yb_ref, yc_ref, yd_ref)):
        part = _dot(y_ref[0].astype(BF16), wout_ref[group * gw:(group + 1) * gw, :])
        mix = part if mix is None else mix + part
    x = _layer_norm(DN_ALPHA * x_ref[0] + mix, g1_ref[...], b1_ref[...])

    q = _dot(x.astype(BF16), wq_ref[...])
    hd = MEM_HEAD_DIM
    outs = []
    for h in range(MEM_HEADS):
        qh = (q[:, h * hd:(h + 1) * hd] * (hd ** -0.5)).astype(BF16)
        kh = kv_ref[0, :, h * hd:(h + 1) * hd]
        vh = kv_ref[0, :, D_MODEL + h * hd:D_MODEL + (h + 1) * hd]
        s = _dot_nt(qh, kh)
        p = jnp.exp(s - jnp.max(s, axis=-1, keepdims=True))
        p = p / jnp.sum(p, axis=-1, keepdims=True)
        outs.append(_dot(p.astype(BF16), vh).astype(BF16))
    y = _dot(jnp.concatenate(outs, axis=-1), wo_ref[...])
    o_ref[0] = _layer_norm(DN_ALPHA * x + y, g2_ref[...], b2_ref[...])


def _mix_and_memory_ln(x, ya, yb, yc, yd, kv, layer, w_out, g_mix, b_mix, wq, wo, g_mem, b_mem):
    bsz, t, d = x.shape
    tm = min(TOKEN_TILE, t)
    row = lambda c: pl.BlockSpec((1, tm, c), lambda i, j: (i, j, 0))
    return pl.pallas_call(
        _mix_mem_kernel,
        grid=(bsz, t // tm),
        in_specs=[row(d)] + [row(GROUP_WIDTH)] * 4 + [pl.BlockSpec((1, kv.shape[1], 2 * d), lambda i, j: (i, 0, 0))]
                 + [_layer_slab(a, layer) for a in (w_out, g_mix, b_mix, wq, wo, g_mem, b_mem)],
        out_specs=row(d),
        out_shape=jax.ShapeDtypeStruct(x.shape, F32),
        compiler_params=_params(("parallel", "parallel")),
        name="mix_and_memory_ln",
    )(x, ya, yb, yc, yd, kv, w_out, g_mix, b_mix, wq, wo, g_mem, b_mem)


def _combined_in_weights(w_in):
    gw, nh = GROUP_WIDTH, GROUP_HEADS
    o = 0
    gdn_qkv = w_in[..., o:o + 3 * gw]; o += 3 * gw
    gdn_z = w_in[..., o:o + gw]; o += gw
    gdn_a = w_in[..., o:o + nh]; o += nh
    gdn_b = w_in[..., o:o + nh]; o += nh
    fox_qkv = w_in[..., o:o + 3 * gw]; o += 3 * gw
    fox_f = w_in[..., o:o + nh]; o += nh
    conf = w_in[..., o:o + 2 * gw]; o += 2 * gw
    sb_qkv = w_in[..., o:o + 3 * gw]
    pad = jnp.zeros(w_in.shape[:-1] + (GATE_LANES - 3 * nh,), w_in.dtype)
    return jnp.concatenate([gdn_qkv, gdn_z, fox_qkv, conf, sb_qkv, gdn_a, gdn_b, fox_f, pad], axis=-1).astype(BF16)


def _gate_rows(vals, first_lane):
    depth, nh = vals.shape
    return jnp.pad(vals, ((0, 0), (first_lane, GATE_LANES - first_lane - nh))).reshape(depth, 1, GATE_LANES)


def kernel(x, mem, ffn1_w_gate, ffn1_w_up, ffn1_w_down, ln_ffn1_g, ln_ffn1_b, w_in, gdn_conv_w, gdn_a_log, gdn_dt_bias, gdn_norm_g, fox_b_f, conf_dw_w, conf_dw_b, conf_norm_g, conf_norm_b, w_out, ln_mix_g, ln_mix_b, mem_w_q, mem_w_kv, mem_w_o, ln_mem_g, ln_mem_b, ffn2_w_gate, ffn2_w_up, ffn2_w_down, ln_ffn2_g, ln_ffn2_b):
    bf = lambda a: a.astype(BF16)
    rows = lambda a: a.reshape(a.shape[0], 1, a.shape[1])
    ffn1 = (bf(ffn1_w_gate), bf(ffn1_w_up), bf(ffn1_w_down), rows(ln_ffn1_g), rows(ln_ffn1_b))
    ffn2 = (bf(ffn2_w_gate), bf(ffn2_w_up), bf(ffn2_w_down), rows(ln_ffn2_g), rows(ln_ffn2_b))
    w_cat = _combined_in_weights(w_in)
    gdn = (gdn_conv_w, _gate_rows(gdn_a_log, GATE_A0), _gate_rows(gdn_dt_bias, GATE_A0),
           rows(jnp.tile(gdn_norm_g, (1, GROUP_HEADS))),
           jnp.kron(jnp.eye(GROUP_HEADS, dtype=BF16), jnp.ones((HEAD_DIM, HEAD_DIM), BF16)))
    fox_bias = _gate_rows(fox_b_f, GATE_F0)
    conf = (conf_dw_w, rows(conf_dw_b), rows(conf_norm_g), rows(conf_norm_b))
    mix_mem = (bf(w_out), rows(ln_mix_g), rows(ln_mix_b), bf(mem_w_q), bf(mem_w_o), rows(ln_mem_g), rows(ln_mem_b))
    w_kv = bf(mem_w_kv)

    for i in range(DEPTH):
        x = _ffn_ln(x, i, *ffn1)
        gq, gk, gv, gz, fq, fk, fv, glu, sq, sk, sv, gates = _in_projection(x, i, w_cat)
        y_a = _gdn_mixer(gq, gk, gv, gz, gates, i, *gdn)
        y_b = _fox_attention(fq, fk, fv, _fox_cumsum(gates, i, fox_bias))
        y_c = _conv_module(glu, i, *conf)
        y_d = _sb_attention(sq, sk, sv)
        x = _mix_and_memory_ln(x, y_a, y_b, y_c, y_d, _memory_kv(mem, i, w_kv), i, *mix_mem)
        x = _ffn_ln(x, i, *ffn2)
    return x
```

```python
import functools
import math

import jax
import jax.numpy as jnp
from jax import lax
from jax.experimental import pallas as pl
from jax.experimental.pallas import tpu as pltpu

F32 = jnp.float32
BF16 = jnp.bfloat16

D_MODEL = 1024
DEPTH = 2
GROUP_WIDTH = 256
HEAD_DIM = 64
GROUP_HEADS = 4
D_FF = 2816
SHORT_CONV = 4
CONF_KERNEL = 31
CONF_GROUPS = 4
GDN_CHUNK = 64
N_MEM = 256
MEM_HEADS = 4
MEM_HEAD_DIM = D_MODEL // MEM_HEADS
DN_ALPHA = float((2 * DEPTH) ** 0.25)
LN_EPS = 1e-5
RMS_EPS = 1e-6
L2_EPS = 1e-6
NEG_BIG = -1e30

GATE_LANES = 128
GATE_A0, GATE_B0, GATE_F0 = 0, 4, 8

TOKEN_TILE = 512
MXU_WIDTH = 256
SUBLANES = 8
FFN_SPLITS = (0, 6 * MXU_WIDTH, D_FF)
PAIR_WIDTH = 2 * HEAD_DIM
FOX_BLOCK = 512
SB_Q_BLOCK = 512
SB_K_BLOCK = 256
ATTN_SUBTILE = 256
MIX_TILE = 1024
ROW_SUBTILE = 512
SB_LOG_UNDERFLOW = -105.0
FOX_LOG_UNDERFLOW = -106.0
GDN_BLOCK = 256
CONV_BLOCK = 512
VMEM_LIMIT = 56 * 1024 * 1024


def _params(sem, vmem=VMEM_LIMIT):
    return pltpu.CompilerParams(dimension_semantics=sem, vmem_limit_bytes=vmem)


def _resident(shape):
    nd = len(shape)
    return pl.BlockSpec(shape, lambda *_: (0,) * nd, pipeline_mode=pl.Buffered(1))


def _layer_slab(stacked, layer):
    tail = stacked.shape[1:]
    return pl.BlockSpec((None,) + tail, lambda *_: (layer,) + (0,) * len(tail), pipeline_mode=pl.Buffered(1))


def _layer_norm(y, g, b):
    mu = jnp.mean(y, axis=-1, keepdims=True)
    d = y - mu
    var = jnp.mean(d * d, axis=-1, keepdims=True)
    return d * lax.rsqrt(var + LN_EPS) * g + b


def _sigmoid(x):
    return 1.0 / (1.0 + jnp.exp(-x))


def _silu(x):
    return x * _sigmoid(x)


def _softplus(x):
    return jnp.maximum(x, 0.0) + jnp.log(1.0 + jnp.exp(-jnp.abs(x)))


def _dot(a, b):
    return jnp.dot(a, b, preferred_element_type=F32)


def _dot_nt(a, b):
    return lax.dot_general(a, b, (((1,), (1,)), ((), ())), preferred_element_type=F32)


def _dot_tn(a, b):
    return lax.dot_general(a, b, (((0,), (0,)), ((), ())), preferred_element_type=F32)


def _mask_dot(mask, x):
    n = x.shape[1]
    hi = x.astype(BF16)
    r1 = x - hi.astype(F32)
    mid = r1.astype(BF16)
    lo = (r1 - mid.astype(F32)).astype(BF16)
    r = _dot(mask.astype(BF16), jnp.concatenate([hi, mid, lo], axis=1))
    return r[:, :n] + r[:, n:2 * n] + r[:, 2 * n:]


def _ffn_kernel(x_ref, wg_ref, wu_ref, wd_ref, g_ref, b_ref, o_ref):
    x = x_ref[0]
    xb = x.astype(BF16)
    acc = None
    for lo, hi in zip(FFN_SPLITS[:-1], FFN_SPLITS[1:]):
        h = _dot(xb, wg_ref[:, lo:hi])
        u = _dot(xb, wu_ref[:, lo:hi])
        a = (_silu(h) * u).astype(BF16)
        part = _dot(a, wd_ref[lo:hi, :])
        acc = part if acc is None else acc + part
    o_ref[0] = _layer_norm(DN_ALPHA * x + 0.5 * acc, g_ref[...], b_ref[...])


def _ffn_ln(x, layer, wg, wu, wd, g, b):
    bsz, t, d = x.shape
    tm = min(TOKEN_TILE, t)
    row = pl.BlockSpec((1, tm, d), lambda i, j: (i, j, 0))
    return pl.pallas_call(
        _ffn_kernel,
        grid=(bsz, t // tm),
        in_specs=[row] + [_layer_slab(a, layer) for a in (wg, wu, wd, g, b)],
        out_specs=row,
        out_shape=jax.ShapeDtypeStruct(x.shape, F32),
        compiler_params=_params(("parallel", "parallel")),
        name="ffn_ln",
    )(x, wg, wu, wd, g, b)


def _inproj_kernel(x_ref, w_ref, gq_ref, gk_ref, gv_ref, gz_ref, fq_ref, fk_ref, fv_ref,
                   glu_ref, sq_ref, sk_ref, sv_ref, gate_ref):
    xb = x_ref[0].astype(BF16)
    gw = GROUP_WIDTH

    def group(i, width=gw):
        return _dot(xb, w_ref[:, i * gw:i * gw + width])

    qk_scale = HEAD_DIM ** -0.5
    gq_ref[0] = group(0)
    gk_ref[0] = group(1)
    gv_ref[0] = group(2)
    gz_ref[0] = group(3)
    fq_ref[0] = (group(4) * qk_scale).astype(BF16)
    fk_ref[0] = group(5).astype(BF16)
    fv_ref[0] = group(6).astype(BF16)
    glu_ref[0] = group(7) * _sigmoid(group(8))
    sq_ref[0] = (group(9) * qk_scale).astype(BF16)
    sk_ref[0] = group(10).astype(BF16)
    sv_ref[0] = group(11).astype(BF16)
    gate_ref[0] = group(12, GATE_LANES)


def _in_projection(x, layer, w_cat):
    bsz, t, d = x.shape
    tm = min(TOKEN_TILE, t)
    row = lambda c: pl.BlockSpec((1, tm, c), lambda i, j: (i, j, 0))
    wide = lambda dt: jax.ShapeDtypeStruct((bsz, t, GROUP_WIDTH), dt)
    out_shape = ([wide(F32)] * 4 + [wide(BF16)] * 3 + [wide(F32)] + [wide(BF16)] * 3
                 + [jax.ShapeDtypeStruct((bsz, t, GATE_LANES), F32)])
    out_specs = [row(GROUP_WIDTH)] * 11 + [row(GATE_LANES)]
    return pl.pallas_call(
        _inproj_kernel,
        grid=(bsz, t // tm),
        in_specs=[row(d), _layer_slab(w_cat, layer)],
        out_specs=out_specs,
        out_shape=out_shape,
        compiler_params=_params(("parallel", "parallel")),
        name="in_projection",
    )(x, w_cat)


def _fox_cum_kernel(gate_ref, bias_ref, o_ref, carry_ref):
    @pl.when(pl.program_id(1) == 0)
    def _():
        carry_ref[...] = jnp.zeros_like(carry_ref)

    tb = gate_ref.shape[1]
    logit = gate_ref[0] + bias_ref[...]
    log_f = -_softplus(-logit)
    rows = lax.broadcasted_iota(jnp.int32, (tb, tb), 0)
    cols = lax.broadcasted_iota(jnp.int32, (tb, tb), 1)
    cum = _mask_dot(cols <= rows, log_f) + carry_ref[...]
    carry_ref[...] = cum[tb - 1:tb, :]
    o_ref[0, 0] = cum.T


def _fox_cumsum(gates, layer, bias_rows):
    bsz, t, _ = gates.shape
    tb = min(FOX_BLOCK, t)
    return pl.pallas_call(
        _fox_cum_kernel,
        grid=(bsz, t // tb),
        in_specs=[pl.BlockSpec((1, tb, GATE_LANES), lambda i, j: (i, j, 0)), _layer_slab(bias_rows, layer)],
        out_specs=pl.BlockSpec((1, 1, GATE_LANES, tb), lambda i, j: (i, j, 0, 0)),
        out_shape=jax.ShapeDtypeStruct((bsz, t // tb, GATE_LANES, tb), F32),
        scratch_shapes=[pltpu.VMEM((1, GATE_LANES), F32)],
        compiler_params=_params(("parallel", "arbitrary")),
        name="fox_cumsum",
    )(gates, bias_rows)


def _causal_masks(blk):
    rows = lax.broadcasted_iota(jnp.int32, (blk, blk), 0)
    cols = lax.broadcasted_iota(jnp.int32, (blk, blk), 1)
    return cols <= rows, cols < rows


def _stack_heads(x2):
    lane = lax.broadcasted_iota(jnp.int32, x2.shape, 1)
    zero = jnp.zeros_like(x2)
    return jnp.concatenate([jnp.where(lane < HEAD_DIM, x2, zero), jnp.where(lane >= HEAD_DIM, x2, zero)], axis=0)


def _unstack_heads(y, rows):
    lane = lax.broadcasted_iota(jnp.int32, (rows, PAIR_WIDTH), 1)
    return jnp.where(lane < HEAD_DIM, y[:rows], y[rows:])


def _skewed(tiles, stages):
    tiles = list(tiles)
    for t in range(len(tiles) + len(stages) - 1):
        for k in reversed(range(len(stages))):
            if 0 <= t - k < len(tiles):
                stages[k](tiles[t - k])


def _fox_kernel(q_ref, k_ref, v_ref, ck_ref, o_ref, kmax_ref, *, sub):
    blk = q_ref.shape[1]
    pair = pl.program_id(1)
    qi = pl.program_id(2)

    @pl.when(qi == 0)
    def _():
        def widest(c, best):
            kk = k_ref[0, pl.ds(pl.multiple_of(c * blk, blk), blk), :].astype(F32)
            return jnp.maximum(best, jnp.sum(kk * kk, axis=-1, keepdims=True))
        best = lax.fori_loop(0, k_ref.shape[1] // blk, widest, jnp.zeros((blk, 1), F32))
        kmax_ref[...] = jnp.sqrt(jnp.max(best, axis=0, keepdims=True))

    qs = _stack_heads(q_ref[0])
    n_sub = 2 * blk // sub
    tile = lambda i: slice(i * sub, (i + 1) * sub)
    qf = qs.astype(F32)
    qk_bound = jnp.sqrt(jnp.sum(qf * qf, axis=-1, keepdims=True)) * kmax_ref[...]
    rows = lax.broadcasted_iota(jnp.int32, (sub, blk), 0)
    cols = lax.broadcasted_iota(jnp.int32, (sub, blk), 1)
    ones = jnp.ones((blk, PAIR_WIDTH), BF16)

    def step(j, carry, masked):
        ms, accs = carry
        start = pl.multiple_of(j * blk, blk)
        kb = k_ref[0, pl.ds(start, blk), :]
        vb = jnp.concatenate([v_ref[0, pl.ds(start, blk), :], ones], axis=1)
        ms, accs = list(ms), list(accs)
        s, p, alpha = [None] * n_sub, [None] * n_sub, [None] * n_sub

        def logits(i):
            head, off = divmod(i * sub, blk)
            si = _dot_nt(qs[tile(i)], kb) - ck_ref[0, j, pl.ds(2 * pair + head, 1), :]
            s[i] = jnp.where(cols <= rows + off, si, NEG_BIG) if masked else si

        def probs(i):
            m_new = jnp.maximum(ms[i], jnp.max(s[i], axis=-1, keepdims=True))
            p[i] = jnp.exp(s[i] - m_new).astype(BF16)
            alpha[i] = jnp.exp(ms[i] - m_new)
            ms[i] = m_new

        def values(i):
            accs[i] = alpha[i] * accs[i] + _dot(p[i], vb)

        _skewed(range(n_sub), [logits, probs, values])
        return tuple(ms), tuple(accs)

    def headroom(j, ms):
        room = None
        for i in range(n_sub):
            last = -ck_ref[0, j, pl.ds(2 * pair + (i * sub) // blk, 1), :][:, blk - 1:blk]
            r = jnp.max(qk_bound[tile(i)] - ms[i] + last)
            room = r if room is None else jnp.maximum(room, r)
        return room

    init = (tuple(jnp.full((sub, 1), NEG_BIG, F32) for _ in range(n_sub)),
            tuple(jnp.zeros((sub, 2 * PAIR_WIDTH), F32) for _ in range(n_sub)))
    carry = step(qi, init, True)

    def more(state):
        i, room, _ = state
        return (i < qi) & (room > FOX_LOG_UNDERFLOW)

    def sweep(state):
        i, _, c = state
        c = step(qi - 1 - i, c, False)
        return i + 1, headroom(jnp.maximum(qi - 2 - i, 0), c[0]), c

    _, _, (_, accs) = lax.while_loop(more, sweep, (jnp.int32(0), headroom(jnp.maximum(qi - 1, 0), carry[0]), carry))
    acc = jnp.concatenate(accs, axis=0)
    o_ref[0] = _unstack_heads(acc[:, :PAIR_WIDTH] / acc[:, PAIR_WIDTH:PAIR_WIDTH + 1], blk)


def _fox_attention(q, k, v, cum_t):
    bsz, t, width = q.shape
    blk = min(FOX_BLOCK, t)
    nk = t // blk
    whole = pl.BlockSpec((1, t, PAIR_WIDTH), lambda b, p, i: (b, 0, p))
    qblk = pl.BlockSpec((1, blk, PAIR_WIDTH), lambda b, p, i: (b, i, p))
    return pl.pallas_call(
        functools.partial(_fox_kernel, sub=min(ATTN_SUBTILE, blk)),
        grid=(bsz, width // PAIR_WIDTH, nk),
        in_specs=[qblk, whole, whole,
                  pl.BlockSpec((1, nk, 8, blk), lambda b, p, i: (b, 0, GATE_F0 // 8, 0))],
        out_specs=qblk,
        out_shape=jax.ShapeDtypeStruct((bsz, t, width), F32),
        scratch_shapes=[pltpu.VMEM((1, 1), F32)],
        compiler_params=_params(("arbitrary", "arbitrary", "arbitrary")),
        name="fox_attention",
    )(q, k, v, cum_t)


def _sb_kernel(q_ref, k_ref, v_ref, o_ref, *, tk, sub):
    tq = q_ref.shape[1]
    per = tq // tk
    qi = pl.program_id(2)
    qs = _stack_heads(q_ref[0])
    n_sub = 2 * tq // sub
    rows = lax.broadcasted_iota(jnp.int32, (sub, tk), 0)
    cols = lax.broadcasted_iota(jnp.int32, (sub, tk), 1)
    suffix = _causal_masks(tk)[0].astype(BF16)

    def step(j, carry, diag):
        rests, accs = carry
        start = pl.multiple_of(j * tk, tk)
        kb = k_ref[0, pl.ds(start, tk), :]
        vb = v_ref[0, pl.ds(start, tk), :]
        rests, accs = list(rests), list(accs)
        z, split, w, strict = [None] * n_sub, [None] * n_sub, [None] * n_sub, [None] * n_sub

        def logits(i):
            z[i] = _dot_nt(qs[i * sub:(i + 1) * sub], kb)
            if diag is not None:
                strict[i] = cols + diag < rows + (i * sub) % tq

        def keep(i):
            log_keep = -_softplus(z[i])
            if diag is not None:
                log_keep = jnp.where(strict[i], log_keep, 0.0)
            split[i] = jnp.concatenate(_split_bf16(log_keep), axis=0)

        def weights(i):
            tails = _dot(split[i], suffix)
            tail = tails[:sub] + tails[sub:]
            wi = jnp.exp(z[i] + tail + rests[i])
            if diag is not None:
                wi = jnp.where(strict[i], wi, 0.0)
            w[i] = wi.astype(BF16)
            rests[i] = rests[i] + tail[:, 0:1]

        def values(i):
            accs[i] = accs[i] + _dot(w[i], vb)

        live = [i for i in range(n_sub) if diag is None or diag < (i * sub) % tq + sub - 1]
        _skewed(live, [logits, keep, weights, values])
        return tuple(rests), tuple(accs)

    carry = (tuple(jnp.zeros((sub, 1), F32) for _ in range(n_sub)),
             tuple(jnp.zeros((sub, PAIR_WIDTH), F32) for _ in range(n_sub)))
    for d in reversed(range(per)):
        carry = step(qi * per + d, carry, d * tk)

    def largest(rests):
        return functools.reduce(jnp.maximum, [jnp.max(r) for r in rests])

    def more(state):
        i, top, _ = state
        return (i < qi * per) & (top > SB_LOG_UNDERFLOW)

    def sweep(state):
        i, _, c = state
        c = step(qi * per - 1 - i, c, None)
        return i + 1, largest(c[0]), c

    _, _, (_, accs) = lax.while_loop(more, sweep, (jnp.int32(0), largest(carry[0]), carry))
    o_ref[0] = _unstack_heads(jnp.concatenate(accs, axis=0), tq)


def _sb_attention(q, k, v):
    bsz, t, width = q.shape
    tq = min(SB_Q_BLOCK, t)
    tk = min(SB_K_BLOCK, tq)
    whole = pl.BlockSpec((1, t, PAIR_WIDTH), lambda b, p, i: (b, 0, p))
    qblk = pl.BlockSpec((1, tq, PAIR_WIDTH), lambda b, p, i: (b, i, p))
    return pl.pallas_call(
        functools.partial(_sb_kernel, tk=tk, sub=min(ATTN_SUBTILE, tq)),
        grid=(bsz, width // PAIR_WIDTH, t // tq),
        in_specs=[qblk, whole, whole],
        out_specs=qblk,
        out_shape=jax.ShapeDtypeStruct((bsz, t, width), F32),
        compiler_params=_params(("parallel", "parallel", "arbitrary")),
        name="sb_attention",
    )(q, k, v)


CONV_HALO = 32


def _conv_kernel(prev_ref, cur_ref, w_ref, b_ref, ng_ref, nb_ref, o_ref, xs_ref, sh_ref):
    tb = cur_ref.shape[1]
    first = pl.program_id(1) == 0
    xs_ref[0:CONV_HALO, :] = jnp.where(first, 0.0, prev_ref[0])
    xs_ref[CONV_HALO:CONV_HALO + tb, :] = cur_ref[0]
    span = tb + CONV_HALO - SUBLANES
    for r in range(1, SUBLANES):
        sh_ref[r, 0:span, :] = xs_ref[r:r + span, :]
    base = CONV_HALO - (CONF_KERNEL - 1)
    acc = jnp.zeros((tb, GROUP_WIDTH), F32) + b_ref[...]
    for tap in range(CONF_KERNEL):
        whole, r = divmod(base + tap, SUBLANES)
        src = xs_ref if r == 0 else sh_ref.at[r]
        acc = acc + w_ref[tap:tap + 1, :] * src[whole * SUBLANES:whole * SUBLANES + tb, :]
    gsz = GROUP_WIDTH // CONF_GROUPS
    parts = []
    for g in range(CONF_GROUPS):
        cg = acc[:, g * gsz:(g + 1) * gsz]
        mu = jnp.mean(cg, axis=-1, keepdims=True)
        d = cg - mu
        var = jnp.mean(d * d, axis=-1, keepdims=True)
        parts.append(d * lax.rsqrt(var + LN_EPS))
    hn = jnp.concatenate(parts, axis=-1) * ng_ref[...] + nb_ref[...]
    o_ref[0] = _silu(hn)


def _conv_module(glu, layer, w, b, ng, nb):
    bsz, t, c = glu.shape
    tb = min(CONV_BLOCK, t)
    per = tb // CONV_HALO
    return pl.pallas_call(
        _conv_kernel,
        grid=(bsz, t // tb),
        in_specs=[pl.BlockSpec((1, CONV_HALO, c), lambda i, j: (i, jnp.maximum(j * per - 1, 0), 0)),
                  pl.BlockSpec((1, tb, c), lambda i, j: (i, j, 0)),
                  ] + [_layer_slab(a, layer) for a in (w, b, ng, nb)],
        out_specs=pl.BlockSpec((1, tb, c), lambda i, j: (i, j, 0)),
        out_shape=jax.ShapeDtypeStruct(glu.shape, F32),
        scratch_shapes=[pltpu.VMEM((CONV_HALO + tb, c), F32), pltpu.VMEM((SUBLANES, CONV_HALO + tb, c), F32)],
        compiler_params=_params(("parallel", "parallel")),
        name="conv_module",
    )(glu, glu, w, b, ng, nb)


GDN_HALO = 8


def _split_bf16(x):
    hi = x.astype(BF16)
    return hi, (x - hi.astype(F32)).astype(BF16)


def _group_sum(x, ones_bd):
    rows = x.shape[0]
    hi, lo = _split_bf16(x)
    r = _dot(jnp.concatenate([hi, lo], axis=0), ones_bd)
    return r[:rows] + r[rows:]


def _odd_blocks(x, size):
    return jnp.concatenate([x[r:r + size] for r in range(size, x.shape[0], 2 * size)], axis=0)


def _spread_odd_blocks(y, size, fill):
    parts = []
    for b in range(y.shape[0] // size):
        parts += [fill[2 * b * size:(2 * b + 1) * size], y[b * size:(b + 1) * size]]
    return jnp.concatenate(parts, axis=0)


def _unit_lower_inverses(lows, rows, cols):
    n = lows[0].shape[0]
    eye = jnp.where(rows == cols, 1.0, 0.0)
    first = (rows % 2 == 1) & (cols == rows - 1)
    invs = [eye - jnp.where(first, low, 0.0) for low in lows]
    size = 2
    while size < GDN_CHUNK:
        rb = rows // size
        level = (rb % 2 == 1) & (cols // size == rb - 1)
        odd_only = size % SUBLANES == 0
        pick = (lambda x: _odd_blocks(x, size)) if odd_only else (lambda x: x)
        splits = [_split_bf16(inv) for inv in invs]
        xs = []
        for low, (d_hi, d_lo) in zip(lows, splits):
            x2 = _dot(pick(jnp.where(level, low, 0.0)).astype(BF16), jnp.concatenate([d_hi, d_lo], axis=1))
            x = x2[:, :n] + x2[:, n:]
            if odd_only:
                x = _spread_odd_blocks(x, size, jnp.zeros((n, n), F32))
            xs.append(_split_bf16(x))
        nxt = []
        for inv, (d_hi, d_lo), (x_hi, x_lo) in zip(invs, splits, xs):
            m = n // 2 if odd_only else n
            lhs = jnp.concatenate([pick(inv).astype(BF16), pick(inv - d_hi.astype(F32)).astype(BF16)], axis=0) \
                if odd_only else jnp.concatenate([d_hi, d_lo], axis=0)
            y4 = _dot(lhs, jnp.concatenate([x_hi, x_lo], axis=1))
            y = y4[:m, :n] + y4[:m, n:] + y4[m:, :n] + y4[m:, n:]
            nxt.append(_spread_odd_blocks(pick(inv) - y, size, inv) if odd_only else inv - y)
        invs = nxt
        size *= 2
    return invs


def _gdn_kernel(qp_ref, kp_ref, vp_ref, q_ref, k_ref, v_ref, z_ref, gate_ref, cw_ref, alog_ref, dtb_ref,
                ng_ref, ones_ref, o_ref, xs_ref, y_ref, state_ref):
    tb = q_ref.shape[1]
    c = GDN_CHUNK
    n = 2 * c
    gw = GROUP_WIDTH
    first = pl.program_id(1) == 0

    @pl.when(first)
    def _():
        state_ref[...] = jnp.zeros_like(state_ref)

    ones_bd = ones_ref[...]

    base = GDN_HALO - (SHORT_CONV - 1)

    def conv_silu(idx, p_ref, c_ref):
        xs_ref[0:GDN_HALO, :] = jnp.where(first, 0.0, p_ref[0])
        xs_ref[GDN_HALO:GDN_HALO + tb, :] = c_ref[0]
        acc = None
        for tap in range(SHORT_CONV):
            term = cw_ref[tap:tap + 1, idx * gw:(idx + 1) * gw] * xs_ref[base + tap:base + tap + tb, :]
            acc = term if acc is None else acc + term
        return _silu(acc)

    def l2_normalize(y):
        return y * lax.rsqrt(_group_sum(y * y, ones_bd) + L2_EPS)

    qn = l2_normalize(conv_silu(0, qp_ref, q_ref)) * (HEAD_DIM ** -0.5)
    kn = l2_normalize(conv_silu(1, kp_ref, k_ref))
    vn = conv_silu(2, vp_ref, v_ref)

    gates = gate_ref[0]
    log_decay = -jnp.exp(alog_ref[...]) * _softplus(gates + dtb_ref[...])
    beta_all = _sigmoid(gates)
    rows_t = lax.broadcasted_iota(jnp.int32, (tb, tb), 0)
    cols_t = lax.broadcasted_iota(jnp.int32, (tb, tb), 1)
    chunk_tri = (cols_t <= rows_t) & (cols_t // c == rows_t // c)
    gcum = _mask_dot(chunk_tri, log_decay)
    lane_bcast = lambda x, lane: jnp.broadcast_to(x[:, lane:lane + 1], (tb, PAIR_WIDTH))
    g_wide = [lane_bcast(gcum, GATE_A0 + h) for h in range(GROUP_HEADS)]
    b_wide = [lane_bcast(beta_all, GATE_B0 + h) for h in range(GROUP_HEADS)]
    low_half = lax.broadcasted_iota(jnp.int32, (tb, PAIR_WIDTH), 1) < HEAD_DIM

    rows = lax.broadcasted_iota(jnp.int32, (n, n), 0)
    cols = lax.broadcasted_iota(jnp.int32, (n, n), 1)
    same_head = rows // c == cols // c
    lower_incl = same_head & (cols <= rows)
    strict_lower = same_head & (cols < rows)

    pairs = []
    for p in range(GROUP_HEADS // 2):
        ls = slice(p * PAIR_WIDTH, (p + 1) * PAIR_WIDTH)
        g_nat = jnp.where(low_half, g_wide[2 * p], g_wide[2 * p + 1])
        b_nat = jnp.where(low_half, b_wide[2 * p], b_wide[2 * p + 1])
        eg = jnp.exp(g_nat)
        k_beta = kn[:, ls] * b_nat
        pairs.append(dict(ls=ls, g_nat=g_nat, q=qn[:, ls], k=kn[:, ls], q_dec=qn[:, ls] * eg, k_beta=k_beta,
                          v_beta=vn[:, ls] * b_nat, kb_eg=k_beta * eg))

    n_chunks = tb // c
    systems = [(ci, p) for ci in range(n_chunks) for p in range(len(pairs))]
    chunk_rows = lambda ci: slice(ci * c, (ci + 1) * c)

    lkks, a_qks = [], []
    for ci, p in systems:
        d, r = pairs[p], chunk_rows(ci)
        g_col = jnp.concatenate([g_wide[2 * p][r], g_wide[2 * p + 1][r]], axis=0)
        decay = jnp.exp(jnp.where(lower_incl, g_col - g_col.T, -jnp.inf))
        k_st = _stack_heads(d["k"][r]).astype(BF16)
        lhs = jnp.concatenate([_stack_heads(d["k_beta"][r]), _stack_heads(d["q"][r])], axis=0).astype(BF16)
        gram = _dot_nt(lhs, k_st)
        lkks.append(jnp.where(strict_lower, gram[:n] * decay, 0.0))
        a_qks.append((gram[n:] * decay).astype(BF16))
    t_invs = _unit_lower_inverses(lkks, rows, cols)
    uws = []
    for (ci, p), t_inv in zip(systems, t_invs):
        d, r = pairs[p], chunk_rows(ci)
        rhs = jnp.concatenate([_stack_heads(d["v_beta"][r]), _stack_heads(d["kb_eg"][r])], axis=1)
        uws.append(_dot(t_inv.astype(BF16), rhs.astype(BF16)))

    states = [state_ref[p] for p in range(len(pairs))]
    for idx, (ci, p) in enumerate(systems):
        d, r = pairs[p], chunk_rows(ci)
        uw, state = uws[idx], states[p]
        g_last = d["g_nat"][ci * c + c - 1:ci * c + c, :]
        k_dec = _stack_heads(d["k"][r] * jnp.exp(g_last - d["g_nat"][r])).astype(BF16)
        wq = _dot(jnp.concatenate([uw[:, n:].astype(BF16), _stack_heads(d["q_dec"][r]).astype(BF16)], axis=0),
                  state.astype(BF16))
        vnb = (uw[:, :n] - wq[:n]).astype(BF16)
        o_st = wq[n:] + _dot(a_qks[idx], vnb)
        states[p] = state * jnp.exp(g_last) + _dot_tn(k_dec, vnb)
        y_ref[r, d["ls"]] = o_st[:c] + o_st[c:]
    for p in range(len(pairs)):
        state_ref[p] = states[p]

    o = y_ref[...]
    mean_sq = _group_sum(o * o, ones_bd) * (1.0 / HEAD_DIM)
    o_ref[0] = o * lax.rsqrt(mean_sq + RMS_EPS) * ng_ref[...] * _silu(z_ref[0])


def _gdn_mixer(q, k, v, z, gates, layer, conv_w, a_log_rows, dt_bias_rows, norm_g_rows, head_ones):
    bsz, t, width = q.shape
    nh = GROUP_HEADS
    tb = min(GDN_BLOCK, t)
    per = tb // GDN_HALO
    cur = pl.BlockSpec((1, tb, width), lambda i, j: (i, j, 0))
    prev = pl.BlockSpec((1, GDN_HALO, width), lambda i, j: (i, jnp.maximum(j * per - 1, 0), 0))
    return pl.pallas_call(
        _gdn_kernel,
        grid=(bsz, t // tb),
        in_specs=[prev, prev, prev, cur, cur, cur, cur,
                  pl.BlockSpec((1, tb, GATE_LANES), lambda i, j: (i, j, 0)),
                  ] + [_layer_slab(a, layer) for a in (conv_w, a_log_rows, dt_bias_rows, norm_g_rows)]
                 + [_resident((width, width))],
        out_specs=cur,
        out_shape=jax.ShapeDtypeStruct(q.shape, F32),
        scratch_shapes=[pltpu.VMEM((GDN_HALO + tb, width), F32), pltpu.VMEM((tb, width), F32),
                        pltpu.VMEM((nh // 2, PAIR_WIDTH, PAIR_WIDTH), F32)],
        compiler_params=_params(("parallel", "arbitrary")),
        name="gdn_mixer",
    )(q, k, v, q, k, v, z, gates, conv_w, a_log_rows, dt_bias_rows, norm_g_rows, head_ones)


def _memkv_kernel(m_ref, w_ref, o_ref):
    o_ref[0] = _dot(m_ref[0].astype(BF16), w_ref[...]).astype(o_ref.dtype)


def _memory_kv(mem, layer, w_kv):
    bsz, m, d = mem.shape
    return pl.pallas_call(
        _memkv_kernel,
        grid=(bsz,),
        in_specs=[pl.BlockSpec((1, m, d), lambda i: (i, 0, 0)), _layer_slab(w_kv, layer)],
        out_specs=pl.BlockSpec((1, m, 2 * d), lambda i: (i, 0, 0)),
        out_shape=jax.ShapeDtypeStruct((bsz, m, 2 * d), BF16),
        compiler_params=_params(("parallel",)),
        name="memory_kv",
    )(mem, w_kv)


def _mix_mem_kernel(x_ref, ya_ref, yb_ref, yc_ref, yd_ref, kv_ref, wout_ref, g1_ref, b1_ref, wq_ref, wo_ref,
                    g2_ref, b2_ref, o_ref, *, sub):
    gw = GROUP_WIDTH
    hd = MEM_HEAD_DIM
    n_sub = x_ref.shape[1] // sub
    tile = lambda i: slice(i * sub, (i + 1) * sub)
    x1, att = [None] * n_sub, [None] * n_sub

    def mix(i):
        tot = None
        for group, y_ref in enumerate((ya_ref, yb_ref, yc_ref, yd_ref)):
            part = _dot(y_ref[0, tile(i), :].astype(BF16), wout_ref[group * gw:(group + 1) * gw, :])
            tot = part if tot is None else tot + part
        x1[i] = _layer_norm(DN_ALPHA * x_ref[0, tile(i), :] + tot, g1_ref[...], b1_ref[...])

    def attend(i):
        q = _dot(x1[i].astype(BF16), wq_ref[...])
        outs = []
        for h in range(MEM_HEADS):
            qh = (q[:, h * hd:(h + 1) * hd] * (hd ** -0.5)).astype(BF16)
            kh = kv_ref[0, :, h * hd:(h + 1) * hd]
            vh = kv_ref[0, :, D_MODEL + h * hd:D_MODEL + (h + 1) * hd]
            s = _dot_nt(qh, kh)
            p = jnp.exp(s - jnp.max(s, axis=-1, keepdims=True))
            p = p / jnp.sum(p, axis=-1, keepdims=True)
            outs.append(_dot(p.astype(BF16), vh).astype(BF16))
        att[i] = jnp.concatenate(outs, axis=-1)

    def project(i):
        y = _dot(att[i], wo_ref[...])
        o_ref[0, tile(i), :] = _layer_norm(DN_ALPHA * x1[i] + y, g2_ref[...], b2_ref[...])

    _skewed(range(n_sub), [mix, attend, project])


def _mix_and_memory_ln(x, ya, yb, yc, yd, kv, layer, w_out, g_mix, b_mix, wq, wo, g_mem, b_mem):
    bsz, t, d = x.shape
    tm = min(MIX_TILE, t)
    row = lambda c: pl.BlockSpec((1, tm, c), lambda i, j: (i, j, 0))
    return pl.pallas_call(
        functools.partial(_mix_mem_kernel, sub=min(ROW_SUBTILE, tm)),
        grid=(bsz, t // tm),
        in_specs=[row(d)] + [row(GROUP_WIDTH)] * 4 + [pl.BlockSpec((1, kv.shape[1], 2 * d), lambda i, j: (i, 0, 0))]
                 + [_layer_slab(a, layer) for a in (w_out, g_mix, b_mix, wq, wo, g_mem, b_mem)],
        out_specs=row(d),
        out_shape=jax.ShapeDtypeStruct(x.shape, F32),
        compiler_params=_params(("parallel", "parallel")),
        name="mix_and_memory_ln",
    )(x, ya, yb, yc, yd, kv, w_out, g_mix, b_mix, wq, wo, g_mem, b_mem)


def _combined_in_weights(w_in):
    gw, nh = GROUP_WIDTH, GROUP_HEADS
    o = 0
    gdn_qkv = w_in[..., o:o + 3 * gw]; o += 3 * gw
    gdn_z = w_in[..., o:o + gw]; o += gw
    gdn_a = w_in[..., o:o + nh]; o += nh
    gdn_b = w_in[..., o:o + nh]; o += nh
    fox_qkv = w_in[..., o:o + 3 * gw]; o += 3 * gw
    fox_f = w_in[..., o:o + nh]; o += nh
    conf = w_in[..., o:o + 2 * gw]; o += 2 * gw
    sb_qkv = w_in[..., o:o + 3 * gw]
    pad = jnp.zeros(w_in.shape[:-1] + (GATE_LANES - 3 * nh,), w_in.dtype)
    return jnp.concatenate([gdn_qkv, gdn_z, fox_qkv, conf, sb_qkv, gdn_a, gdn_b, fox_f, pad], axis=-1).astype(BF16)


def _gate_rows(vals, first_lane):
    depth, nh = vals.shape
    return jnp.pad(vals, ((0, 0), (first_lane, GATE_LANES - first_lane - nh))).reshape(depth, 1, GATE_LANES)


def kernel(x, mem, ffn1_w_gate, ffn1_w_up, ffn1_w_down, ln_ffn1_g, ln_ffn1_b, w_in, gdn_conv_w, gdn_a_log, gdn_dt_bias, gdn_norm_g, fox_b_f, conf_dw_w, conf_dw_b, conf_norm_g, conf_norm_b, w_out, ln_mix_g, ln_mix_b, mem_w_q, mem_w_kv, mem_w_o, ln_mem_g, ln_mem_b, ffn2_w_gate, ffn2_w_up, ffn2_w_down, ln_ffn2_g, ln_ffn2_b):
    bf = lambda a: a.astype(BF16)
    rows = lambda a: a.reshape(a.shape[0], 1, a.shape[1])
    ffn1 = (bf(ffn1_w_gate), bf(ffn1_w_up), bf(ffn1_w_down), rows(ln_ffn1_g), rows(ln_ffn1_b))
    ffn2 = (bf(ffn2_w_gate), bf(ffn2_w_up), bf(ffn2_w_down), rows(ln_ffn2_g), rows(ln_ffn2_b))
    w_cat = _combined_in_weights(w_in)
    gdn = (gdn_conv_w, _gate_rows(gdn_a_log, GATE_A0), _gate_rows(gdn_dt_bias, GATE_A0),
           rows(jnp.tile(gdn_norm_g, (1, GROUP_HEADS))),
           jnp.kron(jnp.eye(GROUP_HEADS, dtype=BF16), jnp.ones((HEAD_DIM, HEAD_DIM), BF16)))
    fox_bias = _gate_rows(fox_b_f, GATE_F0)
    conf = (conf_dw_w, rows(conf_dw_b), rows(conf_norm_g), rows(conf_norm_b))
    mix_mem = (bf(w_out), rows(ln_mix_g), rows(ln_mix_b), bf(mem_w_q), bf(mem_w_o), rows(ln_mem_g), rows(ln_mem_b))
    w_kv = bf(mem_w_kv)

    for i in range(DEPTH):
        x = _ffn_ln(x, i, *ffn1)
        gq, gk, gv, gz, fq, fk, fv, glu, sq, sk, sv, gates = _in_projection(x, i, w_cat)
        y_a = _gdn_mixer(gq, gk, gv, gz, gates, i, *gdn)
        y_b = _fox_attention(fq, fk, fv, _fox_cumsum(gates, i, fox_bias))
        y_c = _conv_module(glu, i, *conf)
        y_d = _sb_attention(sq, sk, sv)
        x = _mix_and_memory_ln(x, y_a, y_b, y_c, y_d, _memory_kv(mem, i, w_kv), i, *mix_mem)
        x = _ffn_ln(x, i, *ffn2)
    return x
```

```python
import functools
import math

import jax
import jax.numpy as jnp
from jax import lax
from jax.experimental import pallas as pl
from jax.experimental.pallas import tpu as pltpu

F32 = jnp.float32
BF16 = jnp.bfloat16

D_MODEL = 1024
DEPTH = 2
GROUP_WIDTH = 256
HEAD_DIM = 64
GROUP_HEADS = 4
D_FF = 2816
SHORT_CONV = 4
CONF_KERNEL = 31
CONF_GROUPS = 4
GDN_CHUNK = 64
N_MEM = 256
MEM_HEADS = 4
MEM_HEAD_DIM = D_MODEL // MEM_HEADS
DN_ALPHA = float((2 * DEPTH) ** 0.25)
LN_EPS = 1e-5
RMS_EPS = 1e-6
L2_EPS = 1e-6
NEG_BIG = -1e30

GATE_LANES = 128
GATE_A0, GATE_B0, GATE_F0 = 0, 4, 8

TOKEN_TILE = 512
MXU_WIDTH = 256
SUBLANES = 8
FFN_SPLITS = (0, 6 * MXU_WIDTH, D_FF)
PAIR_WIDTH = 2 * HEAD_DIM
FOX_BLOCK = 512
SB_Q_BLOCK = 512
SB_K_BLOCK = 256
ATTN_SUBTILE = 256
MIX_TILE = 1024
ROW_SUBTILE = 512
SB_LOG_UNDERFLOW = -105.0
FOX_LOG_UNDERFLOW = -106.0
GDN_BLOCK = 256
CONV_BLOCK = 512
VMEM_LIMIT = 56 * 1024 * 1024


def _params(sem, vmem=VMEM_LIMIT):
    return pltpu.CompilerParams(dimension_semantics=sem, vmem_limit_bytes=vmem)


def _resident(shape):
    nd = len(shape)
    return pl.BlockSpec(shape, lambda *_: (0,) * nd, pipeline_mode=pl.Buffered(1))


def _layer_slab(stacked, layer):
    tail = stacked.shape[1:]
    return pl.BlockSpec((None,) + tail, lambda *_: (layer,) + (0,) * len(tail), pipeline_mode=pl.Buffered(1))


def _layer_norm(y, g, b):
    mu = jnp.mean(y, axis=-1, keepdims=True)
    d = y - mu
    var = jnp.mean(d * d, axis=-1, keepdims=True)
    return d * lax.rsqrt(var + LN_EPS) * g + b


def _sigmoid(x):
    return 1.0 / (1.0 + jnp.exp(-x))


def _silu(x):
    return x * _sigmoid(x)


def _softplus(x):
    return jnp.maximum(x, 0.0) + jnp.log(1.0 + jnp.exp(-jnp.abs(x)))


def _dot(a, b):
    return jnp.dot(a, b, preferred_element_type=F32)


def _dot_nt(a, b):
    return lax.dot_general(a, b, (((1,), (1,)), ((), ())), preferred_element_type=F32)


def _dot_tn(a, b):
    return lax.dot_general(a, b, (((0,), (0,)), ((), ())), preferred_element_type=F32)


def _mask_dot(mask, x):
    n = x.shape[1]
    hi = x.astype(BF16)
    r1 = x - hi.astype(F32)
    mid = r1.astype(BF16)
    lo = (r1 - mid.astype(F32)).astype(BF16)
    r = _dot(mask.astype(BF16), jnp.concatenate([hi, mid, lo], axis=1))
    return r[:, :n] + r[:, n:2 * n] + r[:, 2 * n:]


def _ffn_kernel(x_ref, wg_ref, wu_ref, wd_ref, g_ref, b_ref, o_ref):
    x = x_ref[0]
    xb = x.astype(BF16)
    acc = None
    for lo, hi in zip(FFN_SPLITS[:-1], FFN_SPLITS[1:]):
        h = _dot(xb, wg_ref[:, lo:hi])
        u = _dot(xb, wu_ref[:, lo:hi])
        a = (_silu(h) * u).astype(BF16)
        part = _dot(a, wd_ref[lo:hi, :])
        acc = part if acc is None else acc + part
    o_ref[0] = _layer_norm(DN_ALPHA * x + 0.5 * acc, g_ref[...], b_ref[...])


def _ffn_ln(x, layer, wg, wu, wd, g, b):
    bsz, t, d = x.shape
    tm = min(TOKEN_TILE, t)
    row = pl.BlockSpec((1, tm, d), lambda i, j: (i, j, 0))
    return pl.pallas_call(
        _ffn_kernel,
        grid=(bsz, t // tm),
        in_specs=[row] + [_layer_slab(a, layer) for a in (wg, wu, wd, g, b)],
        out_specs=row,
        out_shape=jax.ShapeDtypeStruct(x.shape, F32),
        compiler_params=_params(("parallel", "parallel")),
        name="ffn_ln",
    )(x, wg, wu, wd, g, b)


def _inproj_kernel(x_ref, w_ref, gq_ref, gk_ref, gv_ref, gz_ref, fq_ref, fk_ref, fv_ref,
                   glu_ref, sq_ref, sk_ref, sv_ref, gate_ref):
    xb = x_ref[0].astype(BF16)
    gw = GROUP_WIDTH

    def group(i, width=gw):
        return _dot(xb, w_ref[:, i * gw:i * gw + width])

    qk_scale = HEAD_DIM ** -0.5
    gq_ref[0] = group(0)
    gk_ref[0] = group(1)
    gv_ref[0] = group(2)
    gz_ref[0] = group(3)
    fq_ref[0] = (group(4) * qk_scale).astype(BF16)
    fk_ref[0] = group(5).astype(BF16)
    fv_ref[0] = group(6).astype(BF16)
    glu_ref[0] = group(7) * _sigmoid(group(8))
    sq_ref[0] = (group(9) * qk_scale).astype(BF16)
    sk_ref[0] = group(10).astype(BF16)
    sv_ref[0] = group(11).astype(BF16)
    gate_ref[0] = group(12, GATE_LANES)


def _in_projection(x, layer, w_cat):
    bsz, t, d = x.shape
    tm = min(TOKEN_TILE, t)
    row = lambda c: pl.BlockSpec((1, tm, c), lambda i, j: (i, j, 0))
    wide = lambda dt: jax.ShapeDtypeStruct((bsz, t, GROUP_WIDTH), dt)
    out_shape = ([wide(F32)] * 4 + [wide(BF16)] * 3 + [wide(F32)] + [wide(BF16)] * 3
                 + [jax.ShapeDtypeStruct((bsz, t, GATE_LANES), F32)])
    out_specs = [row(GROUP_WIDTH)] * 11 + [row(GATE_LANES)]
    return pl.pallas_call(
        _inproj_kernel,
        grid=(bsz, t // tm),
        in_specs=[row(d), _layer_slab(w_cat, layer)],
        out_specs=out_specs,
        out_shape=out_shape,
        compiler_params=_params(("parallel", "parallel")),
        name="in_projection",
    )(x, w_cat)


def _fox_cum_kernel(gate_ref, bias_ref, o_ref, carry_ref):
    @pl.when(pl.program_id(1) == 0)
    def _():
        carry_ref[...] = jnp.zeros_like(carry_ref)

    tb = gate_ref.shape[1]
    logit = gate_ref[0] + bias_ref[...]
    log_f = -_softplus(-logit)
    rows = lax.broadcasted_iota(jnp.int32, (tb, tb), 0)
    cols = lax.broadcasted_iota(jnp.int32, (tb, tb), 1)
    cum = _mask_dot(cols <= rows, log_f) + carry_ref[...]
    carry_ref[...] = cum[tb - 1:tb, :]
    o_ref[0, 0] = cum.T


def _fox_cumsum(gates, layer, bias_rows):
    bsz, t, _ = gates.shape
    tb = min(FOX_BLOCK, t)
    return pl.pallas_call(
        _fox_cum_kernel,
        grid=(bsz, t // tb),
        in_specs=[pl.BlockSpec((1, tb, GATE_LANES), lambda i, j: (i, j, 0)), _layer_slab(bias_rows, layer)],
        out_specs=pl.BlockSpec((1, 1, GATE_LANES, tb), lambda i, j: (i, j, 0, 0)),
        out_shape=jax.ShapeDtypeStruct((bsz, t // tb, GATE_LANES, tb), F32),
        scratch_shapes=[pltpu.VMEM((1, GATE_LANES), F32)],
        compiler_params=_params(("parallel", "arbitrary")),
        name="fox_cumsum",
    )(gates, bias_rows)


def _causal_masks(blk):
    rows = lax.broadcasted_iota(jnp.int32, (blk, blk), 0)
    cols = lax.broadcasted_iota(jnp.int32, (blk, blk), 1)
    return cols <= rows, cols < rows


def _stack_heads(x2):
    lane = lax.broadcasted_iota(jnp.int32, x2.shape, 1)
    zero = jnp.zeros_like(x2)
    return jnp.concatenate([jnp.where(lane < HEAD_DIM, x2, zero), jnp.where(lane >= HEAD_DIM, x2, zero)], axis=0)


def _unstack_heads(y, rows):
    lane = lax.broadcasted_iota(jnp.int32, (rows, PAIR_WIDTH), 1)
    return jnp.where(lane < HEAD_DIM, y[:rows], y[rows:])


def _skewed(tiles, stages):
    tiles = list(tiles)
    for t in range(len(tiles) + len(stages) - 1):
        for k in reversed(range(len(stages))):
            if 0 <= t - k < len(tiles):
                stages[k](tiles[t - k])


def _fox_kernel(q_ref, k_ref, v_ref, ck_ref, o_ref, kmax_ref, *, sub):
    blk = q_ref.shape[1]
    n_pairs = q_ref.shape[2] // PAIR_WIDTH
    qi = pl.program_id(1)
    lanes = lambda p: slice(p * PAIR_WIDTH, (p + 1) * PAIR_WIDTH)

    @pl.when(qi == 0)
    def _():
        for p in range(n_pairs):
            def widest(c, best):
                kk = k_ref[0, pl.ds(pl.multiple_of(c * blk, blk), blk), lanes(p)].astype(F32)
                return jnp.maximum(best, jnp.sum(kk * kk, axis=-1, keepdims=True))
            best = lax.fori_loop(0, k_ref.shape[1] // blk, widest, jnp.zeros((blk, 1), F32))
            kmax_ref[p] = jnp.sqrt(jnp.max(best, axis=0, keepdims=True))

    qs = jnp.concatenate([_stack_heads(q_ref[0, :, lanes(p)]) for p in range(n_pairs)], axis=0)
    n_sub = 2 * n_pairs * blk // sub
    tile = lambda i: slice(i * sub, (i + 1) * sub)
    head_of = lambda i: (i * sub) // blk
    qf = qs.astype(F32)
    qnorm = jnp.sqrt(jnp.sum(qf * qf, axis=-1, keepdims=True))
    qk_bound = jnp.concatenate([qnorm[2 * p * blk:2 * (p + 1) * blk] * kmax_ref[p] for p in range(n_pairs)],
                               axis=0)
    rows = lax.broadcasted_iota(jnp.int32, (sub, blk), 0)
    cols = lax.broadcasted_iota(jnp.int32, (sub, blk), 1)
    ones = jnp.ones((blk, PAIR_WIDTH), BF16)

    def step(j, carry, masked):
        ms, accs = carry
        start = pl.multiple_of(j * blk, blk)
        kb = [k_ref[0, pl.ds(start, blk), lanes(p)] for p in range(n_pairs)]
        vb = [jnp.concatenate([v_ref[0, pl.ds(start, blk), lanes(p)], ones], axis=1) for p in range(n_pairs)]
        ms, accs = list(ms), list(accs)
        s, p, alpha = [None] * n_sub, [None] * n_sub, [None] * n_sub

        def logits(i):
            head, off = divmod(i * sub, blk)
            si = _dot_nt(qs[tile(i)], kb[head // 2]) - ck_ref[0, j, head:head + 1, :]
            s[i] = jnp.where(cols <= rows + off, si, NEG_BIG) if masked else si

        def probs(i):
            m_new = jnp.maximum(ms[i], jnp.max(s[i], axis=-1, keepdims=True))
            p[i] = jnp.exp(s[i] - m_new).astype(BF16)
            alpha[i] = jnp.exp(ms[i] - m_new)
            ms[i] = m_new

        def values(i):
            accs[i] = alpha[i] * accs[i] + _dot(p[i], vb[head_of(i) // 2])

        _skewed(range(n_sub), [logits, probs, values])
        return tuple(ms), tuple(accs)

    def headroom(j, ms):
        room = None
        for i in range(n_sub):
            last = -ck_ref[0, j, head_of(i):head_of(i) + 1, :][:, blk - 1:blk]
            r = jnp.max(qk_bound[tile(i)] - ms[i] + last)
            room = r if room is None else jnp.maximum(room, r)
        return room

    init = (tuple(jnp.full((sub, 1), NEG_BIG, F32) for _ in range(n_sub)),
            tuple(jnp.zeros((sub, 2 * PAIR_WIDTH), F32) for _ in range(n_sub)))
    carry = step(qi, init, True)

    def more(state):
        i, room, _ = state
        return (i < qi) & (room > FOX_LOG_UNDERFLOW)

    def sweep(state):
        i, _, c = state
        c = step(qi - 1 - i, c, False)
        return i + 1, headroom(jnp.maximum(qi - 2 - i, 0), c[0]), c

    _, _, (_, accs) = lax.while_loop(more, sweep, (jnp.int32(0), headroom(jnp.maximum(qi - 1, 0), carry[0]), carry))
    acc = jnp.concatenate(accs, axis=0)
    out = acc[:, :PAIR_WIDTH] / acc[:, PAIR_WIDTH:PAIR_WIDTH + 1]
    o_ref[0] = jnp.concatenate([_unstack_heads(out[2 * p * blk:2 * (p + 1) * blk], blk) for p in range(n_pairs)],
                               axis=1)


def _fox_attention(q, k, v, cum_t):
    bsz, t, width = q.shape
    blk = min(FOX_BLOCK, t)
    nk = t // blk
    whole = pl.BlockSpec((1, t, width), lambda b, i: (b, 0, 0))
    qblk = pl.BlockSpec((1, blk, width), lambda b, i: (b, i, 0))
    return pl.pallas_call(
        functools.partial(_fox_kernel, sub=min(ATTN_SUBTILE, blk)),
        grid=(bsz, nk),
        in_specs=[qblk, whole, whole,
                  pl.BlockSpec((1, nk, 8, blk), lambda b, i: (b, 0, GATE_F0 // 8, 0))],
        out_specs=qblk,
        out_shape=jax.ShapeDtypeStruct((bsz, t, width), F32),
        scratch_shapes=[pltpu.VMEM((width // PAIR_WIDTH, 1, 1), F32)],
        compiler_params=_params(("arbitrary", "arbitrary")),
        name="fox_attention",
    )(q, k, v, cum_t)


def _sb_kernel(q_ref, k_ref, v_ref, o_ref, *, tk, sub):
    tq = q_ref.shape[1]
    n_pairs = q_ref.shape[2] // PAIR_WIDTH
    per = tq // tk
    qi = pl.program_id(1)
    lanes = lambda p: slice(p * PAIR_WIDTH, (p + 1) * PAIR_WIDTH)
    qs = jnp.concatenate([_stack_heads(q_ref[0, :, lanes(p)]) for p in range(n_pairs)], axis=0)
    n_sub = 2 * n_pairs * tq // sub
    pair_of = lambda i: (i * sub) // (2 * tq)
    rows = lax.broadcasted_iota(jnp.int32, (sub, tk), 0)
    cols = lax.broadcasted_iota(jnp.int32, (sub, tk), 1)
    suffix = _causal_masks(tk)[0].astype(BF16)

    def step(j, carry, diag):
        rests, accs = carry
        start = pl.multiple_of(j * tk, tk)
        kb = [k_ref[0, pl.ds(start, tk), lanes(p)] for p in range(n_pairs)]
        vb = [v_ref[0, pl.ds(start, tk), lanes(p)] for p in range(n_pairs)]
        rests, accs = list(rests), list(accs)
        z, split, w, strict = [None] * n_sub, [None] * n_sub, [None] * n_sub, [None] * n_sub

        def logits(i):
            z[i] = _dot_nt(qs[i * sub:(i + 1) * sub], kb[pair_of(i)])
            if diag is not None:
                strict[i] = cols + diag < rows + (i * sub) % tq

        def keep(i):
            log_keep = -_softplus(z[i])
            if diag is not None:
                log_keep = jnp.where(strict[i], log_keep, 0.0)
            split[i] = jnp.concatenate(_split_bf16(log_keep), axis=0)

        def weights(i):
            tails = _dot(split[i], suffix)
            tail = tails[:sub] + tails[sub:]
            wi = jnp.exp(z[i] + tail + rests[i])
            if diag is not None:
                wi = jnp.where(strict[i], wi, 0.0)
            w[i] = wi.astype(BF16)
            rests[i] = rests[i] + tail[:, 0:1]

        def values(i):
            accs[i] = accs[i] + _dot(w[i], vb[pair_of(i)])

        live = [i for i in range(n_sub) if diag is None or diag < (i * sub) % tq + sub - 1]
        _skewed(live, [logits, keep, weights, values])
        return tuple(rests), tuple(accs)

    carry = (tuple(jnp.zeros((sub, 1), F32) for _ in range(n_sub)),
             tuple(jnp.zeros((sub, PAIR_WIDTH), F32) for _ in range(n_sub)))
    for d in reversed(range(per)):
        carry = step(qi * per + d, carry, d * tk)

    def largest(rests):
        return functools.reduce(jnp.maximum, [jnp.max(r) for r in rests])

    def more(state):
        i, top, _ = state
        return (i < qi * per) & (top > SB_LOG_UNDERFLOW)

    def sweep(state):
        i, _, c = state
        c = step(qi * per - 1 - i, c, None)
        return i + 1, largest(c[0]), c

    _, _, (_, accs) = lax.while_loop(more, sweep, (jnp.int32(0), largest(carry[0]), carry))
    acc = jnp.concatenate(accs, axis=0)
    o_ref[0] = jnp.concatenate([_unstack_heads(acc[2 * p * tq:2 * (p + 1) * tq], tq) for p in range(n_pairs)],
                               axis=1)


def _sb_attention(q, k, v):
    bsz, t, width = q.shape
    tq = min(SB_Q_BLOCK, t)
    tk = min(SB_K_BLOCK, tq)
    whole = pl.BlockSpec((1, t, width), lambda b, i: (b, 0, 0))
    qblk = pl.BlockSpec((1, tq, width), lambda b, i: (b, i, 0))
    return pl.pallas_call(
        functools.partial(_sb_kernel, tk=tk, sub=min(ATTN_SUBTILE, tq)),
        grid=(bsz, t // tq),
        in_specs=[qblk, whole, whole],
        out_specs=qblk,
        out_shape=jax.ShapeDtypeStruct((bsz, t, width), F32),
        compiler_params=_params(("parallel", "arbitrary")),
        name="sb_attention",
    )(q, k, v)


CONV_HALO = 32


def _conv_kernel(prev_ref, cur_ref, w_ref, b_ref, ng_ref, nb_ref, o_ref, xs_ref, sh_ref):
    tb = cur_ref.shape[1]
    first = pl.program_id(1) == 0
    xs_ref[0:CONV_HALO, :] = jnp.where(first, 0.0, prev_ref[0])
    xs_ref[CONV_HALO:CONV_HALO + tb, :] = cur_ref[0]
    span = tb + CONV_HALO - SUBLANES
    for r in range(1, SUBLANES):
        sh_ref[r, 0:span, :] = xs_ref[r:r + span, :]
    base = CONV_HALO - (CONF_KERNEL - 1)
    acc = jnp.zeros((tb, GROUP_WIDTH), F32) + b_ref[...]
    for tap in range(CONF_KERNEL):
        whole, r = divmod(base + tap, SUBLANES)
        src = xs_ref if r == 0 else sh_ref.at[r]
        acc = acc + w_ref[tap:tap + 1, :] * src[whole * SUBLANES:whole * SUBLANES + tb, :]
    gsz = GROUP_WIDTH // CONF_GROUPS
    parts = []
    for g in range(CONF_GROUPS):
        cg = acc[:, g * gsz:(g + 1) * gsz]
        mu = jnp.mean(cg, axis=-1, keepdims=True)
        d = cg - mu
        var = jnp.mean(d * d, axis=-1, keepdims=True)
        parts.append(d * lax.rsqrt(var + LN_EPS))
    hn = jnp.concatenate(parts, axis=-1) * ng_ref[...] + nb_ref[...]
    o_ref[0] = _silu(hn)


def _conv_module(glu, layer, w, b, ng, nb):
    bsz, t, c = glu.shape
    tb = min(CONV_BLOCK, t)
    per = tb // CONV_HALO
    return pl.pallas_call(
        _conv_kernel,
        grid=(bsz, t // tb),
        in_specs=[pl.BlockSpec((1, CONV_HALO, c), lambda i, j: (i, jnp.maximum(j * per - 1, 0), 0)),
                  pl.BlockSpec((1, tb, c), lambda i, j: (i, j, 0)),
                  ] + [_layer_slab(a, layer) for a in (w, b, ng, nb)],
        out_specs=pl.BlockSpec((1, tb, c), lambda i, j: (i, j, 0)),
        out_shape=jax.ShapeDtypeStruct(glu.shape, F32),
        scratch_shapes=[pltpu.VMEM((CONV_HALO + tb, c), F32), pltpu.VMEM((SUBLANES, CONV_HALO + tb, c), F32)],
        compiler_params=_params(("parallel", "parallel")),
        name="conv_module",
    )(glu, glu, w, b, ng, nb)


GDN_HALO = 8


def _split_bf16(x):
    hi = x.astype(BF16)
    return hi, (x - hi.astype(F32)).astype(BF16)


def _group_sum(x, ones_bd):
    rows = x.shape[0]
    hi, lo = _split_bf16(x)
    r = _dot(jnp.concatenate([hi, lo], axis=0), ones_bd)
    return r[:rows] + r[rows:]


def _odd_blocks(x, size):
    return jnp.concatenate([x[r:r + size] for r in range(size, x.shape[0], 2 * size)], axis=0)


def _spread_odd_blocks(y, size, fill):
    parts = []
    for b in range(y.shape[0] // size):
        parts += [fill[2 * b * size:(2 * b + 1) * size], y[b * size:(b + 1) * size]]
    return jnp.concatenate(parts, axis=0)


def _unit_lower_inverses(lows, rows, cols):
    n = lows[0].shape[0]
    eye = jnp.where(rows == cols, 1.0, 0.0)
    first = (rows % 2 == 1) & (cols == rows - 1)
    invs = [eye - jnp.where(first, low, 0.0) for low in lows]
    size = 2
    while size < GDN_CHUNK:
        rb = rows // size
        level = (rb % 2 == 1) & (cols // size == rb - 1)
        odd_only = size % SUBLANES == 0
        pick = (lambda x: _odd_blocks(x, size)) if odd_only else (lambda x: x)
        splits = [_split_bf16(inv) for inv in invs]
        xs = []
        for low, (d_hi, d_lo) in zip(lows, splits):
            x2 = _dot(pick(jnp.where(level, low, 0.0)).astype(BF16), jnp.concatenate([d_hi, d_lo], axis=1))
            x = x2[:, :n] + x2[:, n:]
            if odd_only:
                x = _spread_odd_blocks(x, size, jnp.zeros((n, n), F32))
            xs.append(_split_bf16(x))
        nxt = []
        for inv, (d_hi, d_lo), (x_hi, x_lo) in zip(invs, splits, xs):
            m = n // 2 if odd_only else n
            lhs = jnp.concatenate([pick(inv).astype(BF16), pick(inv - d_hi.astype(F32)).astype(BF16)], axis=0) \
                if odd_only else jnp.concatenate([d_hi, d_lo], axis=0)
            y4 = _dot(lhs, jnp.concatenate([x_hi, x_lo], axis=1))
            y = y4[:m, :n] + y4[:m, n:] + y4[m:, :n] + y4[m:, n:]
            nxt.append(_spread_odd_blocks(pick(inv) - y, size, inv) if odd_only else inv - y)
        invs = nxt
        size *= 2
    return invs


def _gdn_kernel(qp_ref, kp_ref, vp_ref, q_ref, k_ref, v_ref, z_ref, gate_ref, cw_ref, alog_ref, dtb_ref,
                ng_ref, ones_ref, o_ref, xs_ref, y_ref, state_ref):
    tb = q_ref.shape[1]
    c = GDN_CHUNK
    n = 2 * c
    gw = GROUP_WIDTH
    first = pl.program_id(1) == 0

    @pl.when(first)
    def _():
        state_ref[...] = jnp.zeros_like(state_ref)

    ones_bd = ones_ref[...]

    base = GDN_HALO - (SHORT_CONV - 1)

    def conv_silu(idx, p_ref, c_ref):
        xs_ref[0:GDN_HALO, :] = jnp.where(first, 0.0, p_ref[0])
        xs_ref[GDN_HALO:GDN_HALO + tb, :] = c_ref[0]
        acc = None
        for tap in range(SHORT_CONV):
            term = cw_ref[tap:tap + 1, idx * gw:(idx + 1) * gw] * xs_ref[base + tap:base + tap + tb, :]
            acc = term if acc is None else acc + term
        return _silu(acc)

    def l2_normalize(y):
        return y * lax.rsqrt(_group_sum(y * y, ones_bd) + L2_EPS)

    qn = l2_normalize(conv_silu(0, qp_ref, q_ref)) * (HEAD_DIM ** -0.5)
    kn = l2_normalize(conv_silu(1, kp_ref, k_ref))
    vn = conv_silu(2, vp_ref, v_ref)

    gates = gate_ref[0]
    log_decay = -jnp.exp(alog_ref[...]) * _softplus(gates + dtb_ref[...])
    beta_all = _sigmoid(gates)
    rows_t = lax.broadcasted_iota(jnp.int32, (tb, tb), 0)
    cols_t = lax.broadcasted_iota(jnp.int32, (tb, tb), 1)
    chunk_tri = (cols_t <= rows_t) & (cols_t // c == rows_t // c)
    gcum = _mask_dot(chunk_tri, log_decay)
    lane_bcast = lambda x, lane: jnp.broadcast_to(x[:, lane:lane + 1], (tb, PAIR_WIDTH))
    g_wide = [lane_bcast(gcum, GATE_A0 + h) for h in range(GROUP_HEADS)]
    b_wide = [lane_bcast(beta_all, GATE_B0 + h) for h in range(GROUP_HEADS)]
    low_half = lax.broadcasted_iota(jnp.int32, (tb, PAIR_WIDTH), 1) < HEAD_DIM

    rows = lax.broadcasted_iota(jnp.int32, (n, n), 0)
    cols = lax.broadcasted_iota(jnp.int32, (n, n), 1)
    same_head = rows // c == cols // c
    lower_incl = same_head & (cols <= rows)
    strict_lower = same_head & (cols < rows)

    pairs = []
    for p in range(GROUP_HEADS // 2):
        ls = slice(p * PAIR_WIDTH, (p + 1) * PAIR_WIDTH)
        g_nat = jnp.where(low_half, g_wide[2 * p], g_wide[2 * p + 1])
        b_nat = jnp.where(low_half, b_wide[2 * p], b_wide[2 * p + 1])
        eg = jnp.exp(g_nat)
        k_beta = kn[:, ls] * b_nat
        pairs.append(dict(ls=ls, g_nat=g_nat, q=qn[:, ls], k=kn[:, ls], q_dec=qn[:, ls] * eg, k_beta=k_beta,
                          v_beta=vn[:, ls] * b_nat, kb_eg=k_beta * eg))

    n_chunks = tb // c
    systems = [(ci, p) for ci in range(n_chunks) for p in range(len(pairs))]
    chunk_rows = lambda ci: slice(ci * c, (ci + 1) * c)

    lkks, a_qks = [], []
    for ci, p in systems:
        d, r = pairs[p], chunk_rows(ci)
        g_col = jnp.concatenate([g_wide[2 * p][r], g_wide[2 * p + 1][r]], axis=0)
        decay = jnp.exp(jnp.where(lower_incl, g_col - g_col.T, -jnp.inf))
        k_st = _stack_heads(d["k"][r]).astype(BF16)
        lhs = jnp.concatenate([_stack_heads(d["k_beta"][r]), _stack_heads(d["q"][r])], axis=0).astype(BF16)
        gram = _dot_nt(lhs, k_st)
        lkks.append(jnp.where(strict_lower, gram[:n] * decay, 0.0))
        a_qks.append((gram[n:] * decay).astype(BF16))
    t_invs = _unit_lower_inverses(lkks, rows, cols)
    uws = []
    for (ci, p), t_inv in zip(systems, t_invs):
        d, r = pairs[p], chunk_rows(ci)
        rhs = jnp.concatenate([_stack_heads(d["v_beta"][r]), _stack_heads(d["kb_eg"][r])], axis=1)
        uws.append(_dot(t_inv.astype(BF16), rhs.astype(BF16)))

    states = [state_ref[p] for p in range(len(pairs))]
    for idx, (ci, p) in enumerate(systems):
        d, r = pairs[p], chunk_rows(ci)
        uw, state = uws[idx], states[p]
        g_last = d["g_nat"][ci * c + c - 1:ci * c + c, :]
        k_dec = _stack_heads(d["k"][r] * jnp.exp(g_last - d["g_nat"][r])).astype(BF16)
        wq = _dot(jnp.concatenate([uw[:, n:].astype(BF16), _stack_heads(d["q_dec"][r]).astype(BF16)], axis=0),
                  state.astype(BF16))
        vnb = (uw[:, :n] - wq[:n]).astype(BF16)
        o_st = wq[n:] + _dot(a_qks[idx], vnb)
        states[p] = state * jnp.exp(g_last) + _dot_tn(k_dec, vnb)
        y_ref[r, d["ls"]] = o_st[:c] + o_st[c:]
    for p in range(len(pairs)):
        state_ref[p] = states[p]

    o = y_ref[...]
    mean_sq = _group_sum(o * o, ones_bd) * (1.0 / HEAD_DIM)
    o_ref[0] = o * lax.rsqrt(mean_sq + RMS_EPS) * ng_ref[...] * _silu(z_ref[0])


def _gdn_mixer(q, k, v, z, gates, layer, conv_w, a_log_rows, dt_bias_rows, norm_g_rows, head_ones):
    bsz, t, width = q.shape
    nh = GROUP_HEADS
    tb = min(GDN_BLOCK, t)
    per = tb // GDN_HALO
    cur = pl.BlockSpec((1, tb, width), lambda i, j: (i, j, 0))
    prev = pl.BlockSpec((1, GDN_HALO, width), lambda i, j: (i, jnp.maximum(j * per - 1, 0), 0))
    return pl.pallas_call(
        _gdn_kernel,
        grid=(bsz, t // tb),
        in_specs=[prev, prev, prev, cur, cur, cur, cur,
                  pl.BlockSpec((1, tb, GATE_LANES), lambda i, j: (i, j, 0)),
                  ] + [_layer_slab(a, layer) for a in (conv_w, a_log_rows, dt_bias_rows, norm_g_rows)]
                 + [_resident((width, width))],
        out_specs=cur,
        out_shape=jax.ShapeDtypeStruct(q.shape, F32),
        scratch_shapes=[pltpu.VMEM((GDN_HALO + tb, width), F32), pltpu.VMEM((tb, width), F32),
                        pltpu.VMEM((nh // 2, PAIR_WIDTH, PAIR_WIDTH), F32)],
        compiler_params=_params(("parallel", "arbitrary")),
        name="gdn_mixer",
    )(q, k, v, q, k, v, z, gates, conv_w, a_log_rows, dt_bias_rows, norm_g_rows, head_ones)


def _memkv_kernel(m_ref, w_ref, o_ref):
    o_ref[0] = _dot(m_ref[0].astype(BF16), w_ref[...]).astype(o_ref.dtype)


def _memory_kv(mem, layer, w_kv):
    bsz, m, d = mem.shape
    return pl.pallas_call(
        _memkv_kernel,
        grid=(bsz,),
        in_specs=[pl.BlockSpec((1, m, d), lambda i: (i, 0, 0)), _layer_slab(w_kv, layer)],
        out_specs=pl.BlockSpec((1, m, 2 * d), lambda i: (i, 0, 0)),
        out_shape=jax.ShapeDtypeStruct((bsz, m, 2 * d), BF16),
        compiler_params=_params(("parallel",)),
        name="memory_kv",
    )(mem, w_kv)


def _mix_mem_kernel(x_ref, ya_ref, yb_ref, yc_ref, yd_ref, kv_ref, wout_ref, g1_ref, b1_ref, wq_ref, wo_ref,
                    g2_ref, b2_ref, o_ref, *, sub):
    gw = GROUP_WIDTH
    hd = MEM_HEAD_DIM
    n_sub = x_ref.shape[1] // sub
    tile = lambda i: slice(i * sub, (i + 1) * sub)
    x1, att = [None] * n_sub, [None] * n_sub

    def mix(i):
        tot = None
        for group, y_ref in enumerate((ya_ref, yb_ref, yc_ref, yd_ref)):
            part = _dot(y_ref[0, tile(i), :].astype(BF16), wout_ref[group * gw:(group + 1) * gw, :])
            tot = part if tot is None else tot + part
        x1[i] = _layer_norm(DN_ALPHA * x_ref[0, tile(i), :] + tot, g1_ref[...], b1_ref[...])

    def attend(i):
        q = _dot(x1[i].astype(BF16), wq_ref[...])
        outs = []
        for h in range(MEM_HEADS):
            qh = (q[:, h * hd:(h + 1) * hd] * (hd ** -0.5)).astype(BF16)
            kh = kv_ref[0, :, h * hd:(h + 1) * hd]
            vh = kv_ref[0, :, D_MODEL + h * hd:D_MODEL + (h + 1) * hd]
            s = _dot_nt(qh, kh)
            p = jnp.exp(s - jnp.max(s, axis=-1, keepdims=True))
            p = p / jnp.sum(p, axis=-1, keepdims=True)
            outs.append(_dot(p.astype(BF16), vh).astype(BF16))
        att[i] = jnp.concatenate(outs, axis=-1)

    def project(i):
        y = _dot(att[i], wo_ref[...])
        o_ref[0, tile(i), :] = _layer_norm(DN_ALPHA * x1[i] + y, g2_ref[...], b2_ref[...])

    _skewed(range(n_sub), [mix, attend, project])


def _mix_and_memory_ln(x, ya, yb, yc, yd, kv, layer, w_out, g_mix, b_mix, wq, wo, g_mem, b_mem):
    bsz, t, d = x.shape
    tm = min(MIX_TILE, t)
    row = lambda c: pl.BlockSpec((1, tm, c), lambda i, j: (i, j, 0))
    return pl.pallas_call(
        functools.partial(_mix_mem_kernel, sub=min(ROW_SUBTILE, tm)),
        grid=(bsz, t // tm),
        in_specs=[row(d)] + [row(GROUP_WIDTH)] * 4 + [pl.BlockSpec((1, kv.shape[1], 2 * d), lambda i, j: (i, 0, 0))]
                 + [_layer_slab(a, layer) for a in (w_out, g_mix, b_mix, wq, wo, g_mem, b_mem)],
        out_specs=row(d),
        out_shape=jax.ShapeDtypeStruct(x.shape, F32),
        compiler_params=_params(("parallel", "parallel")),
        name="mix_and_memory_ln",
    )(x, ya, yb, yc, yd, kv, w_out, g_mix, b_mix, wq, wo, g_mem, b_mem)


def _combined_in_weights(w_in):
    gw, nh = GROUP_WIDTH, GROUP_HEADS
    o = 0
    gdn_qkv = w_in[..., o:o + 3 * gw]; o += 3 * gw
    gdn_z = w_in[..., o:o + gw]; o += gw
    gdn_a = w_in[..., o:o + nh]; o += nh
    gdn_b = w_in[..., o:o + nh]; o += nh
    fox_qkv = w_in[..., o:o + 3 * gw]; o += 3 * gw
    fox_f = w_in[..., o:o + nh]; o += nh
    conf = w_in[..., o:o + 2 * gw]; o += 2 * gw
    sb_qkv = w_in[..., o:o + 3 * gw]
    pad = jnp.zeros(w_in.shape[:-1] + (GATE_LANES - 3 * nh,), w_in.dtype)
    return jnp.concatenate([gdn_qkv, gdn_z, fox_qkv, conf, sb_qkv, gdn_a, gdn_b, fox_f, pad], axis=-1).astype(BF16)


def _gate_rows(vals, first_lane):
    depth, nh = vals.shape
    return jnp.pad(vals, ((0, 0), (first_lane, GATE_LANES - first_lane - nh))).reshape(depth, 1, GATE_LANES)


def kernel(x, mem, ffn1_w_gate, ffn1_w_up, ffn1_w_down, ln_ffn1_g, ln_ffn1_b, w_in, gdn_conv_w, gdn_a_log, gdn_dt_bias, gdn_norm_g, fox_b_f, conf_dw_w, conf_dw_b, conf_norm_g, conf_norm_b, w_out, ln_mix_g, ln_mix_b, mem_w_q, mem_w_kv, mem_w_o, ln_mem_g, ln_mem_b, ffn2_w_gate, ffn2_w_up, ffn2_w_down, ln_ffn2_g, ln_ffn2_b):
    bf = lambda a: a.astype(BF16)
    rows = lambda a: a.reshape(a.shape[0], 1, a.shape[1])
    ffn1 = (bf(ffn1_w_gate), bf(ffn1_w_up), bf(ffn1_w_down), rows(ln_ffn1_g), rows(ln_ffn1_b))
    ffn2 = (bf(ffn2_w_gate), bf(ffn2_w_up), bf(ffn2_w_down), rows(ln_ffn2_g), rows(ln_ffn2_b))
    w_cat = _combined_in_weights(w_in)
    gdn = (gdn_conv_w, _gate_rows(gdn_a_log, GATE_A0), _gate_rows(gdn_dt_bias, GATE_A0),
           rows(jnp.tile(gdn_norm_g, (1, GROUP_HEADS))),
           jnp.kron(jnp.eye(GROUP_HEADS, dtype=BF16), jnp.ones((HEAD_DIM, HEAD_DIM), BF16)))
    fox_bias = _gate_rows(fox_b_f, GATE_F0)
    conf = (conf_dw_w, rows(conf_dw_b), rows(conf_norm_g), rows(conf_norm_b))
    mix_mem = (bf(w_out), rows(ln_mix_g), rows(ln_mix_b), bf(mem_w_q), bf(mem_w_o), rows(ln_mem_g), rows(ln_mem_b))
    w_kv = bf(mem_w_kv)

    for i in range(DEPTH):
        x = _ffn_ln(x, i, *ffn1)
        gq, gk, gv, gz, fq, fk, fv, glu, sq, sk, sv, gates = _in_projection(x, i, w_cat)
        y_a = _gdn_mixer(gq, gk, gv, gz, gates, i, *gdn)
        y_b = _fox_attention(fq, fk, fv, _fox_cumsum(gates, i, fox_bias))
        y_c = _conv_module(glu, i, *conf)
        y_d = _sb_attention(sq, sk, sv)
        x = _mix_and_memory_ln(x, y_a, y_b, y_c, y_d, _memory_kv(mem, i, w_kv), i, *mix_mem)
        x = _ffn_ln(x, i, *ffn2)
    return x
```

```python
import functools
import math

import jax
import jax.numpy as jnp
from jax import lax
from jax.experimental import pallas as pl
from jax.experimental.pallas import tpu as pltpu

F32 = jnp.float32
BF16 = jnp.bfloat16

D_MODEL = 1024
DEPTH = 2
GROUP_WIDTH = 256
HEAD_DIM = 64
GROUP_HEADS = 4
D_FF = 2816
SHORT_CONV = 4
CONF_KERNEL = 31
CONF_GROUPS = 4
GDN_CHUNK = 64
N_MEM = 256
MEM_HEADS = 4
MEM_HEAD_DIM = D_MODEL // MEM_HEADS
DN_ALPHA = float((2 * DEPTH) ** 0.25)
LN_EPS = 1e-5
RMS_EPS = 1e-6
L2_EPS = 1e-6
NEG_BIG = -1e30

GATE_LANES = 128
GATE_A0, GATE_B0, GATE_F0 = 0, 4, 8

TOKEN_TILE = 512
MXU_WIDTH = 256
SUBLANES = 8
FFN_SPLITS = (0, 6 * MXU_WIDTH, D_FF)
PAIR_WIDTH = 2 * HEAD_DIM
FOX_BLOCK = 512
SB_Q_BLOCK = 512
SB_K_BLOCK = 256
ATTN_SUBTILE = 256
MIX_TILE = 1024
ROW_SUBTILE = 512
SB_LOG_UNDERFLOW = -105.0
FOX_LOG_UNDERFLOW = -106.0
GDN_BLOCK = 256
CONV_BLOCK = 512
VMEM_LIMIT = 56 * 1024 * 1024


def _params(sem, vmem=VMEM_LIMIT):
    return pltpu.CompilerParams(dimension_semantics=sem, vmem_limit_bytes=vmem)


def _resident(shape):
    nd = len(shape)
    return pl.BlockSpec(shape, lambda *_: (0,) * nd, pipeline_mode=pl.Buffered(1))


def _layer_slab(stacked, layer):
    tail = stacked.shape[1:]
    return pl.BlockSpec((None,) + tail, lambda *_: (layer,) + (0,) * len(tail), pipeline_mode=pl.Buffered(1))


def _layer_norm(y, g, b):
    mu = jnp.mean(y, axis=-1, keepdims=True)
    d = y - mu
    var = jnp.mean(d * d, axis=-1, keepdims=True)
    return d * lax.rsqrt(var + LN_EPS) * g + b


def _sigmoid(x):
    return 1.0 / (1.0 + jnp.exp(-x))


def _silu(x):
    return x * _sigmoid(x)


def _softplus(x):
    return jnp.maximum(x, 0.0) + jnp.log(1.0 + jnp.exp(-jnp.abs(x)))


def _dot(a, b):
    return jnp.dot(a, b, preferred_element_type=F32)


def _dot_nt(a, b):
    return lax.dot_general(a, b, (((1,), (1,)), ((), ())), preferred_element_type=F32)


def _dot_tn(a, b):
    return lax.dot_general(a, b, (((0,), (0,)), ((), ())), preferred_element_type=F32)


def _mask_dot(mask, x):
    n = x.shape[1]
    hi = x.astype(BF16)
    r1 = x - hi.astype(F32)
    mid = r1.astype(BF16)
    lo = (r1 - mid.astype(F32)).astype(BF16)
    r = _dot(mask.astype(BF16), jnp.concatenate([hi, mid, lo], axis=1))
    return r[:, :n] + r[:, n:2 * n] + r[:, 2 * n:]


def _ffn_kernel(x_ref, wg_ref, wu_ref, wd_ref, g_ref, b_ref, o_ref):
    x = x_ref[0]
    xb = x.astype(BF16)
    acc = None
    for lo, hi in zip(FFN_SPLITS[:-1], FFN_SPLITS[1:]):
        h = _dot(xb, wg_ref[:, lo:hi])
        u = _dot(xb, wu_ref[:, lo:hi])
        a = (_silu(h) * u).astype(BF16)
        part = _dot(a, wd_ref[lo:hi, :])
        acc = part if acc is None else acc + part
    o_ref[0] = _layer_norm(DN_ALPHA * x + 0.5 * acc, g_ref[...], b_ref[...])


def _ffn_ln(x, layer, wg, wu, wd, g, b):
    bsz, t, d = x.shape
    tm = min(TOKEN_TILE, t)
    row = pl.BlockSpec((1, tm, d), lambda i, j: (i, j, 0))
    return pl.pallas_call(
        _ffn_kernel,
        grid=(bsz, t // tm),
        in_specs=[row] + [_layer_slab(a, layer) for a in (wg, wu, wd, g, b)],
        out_specs=row,
        out_shape=jax.ShapeDtypeStruct(x.shape, F32),
        compiler_params=_params(("parallel", "parallel")),
        name="ffn_ln",
    )(x, wg, wu, wd, g, b)


def _inproj_kernel(x_ref, w_ref, gq_ref, gk_ref, gv_ref, gz_ref, fq_ref, fk_ref, fv_ref,
                   glu_ref, sq_ref, sk_ref, sv_ref, gate_ref):
    xb = x_ref[0].astype(BF16)
    gw = GROUP_WIDTH

    def group(i, width=gw):
        return _dot(xb, w_ref[:, i * gw:i * gw + width])

    qk_scale = HEAD_DIM ** -0.5
    gq_ref[0] = group(0)
    gk_ref[0] = group(1)
    gv_ref[0] = group(2)
    gz_ref[0] = group(3)
    fq_ref[0] = (group(4) * qk_scale).astype(BF16)
    fk_ref[0] = group(5).astype(BF16)
    fv_ref[0] = group(6).astype(BF16)
    glu_ref[0] = group(7) * _sigmoid(group(8))
    sq_ref[0] = (group(9) * qk_scale).astype(BF16)
    sk_ref[0] = group(10).astype(BF16)
    sv_ref[0] = group(11).astype(BF16)
    gate_ref[0] = group(12, GATE_LANES)


def _in_projection(x, layer, w_cat):
    bsz, t, d = x.shape
    tm = min(TOKEN_TILE, t)
    row = lambda c: pl.BlockSpec((1, tm, c), lambda i, j: (i, j, 0))
    wide = lambda dt: jax.ShapeDtypeStruct((bsz, t, GROUP_WIDTH), dt)
    out_shape = ([wide(F32)] * 4 + [wide(BF16)] * 3 + [wide(F32)] + [wide(BF16)] * 3
                 + [jax.ShapeDtypeStruct((bsz, t, GATE_LANES), F32)])
    out_specs = [row(GROUP_WIDTH)] * 11 + [row(GATE_LANES)]
    return pl.pallas_call(
        _inproj_kernel,
        grid=(bsz, t // tm),
        in_specs=[row(d), _layer_slab(w_cat, layer)],
        out_specs=out_specs,
        out_shape=out_shape,
        compiler_params=_params(("parallel", "parallel")),
        name="in_projection",
    )(x, w_cat)


def _fox_cum_kernel(gate_ref, bias_ref, o_ref, carry_ref):
    @pl.when(pl.program_id(1) == 0)
    def _():
        carry_ref[...] = jnp.zeros_like(carry_ref)

    tb = gate_ref.shape[1]
    logit = gate_ref[0] + bias_ref[...]
    log_f = -_softplus(-logit)
    rows = lax.broadcasted_iota(jnp.int32, (tb, tb), 0)
    cols = lax.broadcasted_iota(jnp.int32, (tb, tb), 1)
    cum = _mask_dot(cols <= rows, log_f) + carry_ref[...]
    carry_ref[...] = cum[tb - 1:tb, :]
    o_ref[0, 0] = cum.T


def _fox_cumsum(gates, layer, bias_rows):
    bsz, t, _ = gates.shape
    tb = min(FOX_BLOCK, t)
    return pl.pallas_call(
        _fox_cum_kernel,
        grid=(bsz, t // tb),
        in_specs=[pl.BlockSpec((1, tb, GATE_LANES), lambda i, j: (i, j, 0)), _layer_slab(bias_rows, layer)],
        out_specs=pl.BlockSpec((1, 1, GATE_LANES, tb), lambda i, j: (i, j, 0, 0)),
        out_shape=jax.ShapeDtypeStruct((bsz, t // tb, GATE_LANES, tb), F32),
        scratch_shapes=[pltpu.VMEM((1, GATE_LANES), F32)],
        compiler_params=_params(("parallel", "arbitrary")),
        name="fox_cumsum",
    )(gates, bias_rows)


def _causal_masks(blk):
    rows = lax.broadcasted_iota(jnp.int32, (blk, blk), 0)
    cols = lax.broadcasted_iota(jnp.int32, (blk, blk), 1)
    return cols <= rows, cols < rows


def _stack_heads(x2):
    lane = lax.broadcasted_iota(jnp.int32, x2.shape, 1)
    zero = jnp.zeros_like(x2)
    return jnp.concatenate([jnp.where(lane < HEAD_DIM, x2, zero), jnp.where(lane >= HEAD_DIM, x2, zero)], axis=0)


def _unstack_heads(y, rows):
    lane = lax.broadcasted_iota(jnp.int32, (rows, PAIR_WIDTH), 1)
    return jnp.where(lane < HEAD_DIM, y[:rows], y[rows:])


def _skewed(tiles, stages):
    tiles = list(tiles)
    for t in range(len(tiles) + len(stages) - 1):
        for k in reversed(range(len(stages))):
            if 0 <= t - k < len(tiles):
                stages[k](tiles[t - k])


def _fox_kernel(q_ref, k_ref, v_ref, ck_ref, o_ref, kmax_ref, *, sub):
    blk = q_ref.shape[1]
    n_pairs = q_ref.shape[2] // PAIR_WIDTH
    qi = pl.program_id(1)
    lanes = lambda p: slice(p * PAIR_WIDTH, (p + 1) * PAIR_WIDTH)

    @pl.when(qi == 0)
    def _():
        for p in range(n_pairs):
            def widest(c, best):
                kk = k_ref[0, pl.ds(pl.multiple_of(c * blk, blk), blk), lanes(p)].astype(F32)
                return jnp.maximum(best, jnp.sum(kk * kk, axis=-1, keepdims=True))
            best = lax.fori_loop(0, k_ref.shape[1] // blk, widest, jnp.zeros((blk, 1), F32))
            kmax_ref[p] = jnp.sqrt(jnp.max(best, axis=0, keepdims=True))

    qs = jnp.concatenate([_stack_heads(q_ref[0, :, lanes(p)]) for p in range(n_pairs)], axis=0)
    n_sub = 2 * n_pairs * blk // sub
    tile = lambda i: slice(i * sub, (i + 1) * sub)
    head_of = lambda i: (i * sub) // blk
    qf = qs.astype(F32)
    qnorm = jnp.sqrt(jnp.sum(qf * qf, axis=-1, keepdims=True))
    qk_bound = jnp.concatenate([qnorm[2 * p * blk:2 * (p + 1) * blk] * kmax_ref[p] for p in range(n_pairs)],
                               axis=0)
    rows = lax.broadcasted_iota(jnp.int32, (sub, blk), 0)
    cols = lax.broadcasted_iota(jnp.int32, (sub, blk), 1)
    ones = jnp.ones((blk, PAIR_WIDTH), BF16)

    def step(j, carry, masked):
        ms, accs = carry
        start = pl.multiple_of(j * blk, blk)
        kb = [k_ref[0, pl.ds(start, blk), lanes(p)] for p in range(n_pairs)]
        vb = [jnp.concatenate([v_ref[0, pl.ds(start, blk), lanes(p)], ones], axis=1) for p in range(n_pairs)]
        ms, accs = list(ms), list(accs)
        s, p, alpha = [None] * n_sub, [None] * n_sub, [None] * n_sub

        def logits(i):
            head, off = divmod(i * sub, blk)
            si = _dot_nt(qs[tile(i)], kb[head // 2]) - ck_ref[0, j, head:head + 1, :]
            s[i] = jnp.where(cols <= rows + off, si, NEG_BIG) if masked else si

        def probs(i):
            m_new = jnp.maximum(ms[i], jnp.max(s[i], axis=-1, keepdims=True))
            p[i] = jnp.exp(s[i] - m_new).astype(BF16)
            alpha[i] = jnp.exp(ms[i] - m_new)
            ms[i] = m_new

        def values(i):
            accs[i] = alpha[i] * accs[i] + _dot(p[i], vb[head_of(i) // 2])

        _skewed(range(n_sub), [logits, probs, values])
        return tuple(ms), tuple(accs)

    def headroom(j, ms):
        room = None
        for i in range(n_sub):
            last = -ck_ref[0, j, head_of(i):head_of(i) + 1, :][:, blk - 1:blk]
            r = jnp.max(qk_bound[tile(i)] - ms[i] + last)
            room = r if room is None else jnp.maximum(room, r)
        return room

    init = (tuple(jnp.full((sub, 1), NEG_BIG, F32) for _ in range(n_sub)),
            tuple(jnp.zeros((sub, 2 * PAIR_WIDTH), F32) for _ in range(n_sub)))
    carry = step(qi, init, True)

    def more(state):
        i, room, _ = state
        return (i < qi) & (room > FOX_LOG_UNDERFLOW)

    def sweep(state):
        i, _, c = state
        c = step(qi - 1 - i, c, False)
        return i + 1, headroom(jnp.maximum(qi - 2 - i, 0), c[0]), c

    _, _, (_, accs) = lax.while_loop(more, sweep, (jnp.int32(0), headroom(jnp.maximum(qi - 1, 0), carry[0]), carry))
    acc = jnp.concatenate(accs, axis=0)
    out = acc[:, :PAIR_WIDTH] / acc[:, PAIR_WIDTH:PAIR_WIDTH + 1]
    o_ref[0] = jnp.concatenate([_unstack_heads(out[2 * p * blk:2 * (p + 1) * blk], blk) for p in range(n_pairs)],
                               axis=1)


def _fox_attention(q, k, v, cum_t):
    bsz, t, width = q.shape
    blk = min(FOX_BLOCK, t)
    nk = t // blk
    whole = pl.BlockSpec((1, t, width), lambda b, i: (b, 0, 0))
    qblk = pl.BlockSpec((1, blk, width), lambda b, i: (b, i, 0))
    return pl.pallas_call(
        functools.partial(_fox_kernel, sub=min(ATTN_SUBTILE, blk)),
        grid=(bsz, nk),
        in_specs=[qblk, whole, whole,
                  pl.BlockSpec((1, nk, 8, blk), lambda b, i: (b, 0, GATE_F0 // 8, 0))],
        out_specs=qblk,
        out_shape=jax.ShapeDtypeStruct((bsz, t, width), F32),
        scratch_shapes=[pltpu.VMEM((width // PAIR_WIDTH, 1, 1), F32)],
        compiler_params=_params(("arbitrary", "arbitrary")),
        name="fox_attention",
    )(q, k, v, cum_t)


def _sb_kernel(q_ref, k_ref, v_ref, o_ref, *, tk, sub):
    tq = q_ref.shape[1]
    n_pairs = q_ref.shape[2] // PAIR_WIDTH
    per = tq // tk
    qi = pl.program_id(1)
    lanes = lambda p: slice(p * PAIR_WIDTH, (p + 1) * PAIR_WIDTH)
    qs = jnp.concatenate([_stack_heads(q_ref[0, :, lanes(p)]) for p in range(n_pairs)], axis=0)
    n_sub = 2 * n_pairs * tq // sub
    pair_of = lambda i: (i * sub) // (2 * tq)
    rows = lax.broadcasted_iota(jnp.int32, (sub, tk), 0)
    cols = lax.broadcasted_iota(jnp.int32, (sub, tk), 1)
    suffix = _causal_masks(tk)[0].astype(BF16)

    def step(j, carry, diag):
        rests, accs = carry
        start = pl.multiple_of(j * tk, tk)
        kb = [k_ref[0, pl.ds(start, tk), lanes(p)] for p in range(n_pairs)]
        vb = [v_ref[0, pl.ds(start, tk), lanes(p)] for p in range(n_pairs)]
        rests, accs = list(rests), list(accs)
        z, split, w, strict = [None] * n_sub, [None] * n_sub, [None] * n_sub, [None] * n_sub

        def logits(i):
            z[i] = _dot_nt(qs[i * sub:(i + 1) * sub], kb[pair_of(i)])
            if diag is not None:
                strict[i] = cols + diag < rows + (i * sub) % tq

        def keep(i):
            log_keep = -_softplus(z[i])
            if diag is not None:
                log_keep = jnp.where(strict[i], log_keep, 0.0)
            split[i] = jnp.concatenate(_split_bf16(log_keep), axis=0)

        def weights(i):
            tails = _dot(split[i], suffix)
            tail = tails[:sub] + tails[sub:]
            wi = jnp.exp(z[i] + tail + rests[i])
            if diag is not None:
                wi = jnp.where(strict[i], wi, 0.0)
            w[i] = wi.astype(BF16)
            rests[i] = rests[i] + tail[:, 0:1]

        def values(i):
            accs[i] = accs[i] + _dot(w[i], vb[pair_of(i)])

        live = [i for i in range(n_sub) if diag is None or diag < (i * sub) % tq + sub - 1]
        _skewed(live, [logits, keep, weights, values])
        return tuple(rests), tuple(accs)

    carry = (tuple(jnp.zeros((sub, 1), F32) for _ in range(n_sub)),
             tuple(jnp.zeros((sub, PAIR_WIDTH), F32) for _ in range(n_sub)))
    for d in reversed(range(per)):
        carry = step(qi * per + d, carry, d * tk)

    def largest(rests):
        return functools.reduce(jnp.maximum, [jnp.max(r) for r in rests])

    def more(state):
        i, top, _ = state
        return (i < qi * per) & (top > SB_LOG_UNDERFLOW)

    def sweep(state):
        i, _, c = state
        c = step(qi * per - 1 - i, c, None)
        return i + 1, largest(c[0]), c

    _, _, (_, accs) = lax.while_loop(more, sweep, (jnp.int32(0), largest(carry[0]), carry))
    acc = jnp.concatenate(accs, axis=0)
    o_ref[0] = jnp.concatenate([_unstack_heads(acc[2 * p * tq:2 * (p + 1) * tq], tq) for p in range(n_pairs)],
                               axis=1)


def _sb_attention(q, k, v):
    bsz, t, width = q.shape
    tq = min(SB_Q_BLOCK, t)
    tk = min(SB_K_BLOCK, tq)
    whole = pl.BlockSpec((1, t, width), lambda b, i: (b, 0, 0))
    qblk = pl.BlockSpec((1, tq, width), lambda b, i: (b, i, 0))
    return pl.pallas_call(
        functools.partial(_sb_kernel, tk=tk, sub=min(ATTN_SUBTILE, tq)),
        grid=(bsz, t // tq),
        in_specs=[qblk, whole, whole],
        out_specs=qblk,
        out_shape=jax.ShapeDtypeStruct((bsz, t, width), F32),
        compiler_params=_params(("parallel", "arbitrary")),
        name="sb_attention",
    )(q, k, v)


CONV_HALO = 32


def _conv_kernel(prev_ref, cur_ref, w_ref, b_ref, ng_ref, nb_ref, o_ref, xs_ref, sh_ref):
    tb = cur_ref.shape[1]
    first = pl.program_id(1) == 0
    xs_ref[0:CONV_HALO, :] = jnp.where(first, 0.0, prev_ref[0])
    xs_ref[CONV_HALO:CONV_HALO + tb, :] = cur_ref[0]
    span = tb + CONV_HALO - SUBLANES
    for r in range(1, SUBLANES):
        sh_ref[r, 0:span, :] = xs_ref[r:r + span, :]
    base = CONV_HALO - (CONF_KERNEL - 1)
    acc = jnp.zeros((tb, GROUP_WIDTH), F32) + b_ref[...]
    for tap in range(CONF_KERNEL):
        whole, r = divmod(base + tap, SUBLANES)
        src = xs_ref if r == 0 else sh_ref.at[r]
        acc = acc + w_ref[tap:tap + 1, :] * src[whole * SUBLANES:whole * SUBLANES + tb, :]
    gsz = GROUP_WIDTH // CONF_GROUPS
    parts = []
    for g in range(CONF_GROUPS):
        cg = acc[:, g * gsz:(g + 1) * gsz]
        mu = jnp.mean(cg, axis=-1, keepdims=True)
        d = cg - mu
        var = jnp.mean(d * d, axis=-1, keepdims=True)
        parts.append(d * lax.rsqrt(var + LN_EPS))
    hn = jnp.concatenate(parts, axis=-1) * ng_ref[...] + nb_ref[...]
    o_ref[0] = _silu(hn)


def _conv_module(glu, layer, w, b, ng, nb):
    bsz, t, c = glu.shape
    tb = min(CONV_BLOCK, t)
    per = tb // CONV_HALO
    return pl.pallas_call(
        _conv_kernel,
        grid=(bsz, t // tb),
        in_specs=[pl.BlockSpec((1, CONV_HALO, c), lambda i, j: (i, jnp.maximum(j * per - 1, 0), 0)),
                  pl.BlockSpec((1, tb, c), lambda i, j: (i, j, 0)),
                  ] + [_layer_slab(a, layer) for a in (w, b, ng, nb)],
        out_specs=pl.BlockSpec((1, tb, c), lambda i, j: (i, j, 0)),
        out_shape=jax.ShapeDtypeStruct(glu.shape, F32),
        scratch_shapes=[pltpu.VMEM((CONV_HALO + tb, c), F32), pltpu.VMEM((SUBLANES, CONV_HALO + tb, c), F32)],
        compiler_params=_params(("parallel", "parallel")),
        name="conv_module",
    )(glu, glu, w, b, ng, nb)


GDN_HALO = 8


def _split_bf16(x):
    hi = x.astype(BF16)
    return hi, (x - hi.astype(F32)).astype(BF16)


def _group_sum(x, ones_bd):
    rows = x.shape[0]
    hi, lo = _split_bf16(x)
    r = _dot(jnp.concatenate([hi, lo], axis=0), ones_bd)
    return r[:rows] + r[rows:]


def _odd_blocks(x, size):
    return jnp.concatenate([x[r:r + size] for r in range(size, x.shape[0], 2 * size)], axis=0)


def _spread_odd_blocks(y, size, fill):
    parts = []
    for b in range(y.shape[0] // size):
        parts += [fill[2 * b * size:(2 * b + 1) * size], y[b * size:(b + 1) * size]]
    return jnp.concatenate(parts, axis=0)


def _unit_lower_inverses(lows, rows, cols):
    n = lows[0].shape[0]
    eye = jnp.where(rows == cols, 1.0, 0.0)
    first = (rows % 2 == 1) & (cols == rows - 1)
    invs = [eye - jnp.where(first, low, 0.0) for low in lows]
    size = 2
    while size < GDN_CHUNK:
        rb = rows // size
        level = (rb % 2 == 1) & (cols // size == rb - 1)
        odd_only = size % SUBLANES == 0
        pick = (lambda x: _odd_blocks(x, size)) if odd_only else (lambda x: x)
        splits = [_split_bf16(inv) for inv in invs]
        xs = []
        for low, (d_hi, d_lo) in zip(lows, splits):
            x2 = _dot(pick(jnp.where(level, low, 0.0)).astype(BF16), jnp.concatenate([d_hi, d_lo], axis=1))
            x = x2[:, :n] + x2[:, n:]
            if odd_only:
                x = _spread_odd_blocks(x, size, jnp.zeros((n, n), F32))
            xs.append(_split_bf16(x))
        nxt = []
        for inv, (d_hi, d_lo), (x_hi, x_lo) in zip(invs, splits, xs):
            m = n // 2 if odd_only else n
            lhs = jnp.concatenate([pick(inv).astype(BF16), pick(inv - d_hi.astype(F32)).astype(BF16)], axis=0) \
                if odd_only else jnp.concatenate([d_hi, d_lo], axis=0)
            y4 = _dot(lhs, jnp.concatenate([x_hi, x_lo], axis=1))
            y = y4[:m, :n] + y4[:m, n:] + y4[m:, :n] + y4[m:, n:]
            nxt.append(_spread_odd_blocks(pick(inv) - y, size, inv) if odd_only else inv - y)
        invs = nxt
        size *= 2
    return invs


def _gdn_kernel(qp_ref, kp_ref, vp_ref, q_ref, k_ref, v_ref, z_ref, gate_ref, cw_ref, alog_ref, dtb_ref,
                ng_ref, ones_ref, o_ref, xs_ref, y_ref, state_ref):
    nb, tb = q_ref.shape[0], q_ref.shape[1]
    c = GDN_CHUNK
    n = 2 * c
    gw = GROUP_WIDTH
    n_pairs = GROUP_HEADS // 2
    first = pl.program_id(0) == 0

    @pl.when(first)
    def _():
        state_ref[...] = jnp.zeros_like(state_ref)

    ones_bd = ones_ref[...]
    base = GDN_HALO - (SHORT_CONV - 1)

    def conv_silu(idx, p_ref, c_ref, b):
        xs_ref[0:GDN_HALO, :] = jnp.where(first, 0.0, p_ref[b])
        xs_ref[GDN_HALO:GDN_HALO + tb, :] = c_ref[b]
        acc = None
        for tap in range(SHORT_CONV):
            term = cw_ref[tap:tap + 1, idx * gw:(idx + 1) * gw] * xs_ref[base + tap:base + tap + tb, :]
            acc = term if acc is None else acc + term
        return _silu(acc)

    def l2_normalize(y):
        return y * lax.rsqrt(_group_sum(y * y, ones_bd) + L2_EPS)

    rows_t = lax.broadcasted_iota(jnp.int32, (tb, tb), 0)
    cols_t = lax.broadcasted_iota(jnp.int32, (tb, tb), 1)
    chunk_tri = (cols_t <= rows_t) & (cols_t // c == rows_t // c)
    lane_bcast = lambda x, lane: jnp.broadcast_to(x[:, lane:lane + 1], (tb, PAIR_WIDTH))
    low_half = lax.broadcasted_iota(jnp.int32, (tb, PAIR_WIDTH), 1) < HEAD_DIM
    rows = lax.broadcasted_iota(jnp.int32, (n, n), 0)
    cols = lax.broadcasted_iota(jnp.int32, (n, n), 1)
    same_head = rows // c == cols // c
    lower_incl = same_head & (cols <= rows)
    strict_lower = same_head & (cols < rows)

    pairs, g_wides = {}, {}
    for b in range(nb):
        qn = l2_normalize(conv_silu(0, qp_ref, q_ref, b)) * (HEAD_DIM ** -0.5)
        kn = l2_normalize(conv_silu(1, kp_ref, k_ref, b))
        vn = conv_silu(2, vp_ref, v_ref, b)
        gates = gate_ref[b]
        log_decay = -jnp.exp(alog_ref[...]) * _softplus(gates + dtb_ref[...])
        beta_all = _sigmoid(gates)
        gcum = _mask_dot(chunk_tri, log_decay)
        g_wide = [lane_bcast(gcum, GATE_A0 + h) for h in range(GROUP_HEADS)]
        b_wide = [lane_bcast(beta_all, GATE_B0 + h) for h in range(GROUP_HEADS)]
        g_wides[b] = g_wide
        for p in range(n_pairs):
            ls = slice(p * PAIR_WIDTH, (p + 1) * PAIR_WIDTH)
            g_nat = jnp.where(low_half, g_wide[2 * p], g_wide[2 * p + 1])
            b_nat = jnp.where(low_half, b_wide[2 * p], b_wide[2 * p + 1])
            eg = jnp.exp(g_nat)
            k_beta = kn[:, ls] * b_nat
            pairs[b, p] = dict(ls=ls, g_nat=g_nat, q=qn[:, ls], k=kn[:, ls], q_dec=qn[:, ls] * eg, k_beta=k_beta,
                               v_beta=vn[:, ls] * b_nat, kb_eg=k_beta * eg)

    systems = [(ci, b, p) for ci in range(tb // c) for b in range(nb) for p in range(n_pairs)]
    chunk_rows = lambda ci: slice(ci * c, (ci + 1) * c)

    lkks, a_qks = [], []
    for ci, b, p in systems:
        d, r, g_wide = pairs[b, p], chunk_rows(ci), g_wides[b]
        g_col = jnp.concatenate([g_wide[2 * p][r], g_wide[2 * p + 1][r]], axis=0)
        decay = jnp.exp(jnp.where(lower_incl, g_col - g_col.T, -jnp.inf))
        k_st = _stack_heads(d["k"][r]).astype(BF16)
        lhs = jnp.concatenate([_stack_heads(d["k_beta"][r]), _stack_heads(d["q"][r])], axis=0).astype(BF16)
        gram = _dot_nt(lhs, k_st)
        lkks.append(jnp.where(strict_lower, gram[:n] * decay, 0.0))
        a_qks.append((gram[n:] * decay).astype(BF16))
    t_invs = _unit_lower_inverses(lkks, rows, cols)
    uws = []
    for (ci, b, p), t_inv in zip(systems, t_invs):
        d, r = pairs[b, p], chunk_rows(ci)
        rhs = jnp.concatenate([_stack_heads(d["v_beta"][r]), _stack_heads(d["kb_eg"][r])], axis=1)
        uws.append(_dot(t_inv.astype(BF16), rhs.astype(BF16)))

    states = {(b, p): state_ref[b * n_pairs + p] for b in range(nb) for p in range(n_pairs)}
    for idx, (ci, b, p) in enumerate(systems):
        d, r = pairs[b, p], chunk_rows(ci)
        uw, state = uws[idx], states[b, p]
        g_last = d["g_nat"][ci * c + c - 1:ci * c + c, :]
        k_dec = _stack_heads(d["k"][r] * jnp.exp(g_last - d["g_nat"][r])).astype(BF16)
        wq = _dot(jnp.concatenate([uw[:, n:].astype(BF16), _stack_heads(d["q_dec"][r]).astype(BF16)], axis=0),
                  state.astype(BF16))
        vnb = (uw[:, :n] - wq[:n]).astype(BF16)
        o_st = wq[n:] + _dot(a_qks[idx], vnb)
        states[b, p] = state * jnp.exp(g_last) + _dot_tn(k_dec, vnb)
        y_ref[b, r, d["ls"]] = o_st[:c] + o_st[c:]
    for (b, p), state in states.items():
        state_ref[b * n_pairs + p] = state

    for b in range(nb):
        o = y_ref[b]
        mean_sq = _group_sum(o * o, ones_bd) * (1.0 / HEAD_DIM)
        o_ref[b] = o * lax.rsqrt(mean_sq + RMS_EPS) * ng_ref[...] * _silu(z_ref[b])


def _gdn_mixer(q, k, v, z, gates, layer, conv_w, a_log_rows, dt_bias_rows, norm_g_rows, head_ones):
    bsz, t, width = q.shape
    nh = GROUP_HEADS
    tb = min(GDN_BLOCK, t)
    per = tb // GDN_HALO
    cur = pl.BlockSpec((bsz, tb, width), lambda j: (0, j, 0))
    prev = pl.BlockSpec((bsz, GDN_HALO, width), lambda j: (0, jnp.maximum(j * per - 1, 0), 0))
    return pl.pallas_call(
        _gdn_kernel,
        grid=(t // tb,),
        in_specs=[prev, prev, prev, cur, cur, cur, cur,
                  pl.BlockSpec((bsz, tb, GATE_LANES), lambda j: (0, j, 0)),
                  ] + [_layer_slab(a, layer) for a in (conv_w, a_log_rows, dt_bias_rows, norm_g_rows)]
                 + [_resident((width, width))],
        out_specs=cur,
        out_shape=jax.ShapeDtypeStruct(q.shape, F32),
        scratch_shapes=[pltpu.VMEM((GDN_HALO + tb, width), F32), pltpu.VMEM((bsz, tb, width), F32),
                        pltpu.VMEM((bsz * nh // 2, PAIR_WIDTH, PAIR_WIDTH), F32)],
        compiler_params=_params(("arbitrary",)),
        name="gdn_mixer",
    )(q, k, v, q, k, v, z, gates, conv_w, a_log_rows, dt_bias_rows, norm_g_rows, head_ones)


def _memkv_kernel(m_ref, w_ref, o_ref):
    o_ref[0] = _dot(m_ref[0].astype(BF16), w_ref[...]).astype(o_ref.dtype)


def _memory_kv(mem, layer, w_kv):
    bsz, m, d = mem.shape
    return pl.pallas_call(
        _memkv_kernel,
        grid=(bsz,),
        in_specs=[pl.BlockSpec((1, m, d), lambda i: (i, 0, 0)), _layer_slab(w_kv, layer)],
        out_specs=pl.BlockSpec((1, m, 2 * d), lambda i: (i, 0, 0)),
        out_shape=jax.ShapeDtypeStruct((bsz, m, 2 * d), BF16),
        compiler_params=_params(("parallel",)),
        name="memory_kv",
    )(mem, w_kv)


def _mix_mem_kernel(x_ref, ya_ref, yb_ref, yc_ref, yd_ref, kv_ref, wout_ref, g1_ref, b1_ref, wq_ref, wo_ref,
                    g2_ref, b2_ref, o_ref, *, sub):
    gw = GROUP_WIDTH
    hd = MEM_HEAD_DIM
    n_sub = x_ref.shape[1] // sub
    tile = lambda i: slice(i * sub, (i + 1) * sub)
    x1, att = [None] * n_sub, [None] * n_sub

    def mix(i):
        tot = None
        for group, y_ref in enumerate((ya_ref, yb_ref, yc_ref, yd_ref)):
            part = _dot(y_ref[0, tile(i), :].astype(BF16), wout_ref[group * gw:(group + 1) * gw, :])
            tot = part if tot is None else tot + part
        x1[i] = _layer_norm(DN_ALPHA * x_ref[0, tile(i), :] + tot, g1_ref[...], b1_ref[...])

    def attend(i):
        q = _dot(x1[i].astype(BF16), wq_ref[...])
        outs = []
        for h in range(MEM_HEADS):
            qh = (q[:, h * hd:(h + 1) * hd] * (hd ** -0.5)).astype(BF16)
            kh = kv_ref[0, :, h * hd:(h + 1) * hd]
            vh = kv_ref[0, :, D_MODEL + h * hd:D_MODEL + (h + 1) * hd]
            s = _dot_nt(qh, kh)
            p = jnp.exp(s - jnp.max(s, axis=-1, keepdims=True))
            p = p / jnp.sum(p, axis=-1, keepdims=True)
            outs.append(_dot(p.astype(BF16), vh).astype(BF16))
        att[i] = jnp.concatenate(outs, axis=-1)

    def project(i):
        y = _dot(att[i], wo_ref[...])
        o_ref[0, tile(i), :] = _layer_norm(DN_ALPHA * x1[i] + y, g2_ref[...], b2_ref[...])

    _skewed(range(n_sub), [mix, attend, project])


def _mix_and_memory_ln(x, ya, yb, yc, yd, kv, layer, w_out, g_mix, b_mix, wq, wo, g_mem, b_mem):
    bsz, t, d = x.shape
    tm = min(MIX_TILE, t)
    row = lambda c: pl.BlockSpec((1, tm, c), lambda i, j: (i, j, 0))
    return pl.pallas_call(
        functools.partial(_mix_mem_kernel, sub=min(ROW_SUBTILE, tm)),
        grid=(bsz, t // tm),
        in_specs=[row(d)] + [row(GROUP_WIDTH)] * 4 + [pl.BlockSpec((1, kv.shape[1], 2 * d), lambda i, j: (i, 0, 0))]
                 + [_layer_slab(a, layer) for a in (w_out, g_mix, b_mix, wq, wo, g_mem, b_mem)],
        out_specs=row(d),
        out_shape=jax.ShapeDtypeStruct(x.shape, F32),
        compiler_params=_params(("parallel", "parallel")),
        name="mix_and_memory_ln",
    )(x, ya, yb, yc, yd, kv, w_out, g_mix, b_mix, wq, wo, g_mem, b_mem)


def _combined_in_weights(w_in):
    gw, nh = GROUP_WIDTH, GROUP_HEADS
    o = 0
    gdn_qkv = w_in[..., o:o + 3 * gw]; o += 3 * gw
    gdn_z = w_in[..., o:o + gw]; o += gw
    gdn_a = w_in[..., o:o + nh]; o += nh
    gdn_b = w_in[..., o:o + nh]; o += nh
    fox_qkv = w_in[..., o:o + 3 * gw]; o += 3 * gw
    fox_f = w_in[..., o:o + nh]; o += nh
    conf = w_in[..., o:o + 2 * gw]; o += 2 * gw
    sb_qkv = w_in[..., o:o + 3 * gw]
    pad = jnp.zeros(w_in.shape[:-1] + (GATE_LANES - 3 * nh,), w_in.dtype)
    return jnp.concatenate([gdn_qkv, gdn_z, fox_qkv, conf, sb_qkv, gdn_a, gdn_b, fox_f, pad], axis=-1).astype(BF16)


def _gate_rows(vals, first_lane):
    depth, nh = vals.shape
    return jnp.pad(vals, ((0, 0), (first_lane, GATE_LANES - first_lane - nh))).reshape(depth, 1, GATE_LANES)


def kernel(x, mem, ffn1_w_gate, ffn1_w_up, ffn1_w_down, ln_ffn1_g, ln_ffn1_b, w_in, gdn_conv_w, gdn_a_log, gdn_dt_bias, gdn_norm_g, fox_b_f, conf_dw_w, conf_dw_b, conf_norm_g, conf_norm_b, w_out, ln_mix_g, ln_mix_b, mem_w_q, mem_w_kv, mem_w_o, ln_mem_g, ln_mem_b, ffn2_w_gate, ffn2_w_up, ffn2_w_down, ln_ffn2_g, ln_ffn2_b):
    bf = lambda a: a.astype(BF16)
    rows = lambda a: a.reshape(a.shape[0], 1, a.shape[1])
    ffn1 = (bf(ffn1_w_gate), bf(ffn1_w_up), bf(ffn1_w_down), rows(ln_ffn1_g), rows(ln_ffn1_b))
    ffn2 = (bf(ffn2_w_gate), bf(ffn2_w_up), bf(ffn2_w_down), rows(ln_ffn2_g), rows(ln_ffn2_b))
    w_cat = _combined_in_weights(w_in)
    gdn = (gdn_conv_w, _gate_rows(gdn_a_log, GATE_A0), _gate_rows(gdn_dt_bias, GATE_A0),
           rows(jnp.tile(gdn_norm_g, (1, GROUP_HEADS))),
           jnp.kron(jnp.eye(GROUP_HEADS, dtype=BF16), jnp.ones((HEAD_DIM, HEAD_DIM), BF16)))
    fox_bias = _gate_rows(fox_b_f, GATE_F0)
    conf = (conf_dw_w, rows(conf_dw_b), rows(conf_norm_g), rows(conf_norm_b))
    mix_mem = (bf(w_out), rows(ln_mix_g), rows(ln_mix_b), bf(mem_w_q), bf(mem_w_o), rows(ln_mem_g), rows(ln_mem_b))
    w_kv = bf(mem_w_kv)

    for i in range(DEPTH):
        x = _ffn_ln(x, i, *ffn1)
        gq, gk, gv, gz, fq, fk, fv, glu, sq, sk, sv, gates = _in_projection(x, i, w_cat)
        y_a = _gdn_mixer(gq, gk, gv, gz, gates, i, *gdn)
        y_b = _fox_attention(fq, fk, fv, _fox_cumsum(gates, i, fox_bias))
        y_c = _conv_module(glu, i, *conf)
        y_d = _sb_attention(sq, sk, sv)
        x = _mix_and_memory_ln(x, y_a, y_b, y_c, y_d, _memory_kv(mem, i, w_kv), i, *mix_mem)
        x = _ffn_ln(x, i, *ffn2)
    return x
```

```python
import functools

import jax
import jax.numpy as jnp
from jax import lax
from jax.experimental import pallas as pl
from jax.experimental.pallas import tpu as pltpu

F32 = jnp.float32
BF16 = jnp.bfloat16

D_MODEL = 1024
DEPTH = 2
GROUP_WIDTH = 256
HEAD_DIM = 64
GROUP_HEADS = 4
D_FF = 2816
SHORT_CONV = 4
CONF_KERNEL = 31
CONF_GROUPS = 4
GDN_CHUNK = 64
MEM_HEADS = 4
MEM_HEAD_DIM = D_MODEL // MEM_HEADS
DN_ALPHA = float((2 * DEPTH) ** 0.25)
LN_EPS = 1e-5
RMS_EPS = 1e-6
L2_EPS = 1e-6
NEG_BIG = -1e30

GATE_LANES = 128
GATE_A0, GATE_B0, GATE_F0 = 0, 4, 8

TOKEN_TILE = 512
MXU_WIDTH = 256
SUBLANES = 8
FFN_SPLITS = (0, 6 * MXU_WIDTH, D_FF)
PAIR_WIDTH = 2 * HEAD_DIM
FOX_BLOCK = 512
SB_Q_BLOCK = 512
SB_K_BLOCK = 256
ATTN_SUBTILE = 256
MIX_TILE = 1024
ROW_SUBTILE = 512
SB_LOG_UNDERFLOW = -105.0
FOX_LOG_UNDERFLOW = -106.0
GDN_BLOCK = 256
CONV_BLOCK = 512
VMEM_LIMIT = 56 * 1024 * 1024


def _params(sem, vmem=VMEM_LIMIT):
    return pltpu.CompilerParams(dimension_semantics=sem, vmem_limit_bytes=vmem)


def _resident(shape):
    nd = len(shape)
    return pl.BlockSpec(shape, lambda *_: (0,) * nd, pipeline_mode=pl.Buffered(1))


def _layer_slab(stacked, layer):
    tail = stacked.shape[1:]
    return pl.BlockSpec((None,) + tail, lambda *_: (layer,) + (0,) * len(tail), pipeline_mode=pl.Buffered(1))


def _layer_norm(y, g, b):
    mu = jnp.mean(y, axis=-1, keepdims=True)
    d = y - mu
    var = jnp.mean(d * d, axis=-1, keepdims=True)
    return d * lax.rsqrt(var + LN_EPS) * g + b


def _sigmoid(x):
    return 1.0 / (1.0 + jnp.exp(-x))


def _silu(x):
    return x * _sigmoid(x)


def _softplus(x):
    return jnp.maximum(x, 0.0) + jnp.log(1.0 + jnp.exp(-jnp.abs(x)))


def _dot(a, b):
    return jnp.dot(a, b, preferred_element_type=F32)


def _dot_nt(a, b):
    return lax.dot_general(a, b, (((1,), (1,)), ((), ())), preferred_element_type=F32)


def _dot_tn(a, b):
    return lax.dot_general(a, b, (((0,), (0,)), ((), ())), preferred_element_type=F32)


def _mask_dot(mask, x):
    n = x.shape[1]
    hi = x.astype(BF16)
    r1 = x - hi.astype(F32)
    mid = r1.astype(BF16)
    lo = (r1 - mid.astype(F32)).astype(BF16)
    r = _dot(mask.astype(BF16), jnp.concatenate([hi, mid, lo], axis=1))
    return r[:, :n] + r[:, n:2 * n] + r[:, 2 * n:]


def _ffn_kernel(x_ref, wg_ref, wu_ref, wd_ref, g_ref, b_ref, o_ref):
    x = x_ref[0]
    xb = x.astype(BF16)
    acc = None
    for lo, hi in zip(FFN_SPLITS[:-1], FFN_SPLITS[1:]):
        h = _dot(xb, wg_ref[:, lo:hi])
        u = _dot(xb, wu_ref[:, lo:hi])
        a = (_silu(h) * u).astype(BF16)
        part = _dot(a, wd_ref[lo:hi, :])
        acc = part if acc is None else acc + part
    o_ref[0] = _layer_norm(DN_ALPHA * x + 0.5 * acc, g_ref[...], b_ref[...])


def _ffn_ln(x, layer, wg, wu, wd, g, b):
    bsz, t, d = x.shape
    tm = min(TOKEN_TILE, t)
    row = pl.BlockSpec((1, tm, d), lambda i, j: (i, j, 0))
    return pl.pallas_call(
        _ffn_kernel,
        grid=(bsz, t // tm),
        in_specs=[row] + [_layer_slab(a, layer) for a in (wg, wu, wd, g, b)],
        out_specs=row,
        out_shape=jax.ShapeDtypeStruct(x.shape, F32),
        compiler_params=_params(("parallel", "parallel")),
        name="ffn_ln",
    )(x, wg, wu, wd, g, b)


def _inproj_kernel(x_ref, w_ref, gq_ref, gk_ref, gv_ref, gz_ref, fq_ref, fk_ref, fv_ref,
                   glu_ref, sq_ref, sk_ref, sv_ref, gate_ref):
    xb = x_ref[0].astype(BF16)
    gw = GROUP_WIDTH

    def group(i, width=gw):
        return _dot(xb, w_ref[:, i * gw:i * gw + width])

    qk_scale = HEAD_DIM ** -0.5
    gq_ref[0] = group(0)
    gk_ref[0] = group(1)
    gv_ref[0] = group(2)
    gz_ref[0] = group(3)
    fq_ref[0] = (group(4) * qk_scale).astype(BF16)
    fk_ref[0] = group(5).astype(BF16)
    fv_ref[0] = group(6).astype(BF16)
    glu_ref[0] = group(7) * _sigmoid(group(8))
    sq_ref[0] = (group(9) * qk_scale).astype(BF16)
    sk_ref[0] = group(10).astype(BF16)
    sv_ref[0] = group(11).astype(BF16)
    gate_ref[0] = group(12, GATE_LANES)


def _in_projection(x, layer, w_cat):
    bsz, t, d = x.shape
    tm = min(TOKEN_TILE, t)
    row = lambda c: pl.BlockSpec((1, tm, c), lambda i, j: (i, j, 0))
    wide = lambda dt: jax.ShapeDtypeStruct((bsz, t, GROUP_WIDTH), dt)
    out_shape = ([wide(F32)] * 4 + [wide(BF16)] * 3 + [wide(F32)] + [wide(BF16)] * 3
                 + [jax.ShapeDtypeStruct((bsz, t, GATE_LANES), F32)])
    out_specs = [row(GROUP_WIDTH)] * 11 + [row(GATE_LANES)]
    return pl.pallas_call(
        _inproj_kernel,
        grid=(bsz, t // tm),
        in_specs=[row(d), _layer_slab(w_cat, layer)],
        out_specs=out_specs,
        out_shape=out_shape,
        compiler_params=_params(("parallel", "parallel")),
        name="in_projection",
    )(x, w_cat)


def _fox_cum_kernel(gate_ref, bias_ref, o_ref, carry_ref):
    @pl.when(pl.program_id(1) == 0)
    def _():
        carry_ref[...] = jnp.zeros_like(carry_ref)

    tb = gate_ref.shape[1]
    logit = gate_ref[0] + bias_ref[...]
    log_f = -_softplus(-logit)
    rows = lax.broadcasted_iota(jnp.int32, (tb, tb), 0)
    cols = lax.broadcasted_iota(jnp.int32, (tb, tb), 1)
    cum = _mask_dot(cols <= rows, log_f) + carry_ref[...]
    carry_ref[...] = cum[tb - 1:tb, :]
    o_ref[0, 0] = cum.T


def _fox_cumsum(gates, layer, bias_rows):
    bsz, t, _ = gates.shape
    tb = min(FOX_BLOCK, t)
    return pl.pallas_call(
        _fox_cum_kernel,
        grid=(bsz, t // tb),
        in_specs=[pl.BlockSpec((1, tb, GATE_LANES), lambda i, j: (i, j, 0)), _layer_slab(bias_rows, layer)],
        out_specs=pl.BlockSpec((1, 1, GATE_LANES, tb), lambda i, j: (i, j, 0, 0)),
        out_shape=jax.ShapeDtypeStruct((bsz, t // tb, GATE_LANES, tb), F32),
        scratch_shapes=[pltpu.VMEM((1, GATE_LANES), F32)],
        compiler_params=_params(("parallel", "arbitrary")),
        name="fox_cumsum",
    )(gates, bias_rows)


def _causal_masks(blk):
    rows = lax.broadcasted_iota(jnp.int32, (blk, blk), 0)
    cols = lax.broadcasted_iota(jnp.int32, (blk, blk), 1)
    return cols <= rows, cols < rows


def _stack_heads(x2):
    lane = lax.broadcasted_iota(jnp.int32, x2.shape, 1)
    zero = jnp.zeros_like(x2)
    return jnp.concatenate([jnp.where(lane < HEAD_DIM, x2, zero), jnp.where(lane >= HEAD_DIM, x2, zero)], axis=0)


def _unstack_heads(y, rows):
    lane = lax.broadcasted_iota(jnp.int32, (rows, PAIR_WIDTH), 1)
    return jnp.where(lane < HEAD_DIM, y[:rows], y[rows:])


def _skewed(tiles, stages):
    tiles = list(tiles)
    for t in range(len(tiles) + len(stages) - 1):
        for k in reversed(range(len(stages))):
            if 0 <= t - k < len(tiles):
                stages[k](tiles[t - k])


def _fox_kernel(q_ref, k_ref, v_ref, ck_ref, o_ref, kmax_ref, *, sub):
    blk = q_ref.shape[1]
    n_pairs = q_ref.shape[2] // PAIR_WIDTH
    qi = pl.program_id(1)
    lanes = lambda p: slice(p * PAIR_WIDTH, (p + 1) * PAIR_WIDTH)

    @pl.when(qi == 0)
    def _():
        for p in range(n_pairs):
            def widest(c, best):
                kk = k_ref[0, pl.ds(pl.multiple_of(c * blk, blk), blk), lanes(p)].astype(F32)
                return jnp.maximum(best, jnp.sum(kk * kk, axis=-1, keepdims=True))
            best = lax.fori_loop(0, k_ref.shape[1] // blk, widest, jnp.zeros((blk, 1), F32))
            kmax_ref[p] = jnp.sqrt(jnp.max(best, axis=0, keepdims=True))

    qs = jnp.concatenate([_stack_heads(q_ref[0, :, lanes(p)]) for p in range(n_pairs)], axis=0)
    n_sub = 2 * n_pairs * blk // sub
    tile = lambda i: slice(i * sub, (i + 1) * sub)
    head_of = lambda i: (i * sub) // blk
    qf = qs.astype(F32)
    qnorm = jnp.sqrt(jnp.sum(qf * qf, axis=-1, keepdims=True))
    qk_bound = jnp.concatenate([qnorm[2 * p * blk:2 * (p + 1) * blk] * kmax_ref[p] for p in range(n_pairs)],
                               axis=0)
    rows = lax.broadcasted_iota(jnp.int32, (sub, blk), 0)
    cols = lax.broadcasted_iota(jnp.int32, (sub, blk), 1)
    ones = jnp.ones((blk, PAIR_WIDTH), BF16)

    def step(j, carry, masked):
        ms, accs = carry
        start = pl.multiple_of(j * blk, blk)
        kb = [k_ref[0, pl.ds(start, blk), lanes(p)] for p in range(n_pairs)]
        vb = [jnp.concatenate([v_ref[0, pl.ds(start, blk), lanes(p)], ones], axis=1) for p in range(n_pairs)]
        ms, accs = list(ms), list(accs)
        s, p, alpha = [None] * n_sub, [None] * n_sub, [None] * n_sub

        def logits(i):
            head, off = divmod(i * sub, blk)
            si = _dot_nt(qs[tile(i)], kb[head // 2]) - ck_ref[0, j, head:head + 1, :]
            s[i] = jnp.where(cols <= rows + off, si, NEG_BIG) if masked else si

        def probs(i):
            m_new = jnp.maximum(ms[i], jnp.max(s[i], axis=-1, keepdims=True))
            p[i] = jnp.exp(s[i] - m_new).astype(BF16)
            alpha[i] = jnp.exp(ms[i] - m_new)
            ms[i] = m_new

        def values(i):
            accs[i] = alpha[i] * accs[i] + _dot(p[i], vb[head_of(i) // 2])

        _skewed(range(n_sub), [logits, probs, values])
        return tuple(ms), tuple(accs)

    def headroom(j, ms):
        room = None
        for i in range(n_sub):
            last = -ck_ref[0, j, head_of(i):head_of(i) + 1, :][:, blk - 1:blk]
            r = jnp.max(qk_bound[tile(i)] - ms[i] + last)
            room = r if room is None else jnp.maximum(room, r)
        return room

    init = (tuple(jnp.full((sub, 1), NEG_BIG, F32) for _ in range(n_sub)),
            tuple(jnp.zeros((sub, 2 * PAIR_WIDTH), F32) for _ in range(n_sub)))
    carry = step(qi, init, True)

    def more(state):
        i, room, _ = state
        return (i < qi) & (room > FOX_LOG_UNDERFLOW)

    def sweep(state):
        i, _, c = state
        c = step(qi - 1 - i, c, False)
        return i + 1, headroom(jnp.maximum(qi - 2 - i, 0), c[0]), c

    _, _, (_, accs) = lax.while_loop(more, sweep, (jnp.int32(0), headroom(jnp.maximum(qi - 1, 0), carry[0]), carry))
    acc = jnp.concatenate(accs, axis=0)
    out = acc[:, :PAIR_WIDTH] / acc[:, PAIR_WIDTH:PAIR_WIDTH + 1]
    o_ref[0] = jnp.concatenate([_unstack_heads(out[2 * p * blk:2 * (p + 1) * blk], blk) for p in range(n_pairs)],
                               axis=1)


def _fox_attention(q, k, v, cum_t):
    bsz, t, width = q.shape
    blk = min(FOX_BLOCK, t)
    nk = t // blk
    whole = pl.BlockSpec((1, t, width), lambda b, i: (b, 0, 0))
    qblk = pl.BlockSpec((1, blk, width), lambda b, i: (b, i, 0))
    return pl.pallas_call(
        functools.partial(_fox_kernel, sub=min(ATTN_SUBTILE, blk)),
        grid=(bsz, nk),
        in_specs=[qblk, whole, whole,
                  pl.BlockSpec((1, nk, 8, blk), lambda b, i: (b, 0, GATE_F0 // 8, 0))],
        out_specs=qblk,
        out_shape=jax.ShapeDtypeStruct((bsz, t, width), F32),
        scratch_shapes=[pltpu.VMEM((width // PAIR_WIDTH, 1, 1), F32)],
        compiler_params=_params(("arbitrary", "arbitrary")),
        name="fox_attention",
    )(q, k, v, cum_t)


def _sb_kernel(q_ref, k_ref, v_ref, o_ref, *, tk, sub):
    tq = q_ref.shape[1]
    n_pairs = q_ref.shape[2] // PAIR_WIDTH
    per = tq // tk
    qi = pl.program_id(1)
    lanes = lambda p: slice(p * PAIR_WIDTH, (p + 1) * PAIR_WIDTH)
    qs = jnp.concatenate([_stack_heads(q_ref[0, :, lanes(p)]) for p in range(n_pairs)], axis=0)
    n_sub = 2 * n_pairs * tq // sub
    pair_of = lambda i: (i * sub) // (2 * tq)
    rows = lax.broadcasted_iota(jnp.int32, (sub, tk), 0)
    cols = lax.broadcasted_iota(jnp.int32, (sub, tk), 1)
    suffix = _causal_masks(tk)[0].astype(BF16)

    def step(j, carry, diag):
        rests, accs = carry
        start = pl.multiple_of(j * tk, tk)
        kb = [k_ref[0, pl.ds(start, tk), lanes(p)] for p in range(n_pairs)]
        vb = [v_ref[0, pl.ds(start, tk), lanes(p)] for p in range(n_pairs)]
        rests, accs = list(rests), list(accs)
        z, split, w, strict = [None] * n_sub, [None] * n_sub, [None] * n_sub, [None] * n_sub

        def logits(i):
            z[i] = _dot_nt(qs[i * sub:(i + 1) * sub], kb[pair_of(i)])
            if diag is not None:
                strict[i] = cols + diag < rows + (i * sub) % tq

        def keep(i):
            log_keep = -_softplus(z[i])
            if diag is not None:
                log_keep = jnp.where(strict[i], log_keep, 0.0)
            split[i] = jnp.concatenate(_split_bf16(log_keep), axis=0)

        def weights(i):
            tails = _dot(split[i], suffix)
            tail = tails[:sub] + tails[sub:]
            wi = jnp.exp(z[i] + tail + rests[i])
            if diag is not None:
                wi = jnp.where(strict[i], wi, 0.0)
            w[i] = wi.astype(BF16)
            rests[i] = rests[i] + tail[:, 0:1]

        def values(i):
            accs[i] = accs[i] + _dot(w[i], vb[pair_of(i)])

        live = [i for i in range(n_sub) if diag is None or diag < (i * sub) % tq + sub - 1]
        _skewed(live, [logits, keep, weights, values])
        return tuple(rests), tuple(accs)

    carry = (tuple(jnp.zeros((sub, 1), F32) for _ in range(n_sub)),
             tuple(jnp.zeros((sub, PAIR_WIDTH), F32) for _ in range(n_sub)))
    for d in reversed(range(per)):
        carry = step(qi * per + d, carry, d * tk)

    def largest(rests):
        return functools.reduce(jnp.maximum, [jnp.max(r) for r in rests])

    def more(state):
        i, top, _ = state
        return (i < qi * per) & (top > SB_LOG_UNDERFLOW)

    def sweep(state):
        i, _, c = state
        c = step(qi * per - 1 - i, c, None)
        return i + 1, largest(c[0]), c

    _, _, (_, accs) = lax.while_loop(more, sweep, (jnp.int32(0), largest(carry[0]), carry))
    acc = jnp.concatenate(accs, axis=0)
    o_ref[0] = jnp.concatenate([_unstack_heads(acc[2 * p * tq:2 * (p + 1) * tq], tq) for p in range(n_pairs)],
                               axis=1)


def _sb_attention(q, k, v):
    bsz, t, width = q.shape
    tq = min(SB_Q_BLOCK, t)
    tk = min(SB_K_BLOCK, tq)
    whole = pl.BlockSpec((1, t, width), lambda b, i: (b, 0, 0))
    qblk = pl.BlockSpec((1, tq, width), lambda b, i: (b, i, 0))
    return pl.pallas_call(
        functools.partial(_sb_kernel, tk=tk, sub=min(ATTN_SUBTILE, tq)),
        grid=(bsz, t // tq),
        in_specs=[qblk, whole, whole],
        out_specs=qblk,
        out_shape=jax.ShapeDtypeStruct((bsz, t, width), F32),
        compiler_params=_params(("parallel", "arbitrary")),
        name="sb_attention",
    )(q, k, v)


CONV_HALO = 32


def _conv_kernel(prev_ref, cur_ref, w_ref, b_ref, ng_ref, nb_ref, o_ref, xs_ref, sh_ref):
    tb = cur_ref.shape[1]
    first = pl.program_id(1) == 0
    xs_ref[0:CONV_HALO, :] = jnp.where(first, 0.0, prev_ref[0])
    xs_ref[CONV_HALO:CONV_HALO + tb, :] = cur_ref[0]
    span = tb + CONV_HALO - SUBLANES
    for r in range(1, SUBLANES):
        sh_ref[r, 0:span, :] = xs_ref[r:r + span, :]
    base = CONV_HALO - (CONF_KERNEL - 1)
    acc = jnp.zeros((tb, GROUP_WIDTH), F32) + b_ref[...]
    for tap in range(CONF_KERNEL):
        whole, r = divmod(base + tap, SUBLANES)
        src = xs_ref if r == 0 else sh_ref.at[r]
        acc = acc + w_ref[tap:tap + 1, :] * src[whole * SUBLANES:whole * SUBLANES + tb, :]
    gsz = GROUP_WIDTH // CONF_GROUPS
    parts = []
    for g in range(CONF_GROUPS):
        cg = acc[:, g * gsz:(g + 1) * gsz]
        mu = jnp.mean(cg, axis=-1, keepdims=True)
        d = cg - mu
        var = jnp.mean(d * d, axis=-1, keepdims=True)
        parts.append(d * lax.rsqrt(var + LN_EPS))
    hn = jnp.concatenate(parts, axis=-1) * ng_ref[...] + nb_ref[...]
    o_ref[0] = _silu(hn)


def _conv_module(glu, layer, w, b, ng, nb):
    bsz, t, c = glu.shape
    tb = min(CONV_BLOCK, t)
    per = tb // CONV_HALO
    return pl.pallas_call(
        _conv_kernel,
        grid=(bsz, t // tb),
        in_specs=[pl.BlockSpec((1, CONV_HALO, c), lambda i, j: (i, jnp.maximum(j * per - 1, 0), 0)),
                  pl.BlockSpec((1, tb, c), lambda i, j: (i, j, 0)),
                  ] + [_layer_slab(a, layer) for a in (w, b, ng, nb)],
        out_specs=pl.BlockSpec((1, tb, c), lambda i, j: (i, j, 0)),
        out_shape=jax.ShapeDtypeStruct(glu.shape, F32),
        scratch_shapes=[pltpu.VMEM((CONV_HALO + tb, c), F32), pltpu.VMEM((SUBLANES, CONV_HALO + tb, c), F32)],
        compiler_params=_params(("parallel", "parallel")),
        name="conv_module",
    )(glu, glu, w, b, ng, nb)


GDN_HALO = 8


def _split_bf16(x):
    hi = x.astype(BF16)
    return hi, (x - hi.astype(F32)).astype(BF16)


def _group_sum(x, ones_bd):
    rows = x.shape[0]
    hi, lo = _split_bf16(x)
    r = _dot(jnp.concatenate([hi, lo], axis=0), ones_bd)
    return r[:rows] + r[rows:]


def _odd_blocks(x, size):
    return jnp.concatenate([x[r:r + size] for r in range(size, x.shape[0], 2 * size)], axis=0)


def _spread_odd_blocks(y, size, fill):
    parts = []
    for b in range(y.shape[0] // size):
        parts += [fill[2 * b * size:(2 * b + 1) * size], y[b * size:(b + 1) * size]]
    return jnp.concatenate(parts, axis=0)


def _unit_lower_inverses(lows, rows, cols):
    n = lows[0].shape[0]
    eye = jnp.where(rows == cols, 1.0, 0.0)
    first = (rows % 2 == 1) & (cols == rows - 1)
    invs = [eye - jnp.where(first, low, 0.0) for low in lows]
    size = 2
    while size < GDN_CHUNK:
        rb = rows // size
        level = (rb % 2 == 1) & (cols // size == rb - 1)
        odd_only = size % SUBLANES == 0
        pick = (lambda x: _odd_blocks(x, size)) if odd_only else (lambda x: x)
        splits = [_split_bf16(inv) for inv in invs]
        xs = []
        for low, (d_hi, d_lo) in zip(lows, splits):
            x2 = _dot(pick(jnp.where(level, low, 0.0)).astype(BF16), jnp.concatenate([d_hi, d_lo], axis=1))
            x = x2[:, :n] + x2[:, n:]
            if odd_only:
                x = _spread_odd_blocks(x, size, jnp.zeros((n, n), F32))
            xs.append(_split_bf16(x))
        nxt = []
        for inv, (d_hi, d_lo), (x_hi, x_lo) in zip(invs, splits, xs):
            m = n // 2 if odd_only else n
            lhs = jnp.concatenate([pick(inv).astype(BF16), pick(inv - d_hi.astype(F32)).astype(BF16)], axis=0) \
                if odd_only else jnp.concatenate([d_hi, d_lo], axis=0)
            y4 = _dot(lhs, jnp.concatenate([x_hi, x_lo], axis=1))
            y = y4[:m, :n] + y4[:m, n:] + y4[m:, :n] + y4[m:, n:]
            nxt.append(_spread_odd_blocks(pick(inv) - y, size, inv) if odd_only else inv - y)
        invs = nxt
        size *= 2
    return invs


def _gdn_kernel(qp_ref, kp_ref, vp_ref, q_ref, k_ref, v_ref, z_ref, gate_ref, cw_ref, alog_ref, dtb_ref,
                ng_ref, ones_ref, o_ref, xs_ref, y_ref, state_ref):
    nb, tb = q_ref.shape[0], q_ref.shape[1]
    c = GDN_CHUNK
    n = 2 * c
    gw = GROUP_WIDTH
    n_pairs = GROUP_HEADS // 2
    first = pl.program_id(0) == 0

    @pl.when(first)
    def _():
        state_ref[...] = jnp.zeros_like(state_ref)

    ones_bd = ones_ref[...]
    base = GDN_HALO - (SHORT_CONV - 1)

    def conv_silu(idx, p_ref, c_ref, b):
        xs_ref[0:GDN_HALO, :] = jnp.where(first, 0.0, p_ref[b])
        xs_ref[GDN_HALO:GDN_HALO + tb, :] = c_ref[b]
        acc = None
        for tap in range(SHORT_CONV):
            term = cw_ref[tap:tap + 1, idx * gw:(idx + 1) * gw] * xs_ref[base + tap:base + tap + tb, :]
            acc = term if acc is None else acc + term
        return _silu(acc)

    def l2_normalize(y):
        return y * lax.rsqrt(_group_sum(y * y, ones_bd) + L2_EPS)

    rows_t = lax.broadcasted_iota(jnp.int32, (tb, tb), 0)
    cols_t = lax.broadcasted_iota(jnp.int32, (tb, tb), 1)
    chunk_tri = (cols_t <= rows_t) & (cols_t // c == rows_t // c)
    lane_bcast = lambda x, lane: jnp.broadcast_to(x[:, lane:lane + 1], (tb, PAIR_WIDTH))
    low_half = lax.broadcasted_iota(jnp.int32, (tb, PAIR_WIDTH), 1) < HEAD_DIM
    rows = lax.broadcasted_iota(jnp.int32, (n, n), 0)
    cols = lax.broadcasted_iota(jnp.int32, (n, n), 1)
    same_head = rows // c == cols // c
    lower_incl = same_head & (cols <= rows)
    strict_lower = same_head & (cols < rows)

    pairs, g_wides = {}, {}
    for b in range(nb):
        qn = l2_normalize(conv_silu(0, qp_ref, q_ref, b)) * (HEAD_DIM ** -0.5)
        kn = l2_normalize(conv_silu(1, kp_ref, k_ref, b))
        vn = conv_silu(2, vp_ref, v_ref, b)
        gates = gate_ref[b]
        log_decay = -jnp.exp(alog_ref[...]) * _softplus(gates + dtb_ref[...])
        beta_all = _sigmoid(gates)
        gcum = _mask_dot(chunk_tri, log_decay)
        g_wide = [lane_bcast(gcum, GATE_A0 + h) for h in range(GROUP_HEADS)]
        b_wide = [lane_bcast(beta_all, GATE_B0 + h) for h in range(GROUP_HEADS)]
        g_wides[b] = g_wide
        for p in range(n_pairs):
            ls = slice(p * PAIR_WIDTH, (p + 1) * PAIR_WIDTH)
            g_nat = jnp.where(low_half, g_wide[2 * p], g_wide[2 * p + 1])
            b_nat = jnp.where(low_half, b_wide[2 * p], b_wide[2 * p + 1])
            eg = jnp.exp(g_nat)
            k_beta = kn[:, ls] * b_nat
            pairs[b, p] = dict(ls=ls, g_nat=g_nat, q=qn[:, ls], k=kn[:, ls], q_dec=qn[:, ls] * eg, k_beta=k_beta,
                               v_beta=vn[:, ls] * b_nat, kb_eg=k_beta * eg)

    systems = [(ci, b, p) for ci in range(tb // c) for b in range(nb) for p in range(n_pairs)]
    chunk_rows = lambda ci: slice(ci * c, (ci + 1) * c)

    lkks, a_qks = [], []
    for ci, b, p in systems:
        d, r, g_wide = pairs[b, p], chunk_rows(ci), g_wides[b]
        g_col = jnp.concatenate([g_wide[2 * p][r], g_wide[2 * p + 1][r]], axis=0)
        decay = jnp.exp(jnp.where(lower_incl, g_col - g_col.T, -jnp.inf))
        k_st = _stack_heads(d["k"][r]).astype(BF16)
        twice = lambda x: jnp.concatenate([x, x], axis=0)
        lhs = jnp.concatenate([twice(d["k_beta"][r]), twice(d["q"][r])], axis=0).astype(BF16)
        gram = _dot_nt(lhs, k_st)
        lkks.append(jnp.where(strict_lower, gram[:n] * decay, 0.0))
        a_qks.append((gram[n:] * decay).astype(BF16))
    t_invs = _unit_lower_inverses(lkks, rows, cols)
    uws = []
    for (ci, b, p), t_inv in zip(systems, t_invs):
        d, r = pairs[b, p], chunk_rows(ci)
        rhs = jnp.concatenate([_stack_heads(d["v_beta"][r]), _stack_heads(d["kb_eg"][r])], axis=1)
        uws.append(_dot(t_inv.astype(BF16), rhs.astype(BF16)))

    states = {(b, p): state_ref[b * n_pairs + p] for b in range(nb) for p in range(n_pairs)}
    for idx, (ci, b, p) in enumerate(systems):
        d, r = pairs[b, p], chunk_rows(ci)
        uw, state = uws[idx], states[b, p]
        g_last = d["g_nat"][ci * c + c - 1:ci * c + c, :]
        k_dec = _stack_heads(d["k"][r] * jnp.exp(g_last - d["g_nat"][r])).astype(BF16)
        wq = _dot(jnp.concatenate([uw[:, n:].astype(BF16), _stack_heads(d["q_dec"][r]).astype(BF16)], axis=0),
                  state.astype(BF16))
        vnb = (uw[:, :n] - wq[:n]).astype(BF16)
        o_st = wq[n:] + _dot(a_qks[idx], vnb)
        states[b, p] = state * jnp.exp(g_last) + _dot_tn(k_dec, vnb)
        y_ref[b, r, d["ls"]] = o_st[:c] + o_st[c:]
    for (b, p), state in states.items():
        state_ref[b * n_pairs + p] = state

    for b in range(nb):
        o = y_ref[b]
        mean_sq = _group_sum(o * o, ones_bd) * (1.0 / HEAD_DIM)
        o_ref[b] = o * lax.rsqrt(mean_sq + RMS_EPS) * ng_ref[...] * _silu(z_ref[b])


def _gdn_mixer(q, k, v, z, gates, layer, conv_w, a_log_rows, dt_bias_rows, norm_g_rows, head_ones):
    bsz, t, width = q.shape
    nh = GROUP_HEADS
    tb = min(GDN_BLOCK, t)
    per = tb // GDN_HALO
    cur = pl.BlockSpec((bsz, tb, width), lambda j: (0, j, 0))
    prev = pl.BlockSpec((bsz, GDN_HALO, width), lambda j: (0, jnp.maximum(j * per - 1, 0), 0))
    return pl.pallas_call(
        _gdn_kernel,
        grid=(t // tb,),
        in_specs=[prev, prev, prev, cur, cur, cur, cur,
                  pl.BlockSpec((bsz, tb, GATE_LANES), lambda j: (0, j, 0)),
                  ] + [_layer_slab(a, layer) for a in (conv_w, a_log_rows, dt_bias_rows, norm_g_rows)]
                 + [_resident((width, width))],
        out_specs=cur,
        out_shape=jax.ShapeDtypeStruct(q.shape, F32),
        scratch_shapes=[pltpu.VMEM((GDN_HALO + tb, width), F32), pltpu.VMEM((bsz, tb, width), F32),
                        pltpu.VMEM((bsz * nh // 2, PAIR_WIDTH, PAIR_WIDTH), F32)],
        compiler_params=_params(("arbitrary",)),
        name="gdn_mixer",
    )(q, k, v, q, k, v, z, gates, conv_w, a_log_rows, dt_bias_rows, norm_g_rows, head_ones)


def _memkv_kernel(m_ref, w_ref, o_ref):
    o_ref[0] = _dot(m_ref[0].astype(BF16), w_ref[...]).astype(o_ref.dtype)


def _memory_kv(mem, layer, w_kv):
    bsz, m, d = mem.shape
    return pl.pallas_call(
        _memkv_kernel,
        grid=(bsz,),
        in_specs=[pl.BlockSpec((1, m, d), lambda i: (i, 0, 0)), _layer_slab(w_kv, layer)],
        out_specs=pl.BlockSpec((1, m, 2 * d), lambda i: (i, 0, 0)),
        out_shape=jax.ShapeDtypeStruct((bsz, m, 2 * d), BF16),
        compiler_params=_params(("parallel",)),
        name="memory_kv",
    )(mem, w_kv)


def _mix_mem_kernel(x_ref, ya_ref, yb_ref, yc_ref, yd_ref, kv_ref, wout_ref, g1_ref, b1_ref, wq_ref, wo_ref,
                    g2_ref, b2_ref, o_ref, *, sub):
    gw = GROUP_WIDTH
    hd = MEM_HEAD_DIM
    n_sub = x_ref.shape[1] // sub
    tile = lambda i: slice(i * sub, (i + 1) * sub)
    x1, att = [None] * n_sub, [None] * n_sub

    def mix(i):
        tot = None
        for group, y_ref in enumerate((ya_ref, yb_ref, yc_ref, yd_ref)):
            part = _dot(y_ref[0, tile(i), :].astype(BF16), wout_ref[group * gw:(group + 1) * gw, :])
            tot = part if tot is None else tot + part
        x1[i] = _layer_norm(DN_ALPHA * x_ref[0, tile(i), :] + tot, g1_ref[...], b1_ref[...])

    def attend(i):
        q = _dot(x1[i].astype(BF16), wq_ref[...])
        outs = []
        for h in range(MEM_HEADS):
            qh = (q[:, h * hd:(h + 1) * hd] * (hd ** -0.5)).astype(BF16)
            kh = kv_ref[0, :, h * hd:(h + 1) * hd]
            vh = kv_ref[0, :, D_MODEL + h * hd:D_MODEL + (h + 1) * hd]
            s = _dot_nt(qh, kh)
            p = jnp.exp(s - jnp.max(s, axis=-1, keepdims=True))
            p = p / jnp.sum(p, axis=-1, keepdims=True)
            outs.append(_dot(p.astype(BF16), vh).astype(BF16))
        att[i] = jnp.concatenate(outs, axis=-1)

    def project(i):
        y = _dot(att[i], wo_ref[...])
        o_ref[0, tile(i), :] = _layer_norm(DN_ALPHA * x1[i] + y, g2_ref[...], b2_ref[...])

    _skewed(range(n_sub), [mix, attend, project])


def _mix_and_memory_ln(x, ya, yb, yc, yd, kv, layer, w_out, g_mix, b_mix, wq, wo, g_mem, b_mem):
    bsz, t, d = x.shape
    tm = min(MIX_TILE, t)
    row = lambda c: pl.BlockSpec((1, tm, c), lambda i, j: (i, j, 0))
    return pl.pallas_call(
        functools.partial(_mix_mem_kernel, sub=min(ROW_SUBTILE, tm)),
        grid=(bsz, t // tm),
        in_specs=[row(d)] + [row(GROUP_WIDTH)] * 4 + [pl.BlockSpec((1, kv.shape[1], 2 * d), lambda i, j: (i, 0, 0))]
                 + [_layer_slab(a, layer) for a in (w_out, g_mix, b_mix, wq, wo, g_mem, b_mem)],
        out_specs=row(d),
        out_shape=jax.ShapeDtypeStruct(x.shape, F32),
        compiler_params=_params(("parallel", "parallel")),
        name="mix_and_memory_ln",
    )(x, ya, yb, yc, yd, kv, w_out, g_mix, b_mix, wq, wo, g_mem, b_mem)


def _combined_in_weights(w_in):
    gw, nh = GROUP_WIDTH, GROUP_HEADS
    o = 0
    gdn_qkv = w_in[..., o:o + 3 * gw]; o += 3 * gw
    gdn_z = w_in[..., o:o + gw]; o += gw
    gdn_a = w_in[..., o:o + nh]; o += nh
    gdn_b = w_in[..., o:o + nh]; o += nh
    fox_qkv = w_in[..., o:o + 3 * gw]; o += 3 * gw
    fox_f = w_in[..., o:o + nh]; o += nh
    conf = w_in[..., o:o + 2 * gw]; o += 2 * gw
    sb_qkv = w_in[..., o:o + 3 * gw]
    pad = jnp.zeros(w_in.shape[:-1] + (GATE_LANES - 3 * nh,), w_in.dtype)
    return jnp.concatenate([gdn_qkv, gdn_z, fox_qkv, conf, sb_qkv, gdn_a, gdn_b, fox_f, pad], axis=-1).astype(BF16)


def _gate_rows(vals, first_lane):
    depth, nh = vals.shape
    return jnp.pad(vals, ((0, 0), (first_lane, GATE_LANES - first_lane - nh))).reshape(depth, 1, GATE_LANES)


def kernel(x, mem, ffn1_w_gate, ffn1_w_up, ffn1_w_down, ln_ffn1_g, ln_ffn1_b, w_in, gdn_conv_w, gdn_a_log, gdn_dt_bias, gdn_norm_g, fox_b_f, conf_dw_w, conf_dw_b, conf_norm_g, conf_norm_b, w_out, ln_mix_g, ln_mix_b, mem_w_q, mem_w_kv, mem_w_o, ln_mem_g, ln_mem_b, ffn2_w_gate, ffn2_w_up, ffn2_w_down, ln_ffn2_g, ln_ffn2_b):
    bf = lambda a: a.astype(BF16)
    rows = lambda a: a.reshape(a.shape[0], 1, a.shape[1])
    ffn1 = (bf(ffn1_w_gate), bf(ffn1_w_up), bf(ffn1_w_down), rows(ln_ffn1_g), rows(ln_ffn1_b))
    ffn2 = (bf(ffn2_w_gate), bf(ffn2_w_up), bf(ffn2_w_down), rows(ln_ffn2_g), rows(ln_ffn2_b))
    w_cat = _combined_in_weights(w_in)
    gdn = (gdn_conv_w, _gate_rows(gdn_a_log, GATE_A0), _gate_rows(gdn_dt_bias, GATE_A0),
           rows(jnp.tile(gdn_norm_g, (1, GROUP_HEADS))),
           jnp.kron(jnp.eye(GROUP_HEADS, dtype=BF16), jnp.ones((HEAD_DIM, HEAD_DIM), BF16)))
    fox_bias = _gate_rows(fox_b_f, GATE_F0)
    conf = (conf_dw_w, rows(conf_dw_b), rows(conf_norm_g), rows(conf_norm_b))
    mix_mem = (bf(w_out), rows(ln_mix_g), rows(ln_mix_b), bf(mem_w_q), bf(mem_w_o), rows(ln_mem_g), rows(ln_mem_b))
    w_kv = bf(mem_w_kv)

    for i in range(DEPTH):
        x = _ffn_ln(x, i, *ffn1)
        gq, gk, gv, gz, fq, fk, fv, glu, sq, sk, sv, gates = _in_projection(x, i, w_cat)
        y_a = _gdn_mixer(gq, gk, gv, gz, gates, i, *gdn)
        y_b = _fox_attention(fq, fk, fv, _fox_cumsum(gates, i, fox_bias))
        y_c = _conv_module(glu, i, *conf)
        y_d = _sb_attention(sq, sk, sv)
        x = _mix_and_memory_ln(x, y_a, y_b, y_c, y_d, _memory_kv(mem, i, w_kv), i, *mix_mem)
        x = _ffn_ln(x, i, *ffn2)
    return x
```

```python
import functools

import jax
import jax.numpy as jnp
from jax import lax
from jax.experimental import pallas as pl
from jax.experimental.pallas import tpu as pltpu

F32 = jnp.float32
BF16 = jnp.bfloat16

D_MODEL = 1024
DEPTH = 2
GROUP_WIDTH = 256
HEAD_DIM = 64
GROUP_HEADS = 4
D_FF = 2816
SHORT_CONV = 4
CONF_KERNEL = 31
CONF_GROUPS = 4
GDN_CHUNK = 64
MEM_HEADS = 4
MEM_HEAD_DIM = D_MODEL // MEM_HEADS
DN_ALPHA = float((2 * DEPTH) ** 0.25)
LN_EPS = 1e-5
RMS_EPS = 1e-6
L2_EPS = 1e-6
NEG_BIG = -1e30

GATE_LANES = 128
GATE_A0, GATE_B0, GATE_F0 = 0, 4, 8

TOKEN_TILE = 512
MXU_WIDTH = 256
SUBLANES = 8
FFN_SPLITS = (0, 6 * MXU_WIDTH, D_FF)
PAIR_WIDTH = 2 * HEAD_DIM
FOX_BLOCK = 512
SB_Q_BLOCK = 512
SB_K_BLOCK = 256
ATTN_SUBTILE = 256
MIX_TILE = 1024
ROW_SUBTILE = 512
SB_LOG_UNDERFLOW = -105.0
FOX_LOG_UNDERFLOW = -106.0
GDN_BLOCK = 256
CONV_BLOCK = 512
VMEM_LIMIT = 56 * 1024 * 1024


def _params(sem, vmem=VMEM_LIMIT):
    return pltpu.CompilerParams(dimension_semantics=sem, vmem_limit_bytes=vmem)


def _resident(shape):
    nd = len(shape)
    return pl.BlockSpec(shape, lambda *_: (0,) * nd, pipeline_mode=pl.Buffered(1))


def _layer_slab(stacked, layer):
    tail = stacked.shape[1:]
    return pl.BlockSpec((None,) + tail, lambda *_: (layer,) + (0,) * len(tail), pipeline_mode=pl.Buffered(1))


def _layer_norm(y, g, b):
    mu = jnp.mean(y, axis=-1, keepdims=True)
    d = y - mu
    var = jnp.mean(d * d, axis=-1, keepdims=True)
    return d * lax.rsqrt(var + LN_EPS) * g + b


def _sigmoid(x):
    return 1.0 / (1.0 + jnp.exp(-x))


def _silu(x):
    return x * _sigmoid(x)


def _softplus(x):
    return jnp.maximum(x, 0.0) + jnp.log(1.0 + jnp.exp(-jnp.abs(x)))


def _dot(a, b):
    return jnp.dot(a, b, preferred_element_type=F32)


def _dot_nt(a, b):
    return lax.dot_general(a, b, (((1,), (1,)), ((), ())), preferred_element_type=F32)


def _dot_tn(a, b):
    return lax.dot_general(a, b, (((0,), (0,)), ((), ())), preferred_element_type=F32)


def _mask_dot(mask, x):
    n = x.shape[1]
    hi = x.astype(BF16)
    r1 = x - hi.astype(F32)
    mid = r1.astype(BF16)
    lo = (r1 - mid.astype(F32)).astype(BF16)
    r = _dot(mask.astype(BF16), jnp.concatenate([hi, mid, lo], axis=1))
    return r[:, :n] + r[:, n:2 * n] + r[:, 2 * n:]


def _ffn_kernel(x_ref, wg_ref, wu_ref, wd_ref, g_ref, b_ref, o_ref):
    x = x_ref[0]
    xb = x.astype(BF16)
    acc = None
    for lo, hi in zip(FFN_SPLITS[:-1], FFN_SPLITS[1:]):
        h = _dot(xb, wg_ref[:, lo:hi].astype(BF16))
        u = _dot(xb, wu_ref[:, lo:hi].astype(BF16))
        a = (_silu(h) * u).astype(BF16)
        part = _dot(a, wd_ref[lo:hi, :].astype(BF16))
        acc = part if acc is None else acc + part
    o_ref[0] = _layer_norm(DN_ALPHA * x + 0.5 * acc, g_ref[...], b_ref[...])


def _ffn_ln(x, layer, wg, wu, wd, g, b):
    bsz, t, d = x.shape
    tm = min(TOKEN_TILE, t)
    row = pl.BlockSpec((1, tm, d), lambda i, j: (i, j, 0))
    return pl.pallas_call(
        _ffn_kernel,
        grid=(bsz, t // tm),
        in_specs=[row] + [_layer_slab(a, layer) for a in (wg, wu, wd, g, b)],
        out_specs=row,
        out_shape=jax.ShapeDtypeStruct(x.shape, F32),
        compiler_params=_params(("parallel", "parallel")),
        name="ffn_ln",
    )(x, wg, wu, wd, g, b)


def _inproj_kernel(x_ref, w_ref, gq_ref, gk_ref, gv_ref, gz_ref, fq_ref, fk_ref, fv_ref,
                   glu_ref, sq_ref, sk_ref, sv_ref, gate_ref):
    xb = x_ref[0].astype(BF16)
    gw = GROUP_WIDTH

    def group(i, width=gw):
        return _dot(xb, w_ref[:, i * gw:i * gw + width])

    qk_scale = HEAD_DIM ** -0.5
    gq_ref[0] = group(0)
    gk_ref[0] = group(1)
    gv_ref[0] = group(2)
    gz_ref[0] = group(3)
    fq_ref[0] = (group(4) * qk_scale).astype(BF16)
    fk_ref[0] = group(5).astype(BF16)
    fv_ref[0] = group(6).astype(BF16)
    glu_ref[0] = group(7) * _sigmoid(group(8))
    sq_ref[0] = (group(9) * qk_scale).astype(BF16)
    sk_ref[0] = group(10).astype(BF16)
    sv_ref[0] = group(11).astype(BF16)
    gate_ref[0] = group(12, GATE_LANES)


def _in_projection(x, layer, w_cat):
    bsz, t, d = x.shape
    tm = min(TOKEN_TILE, t)
    row = lambda c: pl.BlockSpec((1, tm, c), lambda i, j: (i, j, 0))
    wide = lambda dt: jax.ShapeDtypeStruct((bsz, t, GROUP_WIDTH), dt)
    out_shape = ([wide(F32)] * 4 + [wide(BF16)] * 3 + [wide(F32)] + [wide(BF16)] * 3
                 + [jax.ShapeDtypeStruct((bsz, t, GATE_LANES), F32)])
    out_specs = [row(GROUP_WIDTH)] * 11 + [row(GATE_LANES)]
    return pl.pallas_call(
        _inproj_kernel,
        grid=(bsz, t // tm),
        in_specs=[row(d), _layer_slab(w_cat, layer)],
        out_specs=out_specs,
        out_shape=out_shape,
        compiler_params=_params(("parallel", "parallel")),
        name="in_projection",
    )(x, w_cat)


def _fox_cum_kernel(gate_ref, bias_ref, o_ref, carry_ref):
    @pl.when(pl.program_id(1) == 0)
    def _():
        carry_ref[...] = jnp.zeros_like(carry_ref)

    tb = gate_ref.shape[1]
    logit = gate_ref[0] + bias_ref[...]
    log_f = -_softplus(-logit)
    rows = lax.broadcasted_iota(jnp.int32, (tb, tb), 0)
    cols = lax.broadcasted_iota(jnp.int32, (tb, tb), 1)
    cum = _mask_dot(cols <= rows, log_f) + carry_ref[...]
    carry_ref[...] = cum[tb - 1:tb, :]
    o_ref[0, 0] = cum.T


def _fox_cumsum(gates, layer, bias_rows):
    bsz, t, _ = gates.shape
    tb = min(FOX_BLOCK, t)
    return pl.pallas_call(
        _fox_cum_kernel,
        grid=(bsz, t // tb),
        in_specs=[pl.BlockSpec((1, tb, GATE_LANES), lambda i, j: (i, j, 0)), _layer_slab(bias_rows, layer)],
        out_specs=pl.BlockSpec((1, 1, GATE_LANES, tb), lambda i, j: (i, j, 0, 0)),
        out_shape=jax.ShapeDtypeStruct((bsz, t // tb, GATE_LANES, tb), F32),
        scratch_shapes=[pltpu.VMEM((1, GATE_LANES), F32)],
        compiler_params=_params(("parallel", "arbitrary")),
        name="fox_cumsum",
    )(gates, bias_rows)


def _causal_masks(blk):
    rows = lax.broadcasted_iota(jnp.int32, (blk, blk), 0)
    cols = lax.broadcasted_iota(jnp.int32, (blk, blk), 1)
    return cols <= rows, cols < rows


def _stack_heads(x2):
    lane = lax.broadcasted_iota(jnp.int32, x2.shape, 1)
    zero = jnp.zeros_like(x2)
    return jnp.concatenate([jnp.where(lane < HEAD_DIM, x2, zero), jnp.where(lane >= HEAD_DIM, x2, zero)], axis=0)


def _unstack_heads(y, rows):
    lane = lax.broadcasted_iota(jnp.int32, (rows, PAIR_WIDTH), 1)
    return jnp.where(lane < HEAD_DIM, y[:rows], y[rows:])


def _skewed(tiles, stages):
    tiles = list(tiles)
    for t in range(len(tiles) + len(stages) - 1):
        for k in reversed(range(len(stages))):
            if 0 <= t - k < len(tiles):
                stages[k](tiles[t - k])


def _fox_kernel(q_ref, k_ref, v_ref, ck_ref, o_ref, kmax_ref, *, sub):
    blk = q_ref.shape[1]
    n_pairs = q_ref.shape[2] // PAIR_WIDTH
    qi = pl.program_id(1)
    lanes = lambda p: slice(p * PAIR_WIDTH, (p + 1) * PAIR_WIDTH)

    @pl.when(qi == 0)
    def _():
        for p in range(n_pairs):
            def widest(c, best):
                kk = k_ref[0, pl.ds(pl.multiple_of(c * blk, blk), blk), lanes(p)].astype(F32)
                return jnp.maximum(best, jnp.sum(kk * kk, axis=-1, keepdims=True))
            best = lax.fori_loop(0, k_ref.shape[1] // blk, widest, jnp.zeros((blk, 1), F32))
            kmax_ref[p] = jnp.sqrt(jnp.max(best, axis=0, keepdims=True))

    qs = jnp.concatenate([_stack_heads(q_ref[0, :, lanes(p)]) for p in range(n_pairs)], axis=0)
    n_sub = 2 * n_pairs * blk // sub
    tile = lambda i: slice(i * sub, (i + 1) * sub)
    head_of = lambda i: (i * sub) // blk
    qf = qs.astype(F32)
    qnorm = jnp.sqrt(jnp.sum(qf * qf, axis=-1, keepdims=True))
    qk_bound = jnp.concatenate([qnorm[2 * p * blk:2 * (p + 1) * blk] * kmax_ref[p] for p in range(n_pairs)],
                               axis=0)
    rows = lax.broadcasted_iota(jnp.int32, (sub, blk), 0)
    cols = lax.broadcasted_iota(jnp.int32, (sub, blk), 1)
    ones = jnp.ones((blk, PAIR_WIDTH), BF16)

    def step(j, carry, masked):
        ms, accs = carry
        start = pl.multiple_of(j * blk, blk)
        kb = [k_ref[0, pl.ds(start, blk), lanes(p)] for p in range(n_pairs)]
        vb = [jnp.concatenate([v_ref[0, pl.ds(start, blk), lanes(p)], ones], axis=1) for p in range(n_pairs)]
        ms, accs = list(ms), list(accs)
        s, p, alpha = [None] * n_sub, [None] * n_sub, [None] * n_sub

        def logits(i):
            head, off = divmod(i * sub, blk)
            si = _dot_nt(qs[tile(i)], kb[head // 2]) - ck_ref[0, j, head:head + 1, :]
            s[i] = jnp.where(cols <= rows + off, si, NEG_BIG) if masked else si

        def probs(i):
            m_new = jnp.maximum(ms[i], jnp.max(s[i], axis=-1, keepdims=True))
            p[i] = jnp.exp(s[i] - m_new).astype(BF16)
            alpha[i] = jnp.exp(ms[i] - m_new)
            ms[i] = m_new

        def values(i):
            accs[i] = alpha[i] * accs[i] + _dot(p[i], vb[head_of(i) // 2])

        _skewed(range(n_sub), [logits, probs, values])
        return tuple(ms), tuple(accs)

    def headroom(j, ms):
        room = None
        for i in range(n_sub):
            last = -ck_ref[0, j, head_of(i):head_of(i) + 1, :][:, blk - 1:blk]
            r = jnp.max(qk_bound[tile(i)] - ms[i] + last)
            room = r if room is None else jnp.maximum(room, r)
        return room

    init = (tuple(jnp.full((sub, 1), NEG_BIG, F32) for _ in range(n_sub)),
            tuple(jnp.zeros((sub, 2 * PAIR_WIDTH), F32) for _ in range(n_sub)))
    carry = step(qi, init, True)

    def more(state):
        i, room, _ = state
        return (i < qi) & (room > FOX_LOG_UNDERFLOW)

    def sweep(state):
        i, _, c = state
        c = step(qi - 1 - i, c, False)
        return i + 1, headroom(jnp.maximum(qi - 2 - i, 0), c[0]), c

    _, _, (_, accs) = lax.while_loop(more, sweep, (jnp.int32(0), headroom(jnp.maximum(qi - 1, 0), carry[0]), carry))
    acc = jnp.concatenate(accs, axis=0)
    out = acc[:, :PAIR_WIDTH] / acc[:, PAIR_WIDTH:PAIR_WIDTH + 1]
    o_ref[0] = jnp.concatenate([_unstack_heads(out[2 * p * blk:2 * (p + 1) * blk], blk) for p in range(n_pairs)],
                               axis=1)


def _fox_attention(q, k, v, cum_t):
    bsz, t, width = q.shape
    blk = min(FOX_BLOCK, t)
    nk = t // blk
    whole = pl.BlockSpec((1, t, width), lambda b, i: (b, 0, 0))
    qblk = pl.BlockSpec((1, blk, width), lambda b, i: (b, i, 0))
    return pl.pallas_call(
        functools.partial(_fox_kernel, sub=min(ATTN_SUBTILE, blk)),
        grid=(bsz, nk),
        in_specs=[qblk, whole, whole,
                  pl.BlockSpec((1, nk, 8, blk), lambda b, i: (b, 0, GATE_F0 // 8, 0))],
        out_specs=qblk,
        out_shape=jax.ShapeDtypeStruct((bsz, t, width), F32),
        scratch_shapes=[pltpu.VMEM((width // PAIR_WIDTH, 1, 1), F32)],
        compiler_params=_params(("arbitrary", "arbitrary")),
        name="fox_attention",
    )(q, k, v, cum_t)


def _sb_kernel(q_ref, k_ref, v_ref, o_ref, *, tk, sub):
    tq = q_ref.shape[1]
    n_pairs = q_ref.shape[2] // PAIR_WIDTH
    per = tq // tk
    qi = pl.program_id(1)
    lanes = lambda p: slice(p * PAIR_WIDTH, (p + 1) * PAIR_WIDTH)
    qs = jnp.concatenate([_stack_heads(q_ref[0, :, lanes(p)]) for p in range(n_pairs)], axis=0)
    n_sub = 2 * n_pairs * tq // sub
    pair_of = lambda i: (i * sub) // (2 * tq)
    rows = lax.broadcasted_iota(jnp.int32, (sub, tk), 0)
    cols = lax.broadcasted_iota(jnp.int32, (sub, tk), 1)
    suffix = _causal_masks(tk)[0].astype(BF16)

    def step(j, carry, diag):
        rests, accs = carry
        start = pl.multiple_of(j * tk, tk)
        kb = [k_ref[0, pl.ds(start, tk), lanes(p)] for p in range(n_pairs)]
        vb = [v_ref[0, pl.ds(start, tk), lanes(p)] for p in range(n_pairs)]
        rests, accs = list(rests), list(accs)
        z, split, w, strict = [None] * n_sub, [None] * n_sub, [None] * n_sub, [None] * n_sub

        def logits(i):
            z[i] = _dot_nt(qs[i * sub:(i + 1) * sub], kb[pair_of(i)])
            if diag is not None:
                strict[i] = cols + diag < rows + (i * sub) % tq

        def keep(i):
            log_keep = -_softplus(z[i])
            if diag is not None:
                log_keep = jnp.where(strict[i], log_keep, 0.0)
            split[i] = jnp.concatenate(_split_bf16(log_keep), axis=0)

        def weights(i):
            tails = _dot(split[i], suffix)
            tail = tails[:sub] + tails[sub:]
            wi = jnp.exp(z[i] + tail + rests[i])
            if diag is not None:
                wi = jnp.where(strict[i], wi, 0.0)
            w[i] = wi.astype(BF16)
            rests[i] = rests[i] + tail[:, 0:1]

        def values(i):
            accs[i] = accs[i] + _dot(w[i], vb[pair_of(i)])

        live = [i for i in range(n_sub) if diag is None or diag < (i * sub) % tq + sub - 1]
        _skewed(live, [logits, keep, weights, values])
        return tuple(rests), tuple(accs)

    carry = (tuple(jnp.zeros((sub, 1), F32) for _ in range(n_sub)),
             tuple(jnp.zeros((sub, PAIR_WIDTH), F32) for _ in range(n_sub)))
    for d in reversed(range(per)):
        carry = step(qi * per + d, carry, d * tk)

    def largest(rests):
        return functools.reduce(jnp.maximum, [jnp.max(r) for r in rests])

    def more(state):
        i, top, _ = state
        return (i < qi * per) & (top > SB_LOG_UNDERFLOW)

    def sweep(state):
        i, _, c = state
        c = step(qi * per - 1 - i, c, None)
        return i + 1, largest(c[0]), c

    _, _, (_, accs) = lax.while_loop(more, sweep, (jnp.int32(0), largest(carry[0]), carry))
    acc = jnp.concatenate(accs, axis=0)
    o_ref[0] = jnp.concatenate([_unstack_heads(acc[2 * p * tq:2 * (p + 1) * tq], tq) for p in range(n_pairs)],
                               axis=1)


def _sb_attention(q, k, v):
    bsz, t, width = q.shape
    tq = min(SB_Q_BLOCK, t)
    tk = min(SB_K_BLOCK, tq)
    whole = pl.BlockSpec((1, t, width), lambda b, i: (b, 0, 0))
    qblk = pl.BlockSpec((1, tq, width), lambda b, i: (b, i, 0))
    return pl.pallas_call(
        functools.partial(_sb_kernel, tk=tk, sub=min(ATTN_SUBTILE, tq)),
        grid=(bsz, t // tq),
        in_specs=[qblk, whole, whole],
        out_specs=qblk,
        out_shape=jax.ShapeDtypeStruct((bsz, t, width), F32),
        compiler_params=_params(("parallel", "arbitrary")),
        name="sb_attention",
    )(q, k, v)


CONV_HALO = 32


def _conv_kernel(prev_ref, cur_ref, w_ref, b_ref, ng_ref, nb_ref, o_ref, xs_ref, sh_ref):
    tb = cur_ref.shape[1]
    first = pl.program_id(1) == 0
    xs_ref[0:CONV_HALO, :] = jnp.where(first, 0.0, prev_ref[0])
    xs_ref[CONV_HALO:CONV_HALO + tb, :] = cur_ref[0]
    span = tb + CONV_HALO - SUBLANES
    for r in range(1, SUBLANES):
        sh_ref[r, 0:span, :] = xs_ref[r:r + span, :]
    base = CONV_HALO - (CONF_KERNEL - 1)
    acc = jnp.zeros((tb, GROUP_WIDTH), F32) + b_ref[...]
    for tap in range(CONF_KERNEL):
        whole, r = divmod(base + tap, SUBLANES)
        src = xs_ref if r == 0 else sh_ref.at[r]
        acc = acc + w_ref[tap:tap + 1, :] * src[whole * SUBLANES:whole * SUBLANES + tb, :]
    gsz = GROUP_WIDTH // CONF_GROUPS
    parts = []
    for g in range(CONF_GROUPS):
        cg = acc[:, g * gsz:(g + 1) * gsz]
        mu = jnp.mean(cg, axis=-1, keepdims=True)
        d = cg - mu
        var = jnp.mean(d * d, axis=-1, keepdims=True)
        parts.append(d * lax.rsqrt(var + LN_EPS))
    hn = jnp.concatenate(parts, axis=-1) * ng_ref[...] + nb_ref[...]
    o_ref[0] = _silu(hn)


def _conv_module(glu, layer, w, b, ng, nb):
    bsz, t, c = glu.shape
    tb = min(CONV_BLOCK, t)
    per = tb // CONV_HALO
    return pl.pallas_call(
        _conv_kernel,
        grid=(bsz, t // tb),
        in_specs=[pl.BlockSpec((1, CONV_HALO, c), lambda i, j: (i, jnp.maximum(j * per - 1, 0), 0)),
                  pl.BlockSpec((1, tb, c), lambda i, j: (i, j, 0)),
                  ] + [_layer_slab(a, layer) for a in (w, b, ng, nb)],
        out_specs=pl.BlockSpec((1, tb, c), lambda i, j: (i, j, 0)),
        out_shape=jax.ShapeDtypeStruct(glu.shape, F32),
        scratch_shapes=[pltpu.VMEM((CONV_HALO + tb, c), F32), pltpu.VMEM((SUBLANES, CONV_HALO + tb, c), F32)],
        compiler_params=_params(("parallel", "parallel")),
        name="conv_module",
    )(glu, glu, w, b, ng, nb)


GDN_HALO = 8


def _split_bf16(x):
    hi = x.astype(BF16)
    return hi, (x - hi.astype(F32)).astype(BF16)


def _group_sum(x, ones_bd):
    rows = x.shape[0]
    hi, lo = _split_bf16(x)
    r = _dot(jnp.concatenate([hi, lo], axis=0), ones_bd)
    return r[:rows] + r[rows:]


def _odd_blocks(x, size):
    return jnp.concatenate([x[r:r + size] for r in range(size, x.shape[0], 2 * size)], axis=0)


def _spread_odd_blocks(y, size, fill):
    parts = []
    for b in range(y.shape[0] // size):
        parts += [fill[2 * b * size:(2 * b + 1) * size], y[b * size:(b + 1) * size]]
    return jnp.concatenate(parts, axis=0)


def _unit_lower_inverses(lows, rows, cols):
    n = lows[0].shape[0]
    eye = jnp.where(rows == cols, 1.0, 0.0)
    first = (rows % 2 == 1) & (cols == rows - 1)
    invs = [eye - jnp.where(first, low, 0.0) for low in lows]
    size = 2
    while size < GDN_CHUNK:
        rb = rows // size
        level = (rb % 2 == 1) & (cols // size == rb - 1)
        odd_only = size % SUBLANES == 0
        pick = (lambda x: _odd_blocks(x, size)) if odd_only else (lambda x: x)
        splits = [_split_bf16(inv) for inv in invs]
        xs = []
        for low, (d_hi, d_lo) in zip(lows, splits):
            x2 = _dot(pick(jnp.where(level, low, 0.0)).astype(BF16), jnp.concatenate([d_hi, d_lo], axis=1))
            x = x2[:, :n] + x2[:, n:]
            if odd_only:
                x = _spread_odd_blocks(x, size, jnp.zeros((n, n), F32))
            xs.append(_split_bf16(x))
        nxt = []
        for inv, (d_hi, d_lo), (x_hi, x_lo) in zip(invs, splits, xs):
            m = n // 2 if odd_only else n
            lhs = jnp.concatenate([pick(inv).astype(BF16), pick(inv - d_hi.astype(F32)).astype(BF16)], axis=0) \
                if odd_only else jnp.concatenate([d_hi, d_lo], axis=0)
            y4 = _dot(lhs, jnp.concatenate([x_hi, x_lo], axis=1))
            y = y4[:m, :n] + y4[:m, n:] + y4[m:, :n] + y4[m:, n:]
            nxt.append(_spread_odd_blocks(pick(inv) - y, size, inv) if odd_only else inv - y)
        invs = nxt
        size *= 2
    return invs


def _gdn_kernel(qp_ref, kp_ref, vp_ref, q_ref, k_ref, v_ref, z_ref, gate_ref, cw_ref, alog_ref, dtb_ref,
                ng_ref, ones_ref, o_ref, xs_ref, y_ref, state_ref):
    nb, tb = q_ref.shape[0], q_ref.shape[1]
    c = GDN_CHUNK
    n = 2 * c
    gw = GROUP_WIDTH
    n_pairs = GROUP_HEADS // 2
    first = pl.program_id(0) == 0

    @pl.when(first)
    def _():
        state_ref[...] = jnp.zeros_like(state_ref)

    ones_bd = ones_ref[...]
    base = GDN_HALO - (SHORT_CONV - 1)

    def conv_silu(idx, p_ref, c_ref, b):
        xs_ref[0:GDN_HALO, :] = jnp.where(first, 0.0, p_ref[b])
        xs_ref[GDN_HALO:GDN_HALO + tb, :] = c_ref[b]
        acc = None
        for tap in range(SHORT_CONV):
            term = cw_ref[tap:tap + 1, idx * gw:(idx + 1) * gw] * xs_ref[base + tap:base + tap + tb, :]
            acc = term if acc is None else acc + term
        return _silu(acc)

    def l2_normalize(y):
        return y * lax.rsqrt(_group_sum(y * y, ones_bd) + L2_EPS)

    rows_t = lax.broadcasted_iota(jnp.int32, (tb, tb), 0)
    cols_t = lax.broadcasted_iota(jnp.int32, (tb, tb), 1)
    chunk_tri = (cols_t <= rows_t) & (cols_t // c == rows_t // c)
    lane_bcast = lambda x, lane: jnp.broadcast_to(x[:, lane:lane + 1], (tb, PAIR_WIDTH))
    low_half = lax.broadcasted_iota(jnp.int32, (tb, PAIR_WIDTH), 1) < HEAD_DIM
    rows = lax.broadcasted_iota(jnp.int32, (n, n), 0)
    cols = lax.broadcasted_iota(jnp.int32, (n, n), 1)
    same_head = rows // c == cols // c
    lower_incl = same_head & (cols <= rows)
    strict_lower = same_head & (cols < rows)

    pairs, g_wides = {}, {}
    for b in range(nb):
        qn = l2_normalize(conv_silu(0, qp_ref, q_ref, b)) * (HEAD_DIM ** -0.5)
        kn = l2_normalize(conv_silu(1, kp_ref, k_ref, b))
        vn = conv_silu(2, vp_ref, v_ref, b)
        gates = gate_ref[b]
        log_decay = -jnp.exp(alog_ref[...]) * _softplus(gates + dtb_ref[...])
        beta_all = _sigmoid(gates)
        gcum = _mask_dot(chunk_tri, log_decay)
        g_wide = [lane_bcast(gcum, GATE_A0 + h) for h in range(GROUP_HEADS)]
        b_wide = [lane_bcast(beta_all, GATE_B0 + h) for h in range(GROUP_HEADS)]
        g_wides[b] = g_wide
        for p in range(n_pairs):
            ls = slice(p * PAIR_WIDTH, (p + 1) * PAIR_WIDTH)
            g_nat = jnp.where(low_half, g_wide[2 * p], g_wide[2 * p + 1])
            b_nat = jnp.where(low_half, b_wide[2 * p], b_wide[2 * p + 1])
            eg = jnp.exp(g_nat)
            k_beta = kn[:, ls] * b_nat
            pairs[b, p] = dict(ls=ls, g_nat=g_nat, q=qn[:, ls], k=kn[:, ls], q_dec=qn[:, ls] * eg, k_beta=k_beta,
                               v_beta=vn[:, ls] * b_nat, kb_eg=k_beta * eg)

    systems = [(ci, b, p) for ci in range(tb // c) for b in range(nb) for p in range(n_pairs)]
    chunk_rows = lambda ci: slice(ci * c, (ci + 1) * c)

    lkks, a_qks = [], []
    for ci, b, p in systems:
        d, r, g_wide = pairs[b, p], chunk_rows(ci), g_wides[b]
        g_col = jnp.concatenate([g_wide[2 * p][r], g_wide[2 * p + 1][r]], axis=0)
        decay = jnp.exp(jnp.where(lower_incl, g_col - g_col.T, -jnp.inf))
        k_st = _stack_heads(d["k"][r]).astype(BF16)
        twice = lambda x: jnp.concatenate([x, x], axis=0)
        lhs = jnp.concatenate([twice(d["k_beta"][r]), twice(d["q"][r])], axis=0).astype(BF16)
        gram = _dot_nt(lhs, k_st)
        lkks.append(jnp.where(strict_lower, gram[:n] * decay, 0.0))
        a_qks.append((gram[n:] * decay).astype(BF16))
    t_invs = _unit_lower_inverses(lkks, rows, cols)
    uws = []
    for (ci, b, p), t_inv in zip(systems, t_invs):
        d, r = pairs[b, p], chunk_rows(ci)
        rhs = jnp.concatenate([_stack_heads(d["v_beta"][r]), _stack_heads(d["kb_eg"][r])], axis=1)
        uws.append(_dot(t_inv.astype(BF16), rhs.astype(BF16)))

    states = {(b, p): state_ref[b * n_pairs + p] for b in range(nb) for p in range(n_pairs)}
    for idx, (ci, b, p) in enumerate(systems):
        d, r = pairs[b, p], chunk_rows(ci)
        uw, state = uws[idx], states[b, p]
        g_last = d["g_nat"][ci * c + c - 1:ci * c + c, :]
        k_dec = _stack_heads(d["k"][r] * jnp.exp(g_last - d["g_nat"][r])).astype(BF16)
        wq = _dot(jnp.concatenate([uw[:, n:].astype(BF16), _stack_heads(d["q_dec"][r]).astype(BF16)], axis=0),
                  state.astype(BF16))
        vnb = (uw[:, :n] - wq[:n]).astype(BF16)
        o_st = wq[n:] + _dot(a_qks[idx], vnb)
        states[b, p] = state * jnp.exp(g_last) + _dot_tn(k_dec, vnb)
        y_ref[b, r, d["ls"]] = o_st[:c] + o_st[c:]
    for (b, p), state in states.items():
        state_ref[b * n_pairs + p] = state

    for b in range(nb):
        o = y_ref[b]
        mean_sq = _group_sum(o * o, ones_bd) * (1.0 / HEAD_DIM)
        o_ref[b] = o * lax.rsqrt(mean_sq + RMS_EPS) * ng_ref[...] * _silu(z_ref[b])


def _gdn_mixer(q, k, v, z, gates, layer, conv_w, a_log_rows, dt_bias_rows, norm_g_rows, head_ones):
    bsz, t, width = q.shape
    nh = GROUP_HEADS
    tb = min(GDN_BLOCK, t)
    per = tb // GDN_HALO
    cur = pl.BlockSpec((bsz, tb, width), lambda j: (0, j, 0))
    prev = pl.BlockSpec((bsz, GDN_HALO, width), lambda j: (0, jnp.maximum(j * per - 1, 0), 0))
    return pl.pallas_call(
        _gdn_kernel,
        grid=(t // tb,),
        in_specs=[prev, prev, prev, cur, cur, cur, cur,
                  pl.BlockSpec((bsz, tb, GATE_LANES), lambda j: (0, j, 0)),
                  ] + [_layer_slab(a, layer) for a in (conv_w, a_log_rows, dt_bias_rows, norm_g_rows)]
                 + [_resident((width, width))],
        out_specs=cur,
        out_shape=jax.ShapeDtypeStruct(q.shape, F32),
        scratch_shapes=[pltpu.VMEM((GDN_HALO + tb, width), F32), pltpu.VMEM((bsz, tb, width), F32),
                        pltpu.VMEM((bsz * nh // 2, PAIR_WIDTH, PAIR_WIDTH), F32)],
        compiler_params=_params(("arbitrary",)),
        name="gdn_mixer",
    )(q, k, v, q, k, v, z, gates, conv_w, a_log_rows, dt_bias_rows, norm_g_rows, head_ones)


def _memkv_kernel(m_ref, w_ref, o_ref):
    o_ref[0] = _dot(m_ref[0].astype(BF16), w_ref[...].astype(BF16)).astype(o_ref.dtype)


def _memory_kv(mem, layer, w_kv):
    bsz, m, d = mem.shape
    return pl.pallas_call(
        _memkv_kernel,
        grid=(bsz,),
        in_specs=[pl.BlockSpec((1, m, d), lambda i: (i, 0, 0)), _layer_slab(w_kv, layer)],
        out_specs=pl.BlockSpec((1, m, 2 * d), lambda i: (i, 0, 0)),
        out_shape=jax.ShapeDtypeStruct((bsz, m, 2 * d), BF16),
        compiler_params=_params(("parallel",)),
        name="memory_kv",
    )(mem, w_kv)


def _mix_mem_kernel(x_ref, ya_ref, yb_ref, yc_ref, yd_ref, kv_ref, wout_ref, g1_ref, b1_ref, wq_ref, wo_ref,
                    g2_ref, b2_ref, o_ref, *, sub):
    gw = GROUP_WIDTH
    hd = MEM_HEAD_DIM
    n_sub = x_ref.shape[1] // sub
    tile = lambda i: slice(i * sub, (i + 1) * sub)
    x1, att = [None] * n_sub, [None] * n_sub

    def mix(i):
        tot = None
        for group, y_ref in enumerate((ya_ref, yb_ref, yc_ref, yd_ref)):
            part = _dot(y_ref[0, tile(i), :].astype(BF16), wout_ref[group * gw:(group + 1) * gw, :].astype(BF16))
            tot = part if tot is None else tot + part
        x1[i] = _layer_norm(DN_ALPHA * x_ref[0, tile(i), :] + tot, g1_ref[...], b1_ref[...])

    def attend(i):
        q = _dot(x1[i].astype(BF16), wq_ref[...].astype(BF16))
        outs = []
        for h in range(MEM_HEADS):
            qh = (q[:, h * hd:(h + 1) * hd] * (hd ** -0.5)).astype(BF16)
            kh = kv_ref[0, :, h * hd:(h + 1) * hd]
            vh = kv_ref[0, :, D_MODEL + h * hd:D_MODEL + (h + 1) * hd]
            s = _dot_nt(qh, kh)
            p = jnp.exp(s - jnp.max(s, axis=-1, keepdims=True))
            p = p / jnp.sum(p, axis=-1, keepdims=True)
            outs.append(_dot(p.astype(BF16), vh).astype(BF16))
        att[i] = jnp.concatenate(outs, axis=-1)

    def project(i):
        y = _dot(att[i], wo_ref[...].astype(BF16))
        o_ref[0, tile(i), :] = _layer_norm(DN_ALPHA * x1[i] + y, g2_ref[...], b2_ref[...])

    _skewed(range(n_sub), [mix, attend, project])


def _mix_and_memory_ln(x, ya, yb, yc, yd, kv, layer, w_out, g_mix, b_mix, wq, wo, g_mem, b_mem):
    bsz, t, d = x.shape
    tm = min(MIX_TILE, t)
    row = lambda c: pl.BlockSpec((1, tm, c), lambda i, j: (i, j, 0))
    return pl.pallas_call(
        functools.partial(_mix_mem_kernel, sub=min(ROW_SUBTILE, tm)),
        grid=(bsz, t // tm),
        in_specs=[row(d)] + [row(GROUP_WIDTH)] * 4 + [pl.BlockSpec((1, kv.shape[1], 2 * d), lambda i, j: (i, 0, 0))]
                 + [_layer_slab(a, layer) for a in (w_out, g_mix, b_mix, wq, wo, g_mem, b_mem)],
        out_specs=row(d),
        out_shape=jax.ShapeDtypeStruct(x.shape, F32),
        compiler_params=_params(("parallel", "parallel")),
        name="mix_and_memory_ln",
    )(x, ya, yb, yc, yd, kv, w_out, g_mix, b_mix, wq, wo, g_mem, b_mem)


def _combined_in_weights(w_in):
    gw, nh = GROUP_WIDTH, GROUP_HEADS
    o = 0
    gdn_qkv = w_in[..., o:o + 3 * gw]; o += 3 * gw
    gdn_z = w_in[..., o:o + gw]; o += gw
    gdn_a = w_in[..., o:o + nh]; o += nh
    gdn_b = w_in[..., o:o + nh]; o += nh
    fox_qkv = w_in[..., o:o + 3 * gw]; o += 3 * gw
    fox_f = w_in[..., o:o + nh]; o += nh
    conf = w_in[..., o:o + 2 * gw]; o += 2 * gw
    sb_qkv = w_in[..., o:o + 3 * gw]
    pad = jnp.zeros(w_in.shape[:-1] + (GATE_LANES - 3 * nh,), w_in.dtype)
    return jnp.concatenate([gdn_qkv, gdn_z, fox_qkv, conf, sb_qkv, gdn_a, gdn_b, fox_f, pad], axis=-1).astype(BF16)


def _gate_rows(vals, first_lane):
    depth, nh = vals.shape
    return jnp.pad(vals, ((0, 0), (first_lane, GATE_LANES - first_lane - nh))).reshape(depth, 1, GATE_LANES)


def kernel(x, mem, ffn1_w_gate, ffn1_w_up, ffn1_w_down, ln_ffn1_g, ln_ffn1_b, w_in, gdn_conv_w, gdn_a_log, gdn_dt_bias, gdn_norm_g, fox_b_f, conf_dw_w, conf_dw_b, conf_norm_g, conf_norm_b, w_out, ln_mix_g, ln_mix_b, mem_w_q, mem_w_kv, mem_w_o, ln_mem_g, ln_mem_b, ffn2_w_gate, ffn2_w_up, ffn2_w_down, ln_ffn2_g, ln_ffn2_b):
    rows = lambda a: a.reshape(a.shape[0], 1, a.shape[1])
    ffn1 = (ffn1_w_gate, ffn1_w_up, ffn1_w_down, rows(ln_ffn1_g), rows(ln_ffn1_b))
    ffn2 = (ffn2_w_gate, ffn2_w_up, ffn2_w_down, rows(ln_ffn2_g), rows(ln_ffn2_b))
    w_cat = _combined_in_weights(w_in)
    gdn = (gdn_conv_w, _gate_rows(gdn_a_log, GATE_A0), _gate_rows(gdn_dt_bias, GATE_A0),
           rows(jnp.tile(gdn_norm_g, (1, GROUP_HEADS))),
           jnp.kron(jnp.eye(GROUP_HEADS, dtype=BF16), jnp.ones((HEAD_DIM, HEAD_DIM), BF16)))
    fox_bias = _gate_rows(fox_b_f, GATE_F0)
    conf = (conf_dw_w, rows(conf_dw_b), rows(conf_norm_g), rows(conf_norm_b))
    mix_mem = (w_out, rows(ln_mix_g), rows(ln_mix_b), mem_w_q, mem_w_o, rows(ln_mem_g), rows(ln_mem_b))
    w_kv = mem_w_kv

    for i in range(DEPTH):
        x = _ffn_ln(x, i, *ffn1)
        gq, gk, gv, gz, fq, fk, fv, glu, sq, sk, sv, gates = _in_projection(x, i, w_cat)
        y_a = _gdn_mixer(gq, gk, gv, gz, gates, i, *gdn)
        y_b = _fox_attention(fq, fk, fv, _fox_cumsum(gates, i, fox_bias))
        y_c = _conv_module(glu, i, *conf)
        y_d = _sb_attention(sq, sk, sv)
        x = _mix_and_memory_ln(x, y_a, y_b, y_c, y_d, _memory_kv(mem, i, w_kv), i, *mix_mem)
        x = _ffn_ln(x, i, *ffn2)
    return x
```

```python
import functools

import jax
import jax.numpy as jnp
from jax import lax
from jax.experimental import pallas as pl
from jax.experimental.pallas import tpu as pltpu

F32 = jnp.float32
BF16 = jnp.bfloat16

D_MODEL = 1024
DEPTH = 2
GROUP_WIDTH = 256
HEAD_DIM = 64
GROUP_HEADS = 4
D_FF = 2816
SHORT_CONV = 4
CONF_KERNEL = 31
CONF_GROUPS = 4
GDN_CHUNK = 64
MEM_HEADS = 4
MEM_HEAD_DIM = D_MODEL // MEM_HEADS
DN_ALPHA = float((2 * DEPTH) ** 0.25)
LN_EPS = 1e-5
RMS_EPS = 1e-6
L2_EPS = 1e-6
NEG_BIG = -1e30

GATE_LANES = 128
GATE_A0, GATE_B0, GATE_F0 = 0, 4, 8

TOKEN_TILE = 512
MXU_WIDTH = 256
SUBLANES = 8
FFN_SPLITS = (0, 6 * MXU_WIDTH, D_FF)
PAIR_WIDTH = 2 * HEAD_DIM
FOX_BLOCK = 512
SB_Q_BLOCK = 512
SB_K_BLOCK = 256
ATTN_SUBTILE = 256
MIX_TILE = 1024
ROW_SUBTILE = 512
SB_LOG_UNDERFLOW = -105.0
FOX_LOG_UNDERFLOW = -106.0
GDN_BLOCK = 256
CONV_BLOCK = 512
VMEM_LIMIT = 56 * 1024 * 1024


def _params(sem, vmem=VMEM_LIMIT):
    return pltpu.CompilerParams(dimension_semantics=sem, vmem_limit_bytes=vmem)


def _resident(shape):
    nd = len(shape)
    return pl.BlockSpec(shape, lambda *_: (0,) * nd, pipeline_mode=pl.Buffered(1))


def _layer_slab(stacked, layer):
    tail = stacked.shape[1:]
    return pl.BlockSpec((None,) + tail, lambda *_: (layer,) + (0,) * len(tail), pipeline_mode=pl.Buffered(1))


def _layer_norm(y, g, b):
    mu = jnp.mean(y, axis=-1, keepdims=True)
    d = y - mu
    var = jnp.mean(d * d, axis=-1, keepdims=True)
    return d * lax.rsqrt(var + LN_EPS) * g + b


def _sigmoid(x):
    return 1.0 / (1.0 + jnp.exp(-x))


def _silu(x):
    return x * _sigmoid(x)


def _softplus(x):
    return jnp.maximum(x, 0.0) + jnp.log(1.0 + jnp.exp(-jnp.abs(x)))


def _dot(a, b):
    return jnp.dot(a, b, preferred_element_type=F32)


def _dot_nt(a, b):
    return lax.dot_general(a, b, (((1,), (1,)), ((), ())), preferred_element_type=F32)


def _dot_tn(a, b):
    return lax.dot_general(a, b, (((0,), (0,)), ((), ())), preferred_element_type=F32)


def _mask_dot(mask, x):
    n = x.shape[1]
    hi = x.astype(BF16)
    r1 = x - hi.astype(F32)
    mid = r1.astype(BF16)
    lo = (r1 - mid.astype(F32)).astype(BF16)
    r = _dot(mask.astype(BF16), jnp.concatenate([hi, mid, lo], axis=1))
    return r[:, :n] + r[:, n:2 * n] + r[:, 2 * n:]


def _ffn_kernel(x_ref, wg_ref, wu_ref, wd_ref, g_ref, b_ref, o_ref):
    x = x_ref[0]
    xb = x.astype(BF16)
    acc = None
    for lo, hi in zip(FFN_SPLITS[:-1], FFN_SPLITS[1:]):
        h = _dot(xb, wg_ref[:, lo:hi].astype(BF16))
        u = _dot(xb, wu_ref[:, lo:hi].astype(BF16))
        a = (_silu(h) * u).astype(BF16)
        part = _dot(a, wd_ref[lo:hi, :].astype(BF16))
        acc = part if acc is None else acc + part
    o_ref[0] = _layer_norm(DN_ALPHA * x + 0.5 * acc, g_ref[...], b_ref[...])


def _ffn_ln(x, layer, wg, wu, wd, g, b):
    bsz, t, d = x.shape
    tm = min(TOKEN_TILE, t)
    row = pl.BlockSpec((1, tm, d), lambda i, j: (i, j, 0))
    return pl.pallas_call(
        _ffn_kernel,
        grid=(bsz, t // tm),
        in_specs=[row] + [_layer_slab(a, layer) for a in (wg, wu, wd, g, b)],
        out_specs=row,
        out_shape=jax.ShapeDtypeStruct(x.shape, F32),
        compiler_params=_params(("parallel", "parallel")),
        name="ffn_ln",
    )(x, wg, wu, wd, g, b)


N_LEADING_GROUPS = 4


def _inproj_kernel(x_ref, wa_ref, wb_ref, gq_ref, gk_ref, gv_ref, gz_ref, fq_ref, fk_ref, fv_ref,
                   glu_ref, sq_ref, sk_ref, sv_ref, gate_ref):
    xb = x_ref[0].astype(BF16)
    gw = GROUP_WIDTH

    def group(i, width=gw):
        if i < N_LEADING_GROUPS:
            return _dot(xb, wa_ref[:, i * gw:i * gw + width].astype(BF16))
        i -= N_LEADING_GROUPS
        return _dot(xb, wb_ref[:, i * gw:i * gw + width])

    qk_scale = HEAD_DIM ** -0.5
    gq_ref[0] = group(0)
    gk_ref[0] = group(1)
    gv_ref[0] = group(2)
    gz_ref[0] = group(3)
    fq_ref[0] = (group(4) * qk_scale).astype(BF16)
    fk_ref[0] = group(5).astype(BF16)
    fv_ref[0] = group(6).astype(BF16)
    glu_ref[0] = group(7) * _sigmoid(group(8))
    sq_ref[0] = (group(9) * qk_scale).astype(BF16)
    sk_ref[0] = group(10).astype(BF16)
    sv_ref[0] = group(11).astype(BF16)
    gate_ref[0] = group(12, GATE_LANES)


def _in_projection(x, layer, w_in, w_rest):
    bsz, t, d = x.shape
    tm = min(TOKEN_TILE, t)
    lead = N_LEADING_GROUPS * GROUP_WIDTH
    leading = pl.BlockSpec((None, d, lead), lambda *_: (layer, 0, 0), pipeline_mode=pl.Buffered(1))
    row = lambda c: pl.BlockSpec((1, tm, c), lambda i, j: (i, j, 0))
    wide = lambda dt: jax.ShapeDtypeStruct((bsz, t, GROUP_WIDTH), dt)
    out_shape = ([wide(F32)] * 4 + [wide(BF16)] * 3 + [wide(F32)] + [wide(BF16)] * 3
                 + [jax.ShapeDtypeStruct((bsz, t, GATE_LANES), F32)])
    out_specs = [row(GROUP_WIDTH)] * 11 + [row(GATE_LANES)]
    return pl.pallas_call(
        _inproj_kernel,
        grid=(bsz, t // tm),
        in_specs=[row(d), leading, _layer_slab(w_rest, layer)],
        out_specs=out_specs,
        out_shape=out_shape,
        compiler_params=_params(("parallel", "parallel")),
        name="in_projection",
    )(x, w_in, w_rest)


def _fox_cum_kernel(gate_ref, bias_ref, o_ref, carry_ref):
    @pl.when(pl.program_id(1) == 0)
    def _():
        carry_ref[...] = jnp.zeros_like(carry_ref)

    tb = gate_ref.shape[1]
    logit = gate_ref[0] + bias_ref[...]
    log_f = -_softplus(-logit)
    rows = lax.broadcasted_iota(jnp.int32, (tb, tb), 0)
    cols = lax.broadcasted_iota(jnp.int32, (tb, tb), 1)
    cum = _mask_dot(cols <= rows, log_f) + carry_ref[...]
    carry_ref[...] = cum[tb - 1:tb, :]
    o_ref[0, 0] = cum.T


def _fox_cumsum(gates, layer, bias_rows):
    bsz, t, _ = gates.shape
    tb = min(FOX_BLOCK, t)
    return pl.pallas_call(
        _fox_cum_kernel,
        grid=(bsz, t // tb),
        in_specs=[pl.BlockSpec((1, tb, GATE_LANES), lambda i, j: (i, j, 0)), _layer_slab(bias_rows, layer)],
        out_specs=pl.BlockSpec((1, 1, GATE_LANES, tb), lambda i, j: (i, j, 0, 0)),
        out_shape=jax.ShapeDtypeStruct((bsz, t // tb, GATE_LANES, tb), F32),
        scratch_shapes=[pltpu.VMEM((1, GATE_LANES), F32)],
        compiler_params=_params(("parallel", "arbitrary")),
        name="fox_cumsum",
    )(gates, bias_rows)


def _causal_masks(blk):
    rows = lax.broadcasted_iota(jnp.int32, (blk, blk), 0)
    cols = lax.broadcasted_iota(jnp.int32, (blk, blk), 1)
    return cols <= rows, cols < rows


def _stack_heads(x2):
    lane = lax.broadcasted_iota(jnp.int32, x2.shape, 1)
    zero = jnp.zeros_like(x2)
    return jnp.concatenate([jnp.where(lane < HEAD_DIM, x2, zero), jnp.where(lane >= HEAD_DIM, x2, zero)], axis=0)


def _unstack_heads(y, rows):
    lane = lax.broadcasted_iota(jnp.int32, (rows, PAIR_WIDTH), 1)
    return jnp.where(lane < HEAD_DIM, y[:rows], y[rows:])


def _skewed(tiles, stages):
    tiles = list(tiles)
    for t in range(len(tiles) + len(stages) - 1):
        for k in reversed(range(len(stages))):
            if 0 <= t - k < len(tiles):
                stages[k](tiles[t - k])


def _fox_kernel(q_ref, k_ref, v_ref, ck_ref, o_ref, kmax_ref, *, sub):
    blk = q_ref.shape[1]
    n_pairs = q_ref.shape[2] // PAIR_WIDTH
    qi = pl.program_id(1)
    lanes = lambda p: slice(p * PAIR_WIDTH, (p + 1) * PAIR_WIDTH)

    @pl.when(qi == 0)
    def _():
        for p in range(n_pairs):
            def widest(c, best):
                kk = k_ref[0, pl.ds(pl.multiple_of(c * blk, blk), blk), lanes(p)].astype(F32)
                return jnp.maximum(best, jnp.sum(kk * kk, axis=-1, keepdims=True))
            best = lax.fori_loop(0, k_ref.shape[1] // blk, widest, jnp.zeros((blk, 1), F32))
            kmax_ref[p] = jnp.sqrt(jnp.max(best, axis=0, keepdims=True))

    qs = jnp.concatenate([_stack_heads(q_ref[0, :, lanes(p)]) for p in range(n_pairs)], axis=0)
    n_sub = 2 * n_pairs * blk // sub
    tile = lambda i: slice(i * sub, (i + 1) * sub)
    head_of = lambda i: (i * sub) // blk
    qf = qs.astype(F32)
    qnorm = jnp.sqrt(jnp.sum(qf * qf, axis=-1, keepdims=True))
    qk_bound = jnp.concatenate([qnorm[2 * p * blk:2 * (p + 1) * blk] * kmax_ref[p] for p in range(n_pairs)],
                               axis=0)
    rows = lax.broadcasted_iota(jnp.int32, (sub, blk), 0)
    cols = lax.broadcasted_iota(jnp.int32, (sub, blk), 1)
    ones = jnp.ones((blk, PAIR_WIDTH), BF16)

    def step(j, carry, masked):
        ms, accs = carry
        start = pl.multiple_of(j * blk, blk)
        kb = [k_ref[0, pl.ds(start, blk), lanes(p)] for p in range(n_pairs)]
        vb = [jnp.concatenate([v_ref[0, pl.ds(start, blk), lanes(p)], ones], axis=1) for p in range(n_pairs)]
        ms, accs = list(ms), list(accs)
        s, p, alpha = [None] * n_sub, [None] * n_sub, [None] * n_sub

        def logits(i):
            head, off = divmod(i * sub, blk)
            si = _dot_nt(qs[tile(i)], kb[head // 2]) - ck_ref[0, j, head:head + 1, :]
            s[i] = jnp.where(cols <= rows + off, si, NEG_BIG) if masked else si

        def probs(i):
            m_new = jnp.maximum(ms[i], jnp.max(s[i], axis=-1, keepdims=True))
            p[i] = jnp.exp(s[i] - m_new).astype(BF16)
            alpha[i] = jnp.exp(ms[i] - m_new)
            ms[i] = m_new

        def values(i):
            accs[i] = alpha[i] * accs[i] + _dot(p[i], vb[head_of(i) // 2])

        _skewed(range(n_sub), [logits, probs, values])
        return tuple(ms), tuple(accs)

    def headroom(j, ms):
        room = None
        for i in range(n_sub):
            last = -ck_ref[0, j, head_of(i):head_of(i) + 1, :][:, blk - 1:blk]
            r = jnp.max(qk_bound[tile(i)] - ms[i] + last)
            room = r if room is None else jnp.maximum(room, r)
        return room

    init = (tuple(jnp.full((sub, 1), NEG_BIG, F32) for _ in range(n_sub)),
            tuple(jnp.zeros((sub, 2 * PAIR_WIDTH), F32) for _ in range(n_sub)))
    carry = step(qi, init, True)

    def more(state):
        i, room, _ = state
        return (i < qi) & (room > FOX_LOG_UNDERFLOW)

    def sweep(state):
        i, _, c = state
        c = step(qi - 1 - i, c, False)
        return i + 1, headroom(jnp.maximum(qi - 2 - i, 0), c[0]), c

    _, _, (_, accs) = lax.while_loop(more, sweep, (jnp.int32(0), headroom(jnp.maximum(qi - 1, 0), carry[0]), carry))
    acc = jnp.concatenate(accs, axis=0)
    out = acc[:, :PAIR_WIDTH] / acc[:, PAIR_WIDTH:PAIR_WIDTH + 1]
    o_ref[0] = jnp.concatenate([_unstack_heads(out[2 * p * blk:2 * (p + 1) * blk], blk) for p in range(n_pairs)],
                               axis=1)


def _fox_attention(q, k, v, cum_t):
    bsz, t, width = q.shape
    blk = min(FOX_BLOCK, t)
    nk = t // blk
    whole = pl.BlockSpec((1, t, width), lambda b, i: (b, 0, 0))
    qblk = pl.BlockSpec((1, blk, width), lambda b, i: (b, i, 0))
    return pl.pallas_call(
        functools.partial(_fox_kernel, sub=min(ATTN_SUBTILE, blk)),
        grid=(bsz, nk),
        in_specs=[qblk, whole, whole,
                  pl.BlockSpec((1, nk, 8, blk), lambda b, i: (b, 0, GATE_F0 // 8, 0))],
        out_specs=qblk,
        out_shape=jax.ShapeDtypeStruct((bsz, t, width), F32),
        scratch_shapes=[pltpu.VMEM((width // PAIR_WIDTH, 1, 1), F32)],
        compiler_params=_params(("arbitrary", "arbitrary")),
        name="fox_attention",
    )(q, k, v, cum_t)


def _sb_kernel(q_ref, k_ref, v_ref, o_ref, *, tk, sub):
    tq = q_ref.shape[1]
    n_pairs = q_ref.shape[2] // PAIR_WIDTH
    per = tq // tk
    qi = pl.program_id(1)
    lanes = lambda p: slice(p * PAIR_WIDTH, (p + 1) * PAIR_WIDTH)
    qs = jnp.concatenate([_stack_heads(q_ref[0, :, lanes(p)]) for p in range(n_pairs)], axis=0)
    n_sub = 2 * n_pairs * tq // sub
    pair_of = lambda i: (i * sub) // (2 * tq)
    rows = lax.broadcasted_iota(jnp.int32, (sub, tk), 0)
    cols = lax.broadcasted_iota(jnp.int32, (sub, tk), 1)
    suffix = _causal_masks(tk)[0].astype(BF16)

    def step(j, carry, diag):
        rests, accs = carry
        start = pl.multiple_of(j * tk, tk)
        kb = [k_ref[0, pl.ds(start, tk), lanes(p)] for p in range(n_pairs)]
        vb = [v_ref[0, pl.ds(start, tk), lanes(p)] for p in range(n_pairs)]
        rests, accs = list(rests), list(accs)
        z, split, w, strict = [None] * n_sub, [None] * n_sub, [None] * n_sub, [None] * n_sub

        def logits(i):
            z[i] = _dot_nt(qs[i * sub:(i + 1) * sub], kb[pair_of(i)])
            if diag is not None:
                strict[i] = cols + diag < rows + (i * sub) % tq

        def keep(i):
            log_keep = -_softplus(z[i])
            if diag is not None:
                log_keep = jnp.where(strict[i], log_keep, 0.0)
            split[i] = jnp.concatenate(_split_bf16(log_keep), axis=0)

        def weights(i):
            tails = _dot(split[i], suffix)
            tail = tails[:sub] + tails[sub:]
            wi = jnp.exp(z[i] + tail + rests[i])
            if diag is not None:
                wi = jnp.where(strict[i], wi, 0.0)
            w[i] = wi.astype(BF16)
            rests[i] = rests[i] + tail[:, 0:1]

        def values(i):
            accs[i] = accs[i] + _dot(w[i], vb[pair_of(i)])

        live = [i for i in range(n_sub) if diag is None or diag < (i * sub) % tq + sub - 1]
        _skewed(live, [logits, keep, weights, values])
        return tuple(rests), tuple(accs)

    carry = (tuple(jnp.zeros((sub, 1), F32) for _ in range(n_sub)),
             tuple(jnp.zeros((sub, PAIR_WIDTH), F32) for _ in range(n_sub)))
    for d in reversed(range(per)):
        carry = step(qi * per + d, carry, d * tk)

    def largest(rests):
        return functools.reduce(jnp.maximum, [jnp.max(r) for r in rests])

    def more(state):
        i, top, _ = state
        return (i < qi * per) & (top > SB_LOG_UNDERFLOW)

    def sweep(state):
        i, _, c = state
        c = step(qi * per - 1 - i, c, None)
        return i + 1, largest(c[0]), c

    _, _, (_, accs) = lax.while_loop(more, sweep, (jnp.int32(0), largest(carry[0]), carry))
    acc = jnp.concatenate(accs, axis=0)
    o_ref[0] = jnp.concatenate([_unstack_heads(acc[2 * p * tq:2 * (p + 1) * tq], tq) for p in range(n_pairs)],
                               axis=1)


def _sb_attention(q, k, v):
    bsz, t, width = q.shape
    tq = min(SB_Q_BLOCK, t)
    tk = min(SB_K_BLOCK, tq)
    whole = pl.BlockSpec((1, t, width), lambda b, i: (b, 0, 0))
    qblk = pl.BlockSpec((1, tq, width), lambda b, i: (b, i, 0))
    return pl.pallas_call(
        functools.partial(_sb_kernel, tk=tk, sub=min(ATTN_SUBTILE, tq)),
        grid=(bsz, t // tq),
        in_specs=[qblk, whole, whole],
        out_specs=qblk,
        out_shape=jax.ShapeDtypeStruct((bsz, t, width), F32),
        compiler_params=_params(("parallel", "arbitrary")),
        name="sb_attention",
    )(q, k, v)


CONV_HALO = 32


def _conv_kernel(prev_ref, cur_ref, w_ref, b_ref, ng_ref, nb_ref, o_ref, xs_ref, sh_ref):
    tb = cur_ref.shape[1]
    first = pl.program_id(1) == 0
    xs_ref[0:CONV_HALO, :] = jnp.where(first, 0.0, prev_ref[0])
    xs_ref[CONV_HALO:CONV_HALO + tb, :] = cur_ref[0]
    span = tb + CONV_HALO - SUBLANES
    for r in range(1, SUBLANES):
        sh_ref[r, 0:span, :] = xs_ref[r:r + span, :]
    base = CONV_HALO - (CONF_KERNEL - 1)
    acc = jnp.zeros((tb, GROUP_WIDTH), F32) + b_ref[...]
    for tap in range(CONF_KERNEL):
        whole, r = divmod(base + tap, SUBLANES)
        src = xs_ref if r == 0 else sh_ref.at[r]
        acc = acc + w_ref[tap:tap + 1, :] * src[whole * SUBLANES:whole * SUBLANES + tb, :]
    gsz = GROUP_WIDTH // CONF_GROUPS
    parts = []
    for g in range(CONF_GROUPS):
        cg = acc[:, g * gsz:(g + 1) * gsz]
        mu = jnp.mean(cg, axis=-1, keepdims=True)
        d = cg - mu
        var = jnp.mean(d * d, axis=-1, keepdims=True)
        parts.append(d * lax.rsqrt(var + LN_EPS))
    hn = jnp.concatenate(parts, axis=-1) * ng_ref[...] + nb_ref[...]
    o_ref[0] = _silu(hn)


def _conv_module(glu, layer, w, b, ng, nb):
    bsz, t, c = glu.shape
    tb = min(CONV_BLOCK, t)
    per = tb // CONV_HALO
    return pl.pallas_call(
        _conv_kernel,
        grid=(bsz, t // tb),
        in_specs=[pl.BlockSpec((1, CONV_HALO, c), lambda i, j: (i, jnp.maximum(j * per - 1, 0), 0)),
                  pl.BlockSpec((1, tb, c), lambda i, j: (i, j, 0)),
                  ] + [_layer_slab(a, layer) for a in (w, b, ng, nb)],
        out_specs=pl.BlockSpec((1, tb, c), lambda i, j: (i, j, 0)),
        out_shape=jax.ShapeDtypeStruct(glu.shape, F32),
        scratch_shapes=[pltpu.VMEM((CONV_HALO + tb, c), F32), pltpu.VMEM((SUBLANES, CONV_HALO + tb, c), F32)],
        compiler_params=_params(("parallel", "parallel")),
        name="conv_module",
    )(glu, glu, w, b, ng, nb)


GDN_HALO = 8


def _split_bf16(x):
    hi = x.astype(BF16)
    return hi, (x - hi.astype(F32)).astype(BF16)


def _group_sum(x, ones_bd):
    rows = x.shape[0]
    hi, lo = _split_bf16(x)
    r = _dot(jnp.concatenate([hi, lo], axis=0), ones_bd)
    return r[:rows] + r[rows:]


def _odd_blocks(x, size):
    return jnp.concatenate([x[r:r + size] for r in range(size, x.shape[0], 2 * size)], axis=0)


def _spread_odd_blocks(y, size, fill):
    parts = []
    for b in range(y.shape[0] // size):
        parts += [fill[2 * b * size:(2 * b + 1) * size], y[b * size:(b + 1) * size]]
    return jnp.concatenate(parts, axis=0)


def _unit_lower_inverses(lows, rows, cols):
    n = lows[0].shape[0]
    eye = jnp.where(rows == cols, 1.0, 0.0)
    first = (rows % 2 == 1) & (cols == rows - 1)
    invs = [eye - jnp.where(first, low, 0.0) for low in lows]
    size = 2
    while size < GDN_CHUNK:
        rb = rows // size
        level = (rb % 2 == 1) & (cols // size == rb - 1)
        odd_only = size % SUBLANES == 0
        pick = (lambda x: _odd_blocks(x, size)) if odd_only else (lambda x: x)
        splits = [_split_bf16(inv) for inv in invs]
        xs = []
        for low, (d_hi, d_lo) in zip(lows, splits):
            x2 = _dot(pick(jnp.where(level, low, 0.0)).astype(BF16), jnp.concatenate([d_hi, d_lo], axis=1))
            x = x2[:, :n] + x2[:, n:]
            if odd_only:
                x = _spread_odd_blocks(x, size, jnp.zeros((n, n), F32))
            xs.append(_split_bf16(x))
        nxt = []
        for inv, (d_hi, d_lo), (x_hi, x_lo) in zip(invs, splits, xs):
            m = n // 2 if odd_only else n
            lhs = jnp.concatenate([pick(inv).astype(BF16), pick(inv - d_hi.astype(F32)).astype(BF16)], axis=0) \
                if odd_only else jnp.concatenate([d_hi, d_lo], axis=0)
            y4 = _dot(lhs, jnp.concatenate([x_hi, x_lo], axis=1))
            y = y4[:m, :n] + y4[:m, n:] + y4[m:, :n] + y4[m:, n:]
            nxt.append(_spread_odd_blocks(pick(inv) - y, size, inv) if odd_only else inv - y)
        invs = nxt
        size *= 2
    return invs


def _gdn_kernel(qp_ref, kp_ref, vp_ref, q_ref, k_ref, v_ref, z_ref, gate_ref, cw_ref, alog_ref, dtb_ref,
                ng_ref, ones_ref, o_ref, xs_ref, y_ref, state_ref):
    nb, tb = q_ref.shape[0], q_ref.shape[1]
    c = GDN_CHUNK
    n = 2 * c
    gw = GROUP_WIDTH
    n_pairs = GROUP_HEADS // 2
    first = pl.program_id(0) == 0

    @pl.when(first)
    def _():
        state_ref[...] = jnp.zeros_like(state_ref)

    ones_bd = ones_ref[...]
    base = GDN_HALO - (SHORT_CONV - 1)

    def conv_silu(idx, p_ref, c_ref, b):
        xs_ref[0:GDN_HALO, :] = jnp.where(first, 0.0, p_ref[b])
        xs_ref[GDN_HALO:GDN_HALO + tb, :] = c_ref[b]
        acc = None
        for tap in range(SHORT_CONV):
            term = cw_ref[tap:tap + 1, idx * gw:(idx + 1) * gw] * xs_ref[base + tap:base + tap + tb, :]
            acc = term if acc is None else acc + term
        return _silu(acc)

    def l2_normalize(y):
        return y * lax.rsqrt(_group_sum(y * y, ones_bd) + L2_EPS)

    rows_t = lax.broadcasted_iota(jnp.int32, (tb, tb), 0)
    cols_t = lax.broadcasted_iota(jnp.int32, (tb, tb), 1)
    chunk_tri = (cols_t <= rows_t) & (cols_t // c == rows_t // c)
    lane_bcast = lambda x, lane: jnp.broadcast_to(x[:, lane:lane + 1], (tb, PAIR_WIDTH))
    low_half = lax.broadcasted_iota(jnp.int32, (tb, PAIR_WIDTH), 1) < HEAD_DIM
    rows = lax.broadcasted_iota(jnp.int32, (n, n), 0)
    cols = lax.broadcasted_iota(jnp.int32, (n, n), 1)
    same_head = rows // c == cols // c
    lower_incl = same_head & (cols <= rows)
    strict_lower = same_head & (cols < rows)

    pairs, g_wides = {}, {}
    for b in range(nb):
        qn = l2_normalize(conv_silu(0, qp_ref, q_ref, b)) * (HEAD_DIM ** -0.5)
        kn = l2_normalize(conv_silu(1, kp_ref, k_ref, b))
        vn = conv_silu(2, vp_ref, v_ref, b)
        gates = gate_ref[b]
        log_decay = -jnp.exp(alog_ref[...]) * _softplus(gates + dtb_ref[...])
        beta_all = _sigmoid(gates)
        gcum = _mask_dot(chunk_tri, log_decay)
        g_wide = [lane_bcast(gcum, GATE_A0 + h) for h in range(GROUP_HEADS)]
        b_wide = [lane_bcast(beta_all, GATE_B0 + h) for h in range(GROUP_HEADS)]
        g_wides[b] = g_wide
        for p in range(n_pairs):
            ls = slice(p * PAIR_WIDTH, (p + 1) * PAIR_WIDTH)
            g_nat = jnp.where(low_half, g_wide[2 * p], g_wide[2 * p + 1])
            b_nat = jnp.where(low_half, b_wide[2 * p], b_wide[2 * p + 1])
            eg = jnp.exp(g_nat)
            k_beta = kn[:, ls] * b_nat
            pairs[b, p] = dict(ls=ls, g_nat=g_nat, q=qn[:, ls], k=kn[:, ls], q_dec=qn[:, ls] * eg, k_beta=k_beta,
                               v_beta=vn[:, ls] * b_nat, kb_eg=k_beta * eg)

    systems = [(ci, b, p) for ci in range(tb // c) for b in range(nb) for p in range(n_pairs)]
    chunk_rows = lambda ci: slice(ci * c, (ci + 1) * c)

    lkks, a_qks = [], []
    for ci, b, p in systems:
        d, r, g_wide = pairs[b, p], chunk_rows(ci), g_wides[b]
        g_col = jnp.concatenate([g_wide[2 * p][r], g_wide[2 * p + 1][r]], axis=0)
        decay = jnp.exp(jnp.where(lower_incl, g_col - g_col.T, -jnp.inf))
        k_st = _stack_heads(d["k"][r]).astype(BF16)
        twice = lambda x: jnp.concatenate([x, x], axis=0)
        lhs = jnp.concatenate([twice(d["k_beta"][r]), twice(d["q"][r])], axis=0).astype(BF16)
        gram = _dot_nt(lhs, k_st)
        lkks.append(jnp.where(strict_lower, gram[:n] * decay, 0.0))
        a_qks.append((gram[n:] * decay).astype(BF16))
    t_invs = _unit_lower_inverses(lkks, rows, cols)
    uws = []
    for (ci, b, p), t_inv in zip(systems, t_invs):
        d, r = pairs[b, p], chunk_rows(ci)
        rhs = jnp.concatenate([_stack_heads(d["v_beta"][r]), _stack_heads(d["kb_eg"][r])], axis=1)
        uws.append(_dot(t_inv.astype(BF16), rhs.astype(BF16)))

    states = {(b, p): state_ref[b * n_pairs + p] for b in range(nb) for p in range(n_pairs)}
    for idx, (ci, b, p) in enumerate(systems):
        d, r = pairs[b, p], chunk_rows(ci)
        uw, state = uws[idx], states[b, p]
        g_last = d["g_nat"][ci * c + c - 1:ci * c + c, :]
        k_dec = _stack_heads(d["k"][r] * jnp.exp(g_last - d["g_nat"][r])).astype(BF16)
        wq = _dot(jnp.concatenate([uw[:, n:].astype(BF16), _stack_heads(d["q_dec"][r]).astype(BF16)], axis=0),
                  state.astype(BF16))
        vnb = (uw[:, :n] - wq[:n]).astype(BF16)
        o_st = wq[n:] + _dot(a_qks[idx], vnb)
        states[b, p] = state * jnp.exp(g_last) + _dot_tn(k_dec, vnb)
        y_ref[b, r, d["ls"]] = o_st[:c] + o_st[c:]
    for (b, p), state in states.items():
        state_ref[b * n_pairs + p] = state

    for b in range(nb):
        o = y_ref[b]
        mean_sq = _group_sum(o * o, ones_bd) * (1.0 / HEAD_DIM)
        o_ref[b] = o * lax.rsqrt(mean_sq + RMS_EPS) * ng_ref[...] * _silu(z_ref[b])


def _gdn_mixer(q, k, v, z, gates, layer, conv_w, a_log_rows, dt_bias_rows, norm_g_rows, head_ones):
    bsz, t, width = q.shape
    nh = GROUP_HEADS
    tb = min(GDN_BLOCK, t)
    per = tb // GDN_HALO
    cur = pl.BlockSpec((bsz, tb, width), lambda j: (0, j, 0))
    prev = pl.BlockSpec((bsz, GDN_HALO, width), lambda j: (0, jnp.maximum(j * per - 1, 0), 0))
    return pl.pallas_call(
        _gdn_kernel,
        grid=(t // tb,),
        in_specs=[prev, prev, prev, cur, cur, cur, cur,
                  pl.BlockSpec((bsz, tb, GATE_LANES), lambda j: (0, j, 0)),
                  ] + [_layer_slab(a, layer) for a in (conv_w, a_log_rows, dt_bias_rows, norm_g_rows)]
                 + [_resident((width, width))],
        out_specs=cur,
        out_shape=jax.ShapeDtypeStruct(q.shape, F32),
        scratch_shapes=[pltpu.VMEM((GDN_HALO + tb, width), F32), pltpu.VMEM((bsz, tb, width), F32),
                        pltpu.VMEM((bsz * nh // 2, PAIR_WIDTH, PAIR_WIDTH), F32)],
        compiler_params=_params(("arbitrary",)),
        name="gdn_mixer",
    )(q, k, v, q, k, v, z, gates, conv_w, a_log_rows, dt_bias_rows, norm_g_rows, head_ones)


def _memkv_kernel(m_ref, w_ref, o_ref):
    o_ref[0] = _dot(m_ref[0].astype(BF16), w_ref[...].astype(BF16)).astype(o_ref.dtype)


def _memory_kv(mem, layer, w_kv):
    bsz, m, d = mem.shape
    return pl.pallas_call(
        _memkv_kernel,
        grid=(bsz,),
        in_specs=[pl.BlockSpec((1, m, d), lambda i: (i, 0, 0)), _layer_slab(w_kv, layer)],
        out_specs=pl.BlockSpec((1, m, 2 * d), lambda i: (i, 0, 0)),
        out_shape=jax.ShapeDtypeStruct((bsz, m, 2 * d), BF16),
        compiler_params=_params(("parallel",)),
        name="memory_kv",
    )(mem, w_kv)


def _mix_mem_kernel(x_ref, ya_ref, yb_ref, yc_ref, yd_ref, kv_ref, wout_ref, g1_ref, b1_ref, wq_ref, wo_ref,
                    g2_ref, b2_ref, o_ref, *, sub):
    gw = GROUP_WIDTH
    hd = MEM_HEAD_DIM
    n_sub = x_ref.shape[1] // sub
    tile = lambda i: slice(i * sub, (i + 1) * sub)
    x1, att = [None] * n_sub, [None] * n_sub

    def mix(i):
        tot = None
        for group, y_ref in enumerate((ya_ref, yb_ref, yc_ref, yd_ref)):
            part = _dot(y_ref[0, tile(i), :].astype(BF16), wout_ref[group * gw:(group + 1) * gw, :].astype(BF16))
            tot = part if tot is None else tot + part
        x1[i] = _layer_norm(DN_ALPHA * x_ref[0, tile(i), :] + tot, g1_ref[...], b1_ref[...])

    def attend(i):
        q = _dot(x1[i].astype(BF16), wq_ref[...].astype(BF16))
        outs = []
        for h in range(MEM_HEADS):
            qh = (q[:, h * hd:(h + 1) * hd] * (hd ** -0.5)).astype(BF16)
            kh = kv_ref[0, :, h * hd:(h + 1) * hd]
            vh = kv_ref[0, :, D_MODEL + h * hd:D_MODEL + (h + 1) * hd]
            s = _dot_nt(qh, kh)
            p = jnp.exp(s - jnp.max(s, axis=-1, keepdims=True))
            p = p / jnp.sum(p, axis=-1, keepdims=True)
            outs.append(_dot(p.astype(BF16), vh).astype(BF16))
        att[i] = jnp.concatenate(outs, axis=-1)

    def project(i):
        y = _dot(att[i], wo_ref[...].astype(BF16))
        o_ref[0, tile(i), :] = _layer_norm(DN_ALPHA * x1[i] + y, g2_ref[...], b2_ref[...])

    _skewed(range(n_sub), [mix, attend, project])


def _mix_and_memory_ln(x, ya, yb, yc, yd, kv, layer, w_out, g_mix, b_mix, wq, wo, g_mem, b_mem):
    bsz, t, d = x.shape
    tm = min(MIX_TILE, t)
    row = lambda c: pl.BlockSpec((1, tm, c), lambda i, j: (i, j, 0))
    return pl.pallas_call(
        functools.partial(_mix_mem_kernel, sub=min(ROW_SUBTILE, tm)),
        grid=(bsz, t // tm),
        in_specs=[row(d)] + [row(GROUP_WIDTH)] * 4 + [pl.BlockSpec((1, kv.shape[1], 2 * d), lambda i, j: (i, 0, 0))]
                 + [_layer_slab(a, layer) for a in (w_out, g_mix, b_mix, wq, wo, g_mem, b_mem)],
        out_specs=row(d),
        out_shape=jax.ShapeDtypeStruct(x.shape, F32),
        compiler_params=_params(("parallel", "parallel")),
        name="mix_and_memory_ln",
    )(x, ya, yb, yc, yd, kv, w_out, g_mix, b_mix, wq, wo, g_mem, b_mem)


def _regrouped_in_weights(w_in):
    gw, nh = GROUP_WIDTH, GROUP_HEADS
    w_in = w_in[..., N_LEADING_GROUPS * gw:].astype(BF16)
    o = 0
    gdn_a = w_in[..., o:o + nh]; o += nh
    gdn_b = w_in[..., o:o + nh]; o += nh
    fox_qkv = w_in[..., o:o + 3 * gw]; o += 3 * gw
    fox_f = w_in[..., o:o + nh]; o += nh
    conf = w_in[..., o:o + 2 * gw]; o += 2 * gw
    sb_qkv = w_in[..., o:o + 3 * gw]
    pad = jnp.zeros(w_in.shape[:-1] + (GATE_LANES - 3 * nh,), w_in.dtype)
    return jnp.concatenate([fox_qkv, conf, sb_qkv, gdn_a, gdn_b, fox_f, pad], axis=-1)


def _gate_rows(vals, first_lane):
    depth, nh = vals.shape
    return jnp.pad(vals, ((0, 0), (first_lane, GATE_LANES - first_lane - nh))).reshape(depth, 1, GATE_LANES)


def kernel(x, mem, ffn1_w_gate, ffn1_w_up, ffn1_w_down, ln_ffn1_g, ln_ffn1_b, w_in, gdn_conv_w, gdn_a_log, gdn_dt_bias, gdn_norm_g, fox_b_f, conf_dw_w, conf_dw_b, conf_norm_g, conf_norm_b, w_out, ln_mix_g, ln_mix_b, mem_w_q, mem_w_kv, mem_w_o, ln_mem_g, ln_mem_b, ffn2_w_gate, ffn2_w_up, ffn2_w_down, ln_ffn2_g, ln_ffn2_b):
    rows = lambda a: a.reshape(a.shape[0], 1, a.shape[1])
    ffn1 = (ffn1_w_gate, ffn1_w_up, ffn1_w_down, rows(ln_ffn1_g), rows(ln_ffn1_b))
    ffn2 = (ffn2_w_gate, ffn2_w_up, ffn2_w_down, rows(ln_ffn2_g), rows(ln_ffn2_b))
    w_rest = _regrouped_in_weights(w_in)
    gdn = (gdn_conv_w, _gate_rows(gdn_a_log, GATE_A0), _gate_rows(gdn_dt_bias, GATE_A0),
           rows(jnp.tile(gdn_norm_g, (1, GROUP_HEADS))),
           jnp.kron(jnp.eye(GROUP_HEADS, dtype=BF16), jnp.ones((HEAD_DIM, HEAD_DIM), BF16)))
    fox_bias = _gate_rows(fox_b_f, GATE_F0)
    conf = (conf_dw_w, rows(conf_dw_b), rows(conf_norm_g), rows(conf_norm_b))
    mix_mem = (w_out, rows(ln_mix_g), rows(ln_mix_b), mem_w_q, mem_w_o, rows(ln_mem_g), rows(ln_mem_b))
    w_kv = mem_w_kv

    for i in range(DEPTH):
        x = _ffn_ln(x, i, *ffn1)
        gq, gk, gv, gz, fq, fk, fv, glu, sq, sk, sv, gates = _in_projection(x, i, w_in, w_rest)
        y_a = _gdn_mixer(gq, gk, gv, gz, gates, i, *gdn)
        y_b = _fox_attention(fq, fk, fv, _fox_cumsum(gates, i, fox_bias))
        y_c = _conv_module(glu, i, *conf)
        y_d = _sb_attention(sq, sk, sv)
        x = _mix_and_memory_ln(x, y_a, y_b, y_c, y_d, _memory_kv(mem, i, w_kv), i, *mix_mem)
        x = _ffn_ln(x, i, *ffn2)
    return x
```

```python
import functools

import jax
import jax.numpy as jnp
from jax import lax
from jax.experimental import pallas as pl
from jax.experimental.pallas import tpu as pltpu

F32 = jnp.float32
BF16 = jnp.bfloat16

D_MODEL = 1024
DEPTH = 2
GROUP_WIDTH = 256
HEAD_DIM = 64
GROUP_HEADS = 4
D_FF = 2816
SHORT_CONV = 4
CONF_KERNEL = 31
CONF_GROUPS = 4
GDN_CHUNK = 64
MEM_HEADS = 4
MEM_HEAD_DIM = D_MODEL // MEM_HEADS
DN_ALPHA = float((2 * DEPTH) ** 0.25)
LN_EPS = 1e-5
RMS_EPS = 1e-6
L2_EPS = 1e-6
NEG_BIG = -1e30

GATE_LANES = 128
GATE_A0, GATE_B0, GATE_F0 = 0, 4, 8

TOKEN_TILE = 512
MXU_WIDTH = 256
SUBLANES = 8
FFN_SPLITS = (0, 6 * MXU_WIDTH, D_FF)
PAIR_WIDTH = 2 * HEAD_DIM
FOX_BLOCK = 512
SB_Q_BLOCK = 512
SB_K_BLOCK = 256
ATTN_SUBTILE = 256
MIX_TILE = 1024
ROW_SUBTILE = 512
SB_LOG_UNDERFLOW = -105.0
FOX_LOG_UNDERFLOW = -106.0
GDN_BLOCK = 256
CONV_BLOCK = 512
VMEM_LIMIT = 56 * 1024 * 1024


def _params(sem, vmem=VMEM_LIMIT):
    return pltpu.CompilerParams(dimension_semantics=sem, vmem_limit_bytes=vmem)


def _resident(shape):
    nd = len(shape)
    return pl.BlockSpec(shape, lambda *_: (0,) * nd, pipeline_mode=pl.Buffered(1))


def _layer_slab(stacked, layer):
    tail = stacked.shape[1:]
    return pl.BlockSpec((None,) + tail, lambda *_: (layer,) + (0,) * len(tail), pipeline_mode=pl.Buffered(1))


def _layer_norm(y, g, b):
    mu = jnp.mean(y, axis=-1, keepdims=True)
    d = y - mu
    var = jnp.mean(d * d, axis=-1, keepdims=True)
    return d * lax.rsqrt(var + LN_EPS) * g + b


def _sigmoid(x):
    return 1.0 / (1.0 + jnp.exp(-x))


def _silu(x):
    return x * _sigmoid(x)


def _softplus(x):
    return jnp.maximum(x, 0.0) + jnp.log(1.0 + jnp.exp(-jnp.abs(x)))


def _dot(a, b):
    return jnp.dot(a, b, preferred_element_type=F32)


def _dot_nt(a, b):
    return lax.dot_general(a, b, (((1,), (1,)), ((), ())), preferred_element_type=F32)


def _dot_tn(a, b):
    return lax.dot_general(a, b, (((0,), (0,)), ((), ())), preferred_element_type=F32)


def _mask_dot(mask, x):
    n = x.shape[1]
    hi = x.astype(BF16)
    r1 = x - hi.astype(F32)
    mid = r1.astype(BF16)
    lo = (r1 - mid.astype(F32)).astype(BF16)
    r = _dot(mask.astype(BF16), jnp.concatenate([hi, mid, lo], axis=1))
    return r[:, :n] + r[:, n:2 * n] + r[:, 2 * n:]


def _ffn_kernel(x_ref, wg_ref, wu_ref, wd_ref, g_ref, b_ref, o_ref):
    x = x_ref[0]
    xb = x.astype(BF16)
    acc = None
    for lo, hi in zip(FFN_SPLITS[:-1], FFN_SPLITS[1:]):
        h = _dot(xb, wg_ref[:, lo:hi].astype(BF16))
        u = _dot(xb, wu_ref[:, lo:hi].astype(BF16))
        a = (_silu(h) * u).astype(BF16)
        part = _dot(a, wd_ref[lo:hi, :].astype(BF16))
        acc = part if acc is None else acc + part
    o_ref[0] = _layer_norm(DN_ALPHA * x + 0.5 * acc, g_ref[...], b_ref[...])


def _ffn_ln(x, layer, wg, wu, wd, g, b):
    bsz, t, d = x.shape
    tm = min(TOKEN_TILE, t)
    row = pl.BlockSpec((1, tm, d), lambda i, j: (i, j, 0))
    return pl.pallas_call(
        _ffn_kernel,
        grid=(bsz, t // tm),
        in_specs=[row] + [_layer_slab(a, layer) for a in (wg, wu, wd, g, b)],
        out_specs=row,
        out_shape=jax.ShapeDtypeStruct(x.shape, F32),
        compiler_params=_params(("parallel", "parallel")),
        name="ffn_ln",
    )(x, wg, wu, wd, g, b)


N_LEADING_GROUPS = 4


def _inproj_kernel(x_ref, wa_ref, wb_ref, gq_ref, gk_ref, gv_ref, gz_ref, fq_ref, fk_ref, fv_ref,
                   glu_ref, sq_ref, sk_ref, sv_ref, gate_ref):
    xb = x_ref[0].astype(BF16)
    gw = GROUP_WIDTH

    def group(i, width=gw):
        if i < N_LEADING_GROUPS:
            return _dot(xb, wa_ref[:, i * gw:i * gw + width].astype(BF16))
        i -= N_LEADING_GROUPS
        return _dot(xb, wb_ref[:, i * gw:i * gw + width])

    qk_scale = HEAD_DIM ** -0.5
    gq_ref[0] = group(0)
    gk_ref[0] = group(1)
    gv_ref[0] = group(2)
    gz_ref[0] = group(3)
    fq_ref[0] = (group(4) * qk_scale).astype(BF16)
    fk_ref[0] = group(5).astype(BF16)
    fv_ref[0] = group(6).astype(BF16)
    glu_ref[0] = group(7) * _sigmoid(group(8))
    sq_ref[0] = (group(9) * qk_scale).astype(BF16)
    sk_ref[0] = group(10).astype(BF16)
    sv_ref[0] = group(11).astype(BF16)
    gate_ref[0] = group(12, GATE_LANES)


def _in_projection(x, layer, w_lead, w_rest):
    bsz, t, d = x.shape
    tm = min(TOKEN_TILE, t)
    row = lambda c: pl.BlockSpec((1, tm, c), lambda i, j: (i, j, 0))
    wide = lambda dt: jax.ShapeDtypeStruct((bsz, t, GROUP_WIDTH), dt)
    out_shape = ([wide(F32)] * 4 + [wide(BF16)] * 3 + [wide(F32)] + [wide(BF16)] * 3
                 + [jax.ShapeDtypeStruct((bsz, t, GATE_LANES), F32)])
    out_specs = [row(GROUP_WIDTH)] * 11 + [row(GATE_LANES)]
    return pl.pallas_call(
        _inproj_kernel,
        grid=(bsz, t // tm),
        in_specs=[row(d), _layer_slab(w_lead, layer), _layer_slab(w_rest, layer)],
        out_specs=out_specs,
        out_shape=out_shape,
        compiler_params=_params(("parallel", "parallel")),
        name="in_projection",
    )(x, w_lead, w_rest)


def _fox_cum_kernel(gate_ref, bias_ref, o_ref, carry_ref):
    @pl.when(pl.program_id(1) == 0)
    def _():
        carry_ref[...] = jnp.zeros_like(carry_ref)

    tb = gate_ref.shape[1]
    logit = gate_ref[0] + bias_ref[...]
    log_f = -_softplus(-logit)
    rows = lax.broadcasted_iota(jnp.int32, (tb, tb), 0)
    cols = lax.broadcasted_iota(jnp.int32, (tb, tb), 1)
    cum = _mask_dot(cols <= rows, log_f) + carry_ref[...]
    carry_ref[...] = cum[tb - 1:tb, :]
    o_ref[0, 0] = cum.T


def _fox_cumsum(gates, layer, bias_rows):
    bsz, t, _ = gates.shape
    tb = min(FOX_BLOCK, t)
    return pl.pallas_call(
        _fox_cum_kernel,
        grid=(bsz, t // tb),
        in_specs=[pl.BlockSpec((1, tb, GATE_LANES), lambda i, j: (i, j, 0)), _layer_slab(bias_rows, layer)],
        out_specs=pl.BlockSpec((1, 1, GATE_LANES, tb), lambda i, j: (i, j, 0, 0)),
        out_shape=jax.ShapeDtypeStruct((bsz, t // tb, GATE_LANES, tb), F32),
        scratch_shapes=[pltpu.VMEM((1, GATE_LANES), F32)],
        compiler_params=_params(("parallel", "arbitrary")),
        name="fox_cumsum",
    )(gates, bias_rows)


def _causal_masks(blk):
    rows = lax.broadcasted_iota(jnp.int32, (blk, blk), 0)
    cols = lax.broadcasted_iota(jnp.int32, (blk, blk), 1)
    return cols <= rows, cols < rows


def _stack_heads(x2):
    lane = lax.broadcasted_iota(jnp.int32, x2.shape, 1)
    zero = jnp.zeros_like(x2)
    return jnp.concatenate([jnp.where(lane < HEAD_DIM, x2, zero), jnp.where(lane >= HEAD_DIM, x2, zero)], axis=0)


def _unstack_heads(y, rows):
    lane = lax.broadcasted_iota(jnp.int32, (rows, PAIR_WIDTH), 1)
    return jnp.where(lane < HEAD_DIM, y[:rows], y[rows:])


def _skewed(tiles, stages):
    tiles = list(tiles)
    for t in range(len(tiles) + len(stages) - 1):
        for k in reversed(range(len(stages))):
            if 0 <= t - k < len(tiles):
                stages[k](tiles[t - k])


def _fox_kernel(q_ref, k_ref, v_ref, ck_ref, o_ref, kmax_ref, *, sub):
    blk = q_ref.shape[1]
    n_pairs = q_ref.shape[2] // PAIR_WIDTH
    qi = pl.program_id(1)
    lanes = lambda p: slice(p * PAIR_WIDTH, (p + 1) * PAIR_WIDTH)

    @pl.when(qi == 0)
    def _():
        for p in range(n_pairs):
            def widest(c, best):
                kk = k_ref[0, pl.ds(pl.multiple_of(c * blk, blk), blk), lanes(p)].astype(F32)
                return jnp.maximum(best, jnp.sum(kk * kk, axis=-1, keepdims=True))
            best = lax.fori_loop(0, k_ref.shape[1] // blk, widest, jnp.zeros((blk, 1), F32))
            kmax_ref[p] = jnp.sqrt(jnp.max(best, axis=0, keepdims=True))

    qs = jnp.concatenate([_stack_heads(q_ref[0, :, lanes(p)]) for p in range(n_pairs)], axis=0)
    n_sub = 2 * n_pairs * blk // sub
    tile = lambda i: slice(i * sub, (i + 1) * sub)
    head_of = lambda i: (i * sub) // blk
    qf = qs.astype(F32)
    qnorm = jnp.sqrt(jnp.sum(qf * qf, axis=-1, keepdims=True))
    qk_bound = jnp.concatenate([qnorm[2 * p * blk:2 * (p + 1) * blk] * kmax_ref[p] for p in range(n_pairs)],
                               axis=0)
    rows = lax.broadcasted_iota(jnp.int32, (sub, blk), 0)
    cols = lax.broadcasted_iota(jnp.int32, (sub, blk), 1)
    ones = jnp.ones((blk, PAIR_WIDTH), BF16)

    def step(j, carry, masked):
        ms, accs = carry
        start = pl.multiple_of(j * blk, blk)
        kb = [k_ref[0, pl.ds(start, blk), lanes(p)] for p in range(n_pairs)]
        vb = [jnp.concatenate([v_ref[0, pl.ds(start, blk), lanes(p)], ones], axis=1) for p in range(n_pairs)]
        ms, accs = list(ms), list(accs)
        s, p, alpha = [None] * n_sub, [None] * n_sub, [None] * n_sub

        def logits(i):
            head, off = divmod(i * sub, blk)
            si = _dot_nt(qs[tile(i)], kb[head // 2]) - ck_ref[0, j, head:head + 1, :]
            s[i] = jnp.where(cols <= rows + off, si, NEG_BIG) if masked else si

        def probs(i):
            m_new = jnp.maximum(ms[i], jnp.max(s[i], axis=-1, keepdims=True))
            p[i] = jnp.exp(s[i] - m_new).astype(BF16)
            alpha[i] = jnp.exp(ms[i] - m_new)
            ms[i] = m_new

        def values(i):
            accs[i] = alpha[i] * accs[i] + _dot(p[i], vb[head_of(i) // 2])

        _skewed(range(n_sub), [logits, probs, values])
        return tuple(ms), tuple(accs)

    def headroom(j, ms):
        room = None
        for i in range(n_sub):
            last = -ck_ref[0, j, head_of(i):head_of(i) + 1, :][:, blk - 1:blk]
            r = jnp.max(qk_bound[tile(i)] - ms[i] + last)
            room = r if room is None else jnp.maximum(room, r)
        return room

    init = (tuple(jnp.full((sub, 1), NEG_BIG, F32) for _ in range(n_sub)),
            tuple(jnp.zeros((sub, 2 * PAIR_WIDTH), F32) for _ in range(n_sub)))
    carry = step(qi, init, True)

    def more(state):
        i, room, _ = state
        return (i < qi) & (room > FOX_LOG_UNDERFLOW)

    def sweep(state):
        i, _, c = state
        c = step(qi - 1 - i, c, False)
        return i + 1, headroom(jnp.maximum(qi - 2 - i, 0), c[0]), c

    _, _, (_, accs) = lax.while_loop(more, sweep, (jnp.int32(0), headroom(jnp.maximum(qi - 1, 0), carry[0]), carry))
    acc = jnp.concatenate(accs, axis=0)
    out = acc[:, :PAIR_WIDTH] / acc[:, PAIR_WIDTH:PAIR_WIDTH + 1]
    o_ref[0] = jnp.concatenate([_unstack_heads(out[2 * p * blk:2 * (p + 1) * blk], blk) for p in range(n_pairs)],
                               axis=1)


def _fox_attention(q, k, v, cum_t):
    bsz, t, width = q.shape
    blk = min(FOX_BLOCK, t)
    nk = t // blk
    whole = pl.BlockSpec((1, t, width), lambda b, i: (b, 0, 0))
    qblk = pl.BlockSpec((1, blk, width), lambda b, i: (b, i, 0))
    return pl.pallas_call(
        functools.partial(_fox_kernel, sub=min(ATTN_SUBTILE, blk)),
        grid=(bsz, nk),
        in_specs=[qblk, whole, whole,
                  pl.BlockSpec((1, nk, 8, blk), lambda b, i: (b, 0, GATE_F0 // 8, 0))],
        out_specs=qblk,
        out_shape=jax.ShapeDtypeStruct((bsz, t, width), F32),
        scratch_shapes=[pltpu.VMEM((width // PAIR_WIDTH, 1, 1), F32)],
        compiler_params=_params(("arbitrary", "arbitrary")),
        name="fox_attention",
    )(q, k, v, cum_t)


def _sb_kernel(q_ref, k_ref, v_ref, o_ref, *, tk, sub):
    tq = q_ref.shape[1]
    n_pairs = q_ref.shape[2] // PAIR_WIDTH
    per = tq // tk
    qi = pl.program_id(1)
    lanes = lambda p: slice(p * PAIR_WIDTH, (p + 1) * PAIR_WIDTH)
    qs = jnp.concatenate([_stack_heads(q_ref[0, :, lanes(p)]) for p in range(n_pairs)], axis=0)
    n_sub = 2 * n_pairs * tq // sub
    pair_of = lambda i: (i * sub) // (2 * tq)
    rows = lax.broadcasted_iota(jnp.int32, (sub, tk), 0)
    cols = lax.broadcasted_iota(jnp.int32, (sub, tk), 1)
    suffix = _causal_masks(tk)[0].astype(BF16)

    def step(j, carry, diag):
        rests, accs = carry
        start = pl.multiple_of(j * tk, tk)
        kb = [k_ref[0, pl.ds(start, tk), lanes(p)] for p in range(n_pairs)]
        vb = [v_ref[0, pl.ds(start, tk), lanes(p)] for p in range(n_pairs)]
        rests, accs = list(rests), list(accs)
        z, split, w, strict = [None] * n_sub, [None] * n_sub, [None] * n_sub, [None] * n_sub

        def logits(i):
            z[i] = _dot_nt(qs[i * sub:(i + 1) * sub], kb[pair_of(i)])
            if diag is not None:
                strict[i] = cols + diag < rows + (i * sub) % tq

        def keep(i):
            log_keep = -_softplus(z[i])
            if diag is not None:
                log_keep = jnp.where(strict[i], log_keep, 0.0)
            split[i] = jnp.concatenate(_split_bf16(log_keep), axis=0)

        def weights(i):
            tails = _dot(split[i], suffix)
            tail = tails[:sub] + tails[sub:]
            wi = jnp.exp(z[i] + tail + rests[i])
            if diag is not None:
                wi = jnp.where(strict[i], wi, 0.0)
            w[i] = wi.astype(BF16)
            rests[i] = rests[i] + tail[:, 0:1]

        def values(i):
            accs[i] = accs[i] + _dot(w[i], vb[pair_of(i)])

        live = [i for i in range(n_sub) if diag is None or diag < (i * sub) % tq + sub - 1]
        _skewed(live, [logits, keep, weights, values])
        return tuple(rests), tuple(accs)

    carry = (tuple(jnp.zeros((sub, 1), F32) for _ in range(n_sub)),
             tuple(jnp.zeros((sub, PAIR_WIDTH), F32) for _ in range(n_sub)))
    for d in reversed(range(per)):
        carry = step(qi * per + d, carry, d * tk)

    def largest(rests):
        return functools.reduce(jnp.maximum, [jnp.max(r) for r in rests])

    def more(state):
        i, top, _ = state
        return (i < qi * per) & (top > SB_LOG_UNDERFLOW)

    def sweep(state):
        i, _, c = state
        c = step(qi * per - 1 - i, c, None)
        return i + 1, largest(c[0]), c

    _, _, (_, accs) = lax.while_loop(more, sweep, (jnp.int32(0), largest(carry[0]), carry))
    acc = jnp.concatenate(accs, axis=0)
    o_ref[0] = jnp.concatenate([_unstack_heads(acc[2 * p * tq:2 * (p + 1) * tq], tq) for p in range(n_pairs)],
                               axis=1)


def _sb_attention(q, k, v):
    bsz, t, width = q.shape
    tq = min(SB_Q_BLOCK, t)
    tk = min(SB_K_BLOCK, tq)
    whole = pl.BlockSpec((1, t, width), lambda b, i: (b, 0, 0))
    qblk = pl.BlockSpec((1, tq, width), lambda b, i: (b, i, 0))
    return pl.pallas_call(
        functools.partial(_sb_kernel, tk=tk, sub=min(ATTN_SUBTILE, tq)),
        grid=(bsz, t // tq),
        in_specs=[qblk, whole, whole],
        out_specs=qblk,
        out_shape=jax.ShapeDtypeStruct((bsz, t, width), F32),
        compiler_params=_params(("parallel", "arbitrary")),
        name="sb_attention",
    )(q, k, v)


CONV_HALO = 32


def _conv_kernel(prev_ref, cur_ref, w_ref, b_ref, ng_ref, nb_ref, o_ref, xs_ref, sh_ref):
    tb = cur_ref.shape[1]
    first = pl.program_id(1) == 0
    xs_ref[0:CONV_HALO, :] = jnp.where(first, 0.0, prev_ref[0])
    xs_ref[CONV_HALO:CONV_HALO + tb, :] = cur_ref[0]
    span = tb + CONV_HALO - SUBLANES
    for r in range(1, SUBLANES):
        sh_ref[r, 0:span, :] = xs_ref[r:r + span, :]
    base = CONV_HALO - (CONF_KERNEL - 1)
    acc = jnp.zeros((tb, GROUP_WIDTH), F32) + b_ref[...]
    for tap in range(CONF_KERNEL):
        whole, r = divmod(base + tap, SUBLANES)
        src = xs_ref if r == 0 else sh_ref.at[r]
        acc = acc + w_ref[tap:tap + 1, :] * src[whole * SUBLANES:whole * SUBLANES + tb, :]
    gsz = GROUP_WIDTH // CONF_GROUPS
    parts = []
    for g in range(CONF_GROUPS):
        cg = acc[:, g * gsz:(g + 1) * gsz]
        mu = jnp.mean(cg, axis=-1, keepdims=True)
        d = cg - mu
        var = jnp.mean(d * d, axis=-1, keepdims=True)
        parts.append(d * lax.rsqrt(var + LN_EPS))
    hn = jnp.concatenate(parts, axis=-1) * ng_ref[...] + nb_ref[...]
    o_ref[0] = _silu(hn)


def _conv_module(glu, layer, w, b, ng, nb):
    bsz, t, c = glu.shape
    tb = min(CONV_BLOCK, t)
    per = tb // CONV_HALO
    return pl.pallas_call(
        _conv_kernel,
        grid=(bsz, t // tb),
        in_specs=[pl.BlockSpec((1, CONV_HALO, c), lambda i, j: (i, jnp.maximum(j * per - 1, 0), 0)),
                  pl.BlockSpec((1, tb, c), lambda i, j: (i, j, 0)),
                  ] + [_layer_slab(a, layer) for a in (w, b, ng, nb)],
        out_specs=pl.BlockSpec((1, tb, c), lambda i, j: (i, j, 0)),
        out_shape=jax.ShapeDtypeStruct(glu.shape, F32),
        scratch_shapes=[pltpu.VMEM((CONV_HALO + tb, c), F32), pltpu.VMEM((SUBLANES, CONV_HALO + tb, c), F32)],
        compiler_params=_params(("parallel", "parallel")),
        name="conv_module",
    )(glu, glu, w, b, ng, nb)


GDN_HALO = 8


def _split_bf16(x):
    hi = x.astype(BF16)
    return hi, (x - hi.astype(F32)).astype(BF16)


def _group_sum(x, ones_bd):
    rows = x.shape[0]
    hi, lo = _split_bf16(x)
    r = _dot(jnp.concatenate([hi, lo], axis=0), ones_bd)
    return r[:rows] + r[rows:]


def _odd_blocks(x, size):
    return jnp.concatenate([x[r:r + size] for r in range(size, x.shape[0], 2 * size)], axis=0)


def _spread_odd_blocks(y, size, fill):
    parts = []
    for b in range(y.shape[0] // size):
        parts += [fill[2 * b * size:(2 * b + 1) * size], y[b * size:(b + 1) * size]]
    return jnp.concatenate(parts, axis=0)


def _unit_lower_inverses(lows, rows, cols):
    n = lows[0].shape[0]
    eye = jnp.where(rows == cols, 1.0, 0.0)
    first = (rows % 2 == 1) & (cols == rows - 1)
    invs = [eye - jnp.where(first, low, 0.0) for low in lows]
    size = 2
    while size < GDN_CHUNK:
        rb = rows // size
        level = (rb % 2 == 1) & (cols // size == rb - 1)
        odd_only = size % SUBLANES == 0
        pick = (lambda x: _odd_blocks(x, size)) if odd_only else (lambda x: x)
        splits = [_split_bf16(inv) for inv in invs]
        xs = []
        for low, (d_hi, d_lo) in zip(lows, splits):
            x2 = _dot(pick(jnp.where(level, low, 0.0)).astype(BF16), jnp.concatenate([d_hi, d_lo], axis=1))
            x = x2[:, :n] + x2[:, n:]
            if odd_only:
                x = _spread_odd_blocks(x, size, jnp.zeros((n, n), F32))
            xs.append(_split_bf16(x))
        nxt = []
        for inv, (d_hi, d_lo), (x_hi, x_lo) in zip(invs, splits, xs):
            m = n // 2 if odd_only else n
            lhs = jnp.concatenate([pick(inv).astype(BF16), pick(inv - d_hi.astype(F32)).astype(BF16)], axis=0) \
                if odd_only else jnp.concatenate([d_hi, d_lo], axis=0)
            y4 = _dot(lhs, jnp.concatenate([x_hi, x_lo], axis=1))
            y = y4[:m, :n] + y4[:m, n:] + y4[m:, :n] + y4[m:, n:]
            nxt.append(_spread_odd_blocks(pick(inv) - y, size, inv) if odd_only else inv - y)
        invs = nxt
        size *= 2
    return invs


def _gdn_kernel(qp_ref, kp_ref, vp_ref, q_ref, k_ref, v_ref, z_ref, gate_ref, cw_ref, alog_ref, dtb_ref,
                ng_ref, ones_ref, o_ref, xs_ref, y_ref, state_ref):
    nb, tb = q_ref.shape[0], q_ref.shape[1]
    c = GDN_CHUNK
    n = 2 * c
    gw = GROUP_WIDTH
    n_pairs = GROUP_HEADS // 2
    first = pl.program_id(0) == 0

    @pl.when(first)
    def _():
        state_ref[...] = jnp.zeros_like(state_ref)

    ones_bd = ones_ref[...]
    base = GDN_HALO - (SHORT_CONV - 1)

    def conv_silu(idx, p_ref, c_ref, b):
        xs_ref[0:GDN_HALO, :] = jnp.where(first, 0.0, p_ref[b])
        xs_ref[GDN_HALO:GDN_HALO + tb, :] = c_ref[b]
        acc = None
        for tap in range(SHORT_CONV):
            term = cw_ref[tap:tap + 1, idx * gw:(idx + 1) * gw] * xs_ref[base + tap:base + tap + tb, :]
            acc = term if acc is None else acc + term
        return _silu(acc)

    def l2_normalize(y):
        return y * lax.rsqrt(_group_sum(y * y, ones_bd) + L2_EPS)

    rows_t = lax.broadcasted_iota(jnp.int32, (tb, tb), 0)
    cols_t = lax.broadcasted_iota(jnp.int32, (tb, tb), 1)
    chunk_tri = (cols_t <= rows_t) & (cols_t // c == rows_t // c)
    lane_bcast = lambda x, lane: jnp.broadcast_to(x[:, lane:lane + 1], (tb, PAIR_WIDTH))
    low_half = lax.broadcasted_iota(jnp.int32, (tb, PAIR_WIDTH), 1) < HEAD_DIM
    rows = lax.broadcasted_iota(jnp.int32, (n, n), 0)
    cols = lax.broadcasted_iota(jnp.int32, (n, n), 1)
    same_head = rows // c == cols // c
    lower_incl = same_head & (cols <= rows)
    strict_lower = same_head & (cols < rows)

    pairs, g_wides = {}, {}
    for b in range(nb):
        qn = l2_normalize(conv_silu(0, qp_ref, q_ref, b)) * (HEAD_DIM ** -0.5)
        kn = l2_normalize(conv_silu(1, kp_ref, k_ref, b))
        vn = conv_silu(2, vp_ref, v_ref, b)
        gates = gate_ref[b]
        log_decay = -jnp.exp(alog_ref[...]) * _softplus(gates + dtb_ref[...])
        beta_all = _sigmoid(gates)
        gcum = _mask_dot(chunk_tri, log_decay)
        g_wide = [lane_bcast(gcum, GATE_A0 + h) for h in range(GROUP_HEADS)]
        b_wide = [lane_bcast(beta_all, GATE_B0 + h) for h in range(GROUP_HEADS)]
        g_wides[b] = g_wide
        for p in range(n_pairs):
            ls = slice(p * PAIR_WIDTH, (p + 1) * PAIR_WIDTH)
            g_nat = jnp.where(low_half, g_wide[2 * p], g_wide[2 * p + 1])
            b_nat = jnp.where(low_half, b_wide[2 * p], b_wide[2 * p + 1])
            eg = jnp.exp(g_nat)
            k_beta = kn[:, ls] * b_nat
            pairs[b, p] = dict(ls=ls, g_nat=g_nat, q=qn[:, ls], k=kn[:, ls], q_dec=qn[:, ls] * eg, k_beta=k_beta,
                               v_beta=vn[:, ls] * b_nat, kb_eg=k_beta * eg)

    systems = [(ci, b, p) for ci in range(tb // c) for b in range(nb) for p in range(n_pairs)]
    chunk_rows = lambda ci: slice(ci * c, (ci + 1) * c)

    lkks, a_qks = [], []
    for ci, b, p in systems:
        d, r, g_wide = pairs[b, p], chunk_rows(ci), g_wides[b]
        g_col = jnp.concatenate([g_wide[2 * p][r], g_wide[2 * p + 1][r]], axis=0)
        decay = jnp.exp(jnp.where(lower_incl, g_col - g_col.T, -jnp.inf))
        k_st = _stack_heads(d["k"][r]).astype(BF16)
        twice = lambda x: jnp.concatenate([x, x], axis=0)
        lhs = jnp.concatenate([twice(d["k_beta"][r]), twice(d["q"][r])], axis=0).astype(BF16)
        gram = _dot_nt(lhs, k_st)
        lkks.append(jnp.where(strict_lower, gram[:n] * decay, 0.0))
        a_qks.append((gram[n:] * decay).astype(BF16))
    t_invs = _unit_lower_inverses(lkks, rows, cols)
    uws = []
    for (ci, b, p), t_inv in zip(systems, t_invs):
        d, r = pairs[b, p], chunk_rows(ci)
        rhs = jnp.concatenate([_stack_heads(d["v_beta"][r]), _stack_heads(d["kb_eg"][r])], axis=1)
        uws.append(_dot(t_inv.astype(BF16), rhs.astype(BF16)))

    states = {(b, p): state_ref[b * n_pairs + p] for b in range(nb) for p in range(n_pairs)}
    for idx, (ci, b, p) in enumerate(systems):
        d, r = pairs[b, p], chunk_rows(ci)
        uw, state = uws[idx], states[b, p]
        g_last = d["g_nat"][ci * c + c - 1:ci * c + c, :]
        k_dec = _stack_heads(d["k"][r] * jnp.exp(g_last - d["g_nat"][r])).astype(BF16)
        wq = _dot(jnp.concatenate([uw[:, n:].astype(BF16), _stack_heads(d["q_dec"][r]).astype(BF16)], axis=0),
                  state.astype(BF16))
        vnb = (uw[:, :n] - wq[:n]).astype(BF16)
        o_st = wq[n:] + _dot(a_qks[idx], vnb)
        states[b, p] = state * jnp.exp(g_last) + _dot_tn(k_dec, vnb)
        y_ref[b, r, d["ls"]] = o_st[:c] + o_st[c:]
    for (b, p), state in states.items():
        state_ref[b * n_pairs + p] = state

    for b in range(nb):
        o = y_ref[b]
        mean_sq = _group_sum(o * o, ones_bd) * (1.0 / HEAD_DIM)
        o_ref[b] = o * lax.rsqrt(mean_sq + RMS_EPS) * ng_ref[...] * _silu(z_ref[b])


def _gdn_mixer(q, k, v, z, gates, layer, conv_w, a_log_rows, dt_bias_rows, norm_g_rows, head_ones):
    bsz, t, width = q.shape
    nh = GROUP_HEADS
    tb = min(GDN_BLOCK, t)
    per = tb // GDN_HALO
    cur = pl.BlockSpec((bsz, tb, width), lambda j: (0, j, 0))
    prev = pl.BlockSpec((bsz, GDN_HALO, width), lambda j: (0, jnp.maximum(j * per - 1, 0), 0))
    return pl.pallas_call(
        _gdn_kernel,
        grid=(t // tb,),
        in_specs=[prev, prev, prev, cur, cur, cur, cur,
                  pl.BlockSpec((bsz, tb, GATE_LANES), lambda j: (0, j, 0)),
                  ] + [_layer_slab(a, layer) for a in (conv_w, a_log_rows, dt_bias_rows, norm_g_rows)]
                 + [_resident((width, width))],
        out_specs=cur,
        out_shape=jax.ShapeDtypeStruct(q.shape, F32),
        scratch_shapes=[pltpu.VMEM((GDN_HALO + tb, width), F32), pltpu.VMEM((bsz, tb, width), F32),
                        pltpu.VMEM((bsz * nh // 2, PAIR_WIDTH, PAIR_WIDTH), F32)],
        compiler_params=_params(("arbitrary",)),
        name="gdn_mixer",
    )(q, k, v, q, k, v, z, gates, conv_w, a_log_rows, dt_bias_rows, norm_g_rows, head_ones)


def _memkv_kernel(m_ref, w_ref, o_ref):
    o_ref[0] = _dot(m_ref[0].astype(BF16), w_ref[...].astype(BF16)).astype(o_ref.dtype)


def _memory_kv(mem, layer, w_kv):
    bsz, m, d = mem.shape
    return pl.pallas_call(
        _memkv_kernel,
        grid=(bsz,),
        in_specs=[pl.BlockSpec((1, m, d), lambda i: (i, 0, 0)), _layer_slab(w_kv, layer)],
        out_specs=pl.BlockSpec((1, m, 2 * d), lambda i: (i, 0, 0)),
        out_shape=jax.ShapeDtypeStruct((bsz, m, 2 * d), BF16),
        compiler_params=_params(("parallel",)),
        name="memory_kv",
    )(mem, w_kv)


def _mix_mem_kernel(x_ref, ya_ref, yb_ref, yc_ref, yd_ref, kv_ref, wout_ref, g1_ref, b1_ref, wq_ref, wo_ref,
                    g2_ref, b2_ref, o_ref, *, sub):
    gw = GROUP_WIDTH
    hd = MEM_HEAD_DIM
    n_sub = x_ref.shape[1] // sub
    tile = lambda i: slice(i * sub, (i + 1) * sub)
    x1, att = [None] * n_sub, [None] * n_sub

    def mix(i):
        tot = None
        for group, y_ref in enumerate((ya_ref, yb_ref, yc_ref, yd_ref)):
            part = _dot(y_ref[0, tile(i), :].astype(BF16), wout_ref[group * gw:(group + 1) * gw, :].astype(BF16))
            tot = part if tot is None else tot + part
        x1[i] = _layer_norm(DN_ALPHA * x_ref[0, tile(i), :] + tot, g1_ref[...], b1_ref[...])

    def attend(i):
        q = _dot(x1[i].astype(BF16), wq_ref[...].astype(BF16))
        outs = []
        for h in range(MEM_HEADS):
            qh = (q[:, h * hd:(h + 1) * hd] * (hd ** -0.5)).astype(BF16)
            kh = kv_ref[0, :, h * hd:(h + 1) * hd]
            vh = kv_ref[0, :, D_MODEL + h * hd:D_MODEL + (h + 1) * hd]
            s = _dot_nt(qh, kh)
            p = jnp.exp(s - jnp.max(s, axis=-1, keepdims=True))
            p = p / jnp.sum(p, axis=-1, keepdims=True)
            outs.append(_dot(p.astype(BF16), vh).astype(BF16))
        att[i] = jnp.concatenate(outs, axis=-1)

    def project(i):
        y = _dot(att[i], wo_ref[...].astype(BF16))
        o_ref[0, tile(i), :] = _layer_norm(DN_ALPHA * x1[i] + y, g2_ref[...], b2_ref[...])

    _skewed(range(n_sub), [mix, attend, project])


def _mix_and_memory_ln(x, ya, yb, yc, yd, kv, layer, w_out, g_mix, b_mix, wq, wo, g_mem, b_mem):
    bsz, t, d = x.shape
    tm = min(MIX_TILE, t)
    row = lambda c: pl.BlockSpec((1, tm, c), lambda i, j: (i, j, 0))
    return pl.pallas_call(
        functools.partial(_mix_mem_kernel, sub=min(ROW_SUBTILE, tm)),
        grid=(bsz, t // tm),
        in_specs=[row(d)] + [row(GROUP_WIDTH)] * 4 + [pl.BlockSpec((1, kv.shape[1], 2 * d), lambda i, j: (i, 0, 0))]
                 + [_layer_slab(a, layer) for a in (w_out, g_mix, b_mix, wq, wo, g_mem, b_mem)],
        out_specs=row(d),
        out_shape=jax.ShapeDtypeStruct(x.shape, F32),
        compiler_params=_params(("parallel", "parallel")),
        name="mix_and_memory_ln",
    )(x, ya, yb, yc, yd, kv, w_out, g_mix, b_mix, wq, wo, g_mem, b_mem)


def _regrouped_in_weights(w_in):
    gw, nh = GROUP_WIDTH, GROUP_HEADS
    w_in = w_in[..., N_LEADING_GROUPS * gw:].astype(BF16)
    o = 0
    gdn_a = w_in[..., o:o + nh]; o += nh
    gdn_b = w_in[..., o:o + nh]; o += nh
    fox_qkv = w_in[..., o:o + 3 * gw]; o += 3 * gw
    fox_f = w_in[..., o:o + nh]; o += nh
    conf = w_in[..., o:o + 2 * gw]; o += 2 * gw
    sb_qkv = w_in[..., o:o + 3 * gw]
    pad = jnp.zeros(w_in.shape[:-1] + (GATE_LANES - 3 * nh,), w_in.dtype)
    return jnp.concatenate([fox_qkv, conf, sb_qkv, gdn_a, gdn_b, fox_f, pad], axis=-1)


def _gate_rows(vals, first_lane):
    depth, nh = vals.shape
    return jnp.pad(vals, ((0, 0), (first_lane, GATE_LANES - first_lane - nh))).reshape(depth, 1, GATE_LANES)


def kernel(x, mem, ffn1_w_gate, ffn1_w_up, ffn1_w_down, ln_ffn1_g, ln_ffn1_b, w_in, gdn_conv_w, gdn_a_log, gdn_dt_bias, gdn_norm_g, fox_b_f, conf_dw_w, conf_dw_b, conf_norm_g, conf_norm_b, w_out, ln_mix_g, ln_mix_b, mem_w_q, mem_w_kv, mem_w_o, ln_mem_g, ln_mem_b, ffn2_w_gate, ffn2_w_up, ffn2_w_down, ln_ffn2_g, ln_ffn2_b):
    rows = lambda a: a.reshape(a.shape[0], 1, a.shape[1])
    ffn1 = (ffn1_w_gate, ffn1_w_up, ffn1_w_down, rows(ln_ffn1_g), rows(ln_ffn1_b))
    ffn2 = (ffn2_w_gate, ffn2_w_up, ffn2_w_down, rows(ln_ffn2_g), rows(ln_ffn2_b))
    w_lead = w_in[..., :N_LEADING_GROUPS * GROUP_WIDTH]
    w_rest = _regrouped_in_weights(w_in)
    gdn = (gdn_conv_w, _gate_rows(gdn_a_log, GATE_A0), _gate_rows(gdn_dt_bias, GATE_A0),
           rows(jnp.tile(gdn_norm_g, (1, GROUP_HEADS))),
           jnp.kron(jnp.eye(GROUP_HEADS, dtype=BF16), jnp.ones((HEAD_DIM, HEAD_DIM), BF16)))
    fox_bias = _gate_rows(fox_b_f, GATE_F0)
    conf = (conf_dw_w, rows(conf_dw_b), rows(conf_norm_g), rows(conf_norm_b))
    mix_mem = (w_out, rows(ln_mix_g), rows(ln_mix_b), mem_w_q, mem_w_o, rows(ln_mem_g), rows(ln_mem_b))
    w_kv = mem_w_kv

    for i in range(DEPTH):
        x = _ffn_ln(x, i, *ffn1)
        gq, gk, gv, gz, fq, fk, fv, glu, sq, sk, sv, gates = _in_projection(x, i, w_lead, w_rest)
        y_a = _gdn_mixer(gq, gk, gv, gz, gates, i, *gdn)
        y_b = _fox_attention(fq, fk, fv, _fox_cumsum(gates, i, fox_bias))
        y_c = _conv_module(glu, i, *conf)
        y_d = _sb_attention(sq, sk, sv)
        x = _mix_and_memory_ln(x, y_a, y_b, y_c, y_d, _memory_kv(mem, i, w_kv), i, *mix_mem)
        x = _ffn_ln(x, i, *ffn2)
    return x
```
